```python
import math
import jax, jax.numpy as jnp
from jax import lax
import numpy as np

D_MODEL = 1024
BATCH = 8
SEQ = 8192
DEPTH = 4

GRID_W = 64
CTX_LEN = 256
ATT_HEADS = 4
ATT_KV_HEADS = 2
ATT_GROUP = ATT_HEADS // ATT_KV_HEADS
ATT_HEAD_DIM = 64
ATT_WIDTH = ATT_HEADS * ATT_HEAD_DIM
ATT_KV_WIDTH = ATT_KV_HEADS * ATT_HEAD_DIM
Q_BLOCK = 128
ROPE_THETA = 10000.0
GLA_HEADS = 4
GLA_DK = 64
GLA_DV = 128
GLA_K_WIDTH = GLA_HEADS * GLA_DK
GLA_V_WIDTH = GLA_HEADS * GLA_DV
GLA_GATE_RANK = 16
GLA_GATE_NORM = 16.0
GLA_CHUNK = 64
S5_GROUPS = 16
S5_GROUP_CH = 16
S5_WIDTH = S5_GROUPS * S5_GROUP_CH
S5_STATE = 64
S5_DT_MIN = 0.001
S5_DT_MAX = 0.1
D_FF = 2816
N_BRANCH = 3
EPS = 1e-6

IN_SPLITS = (ATT_WIDTH, ATT_KV_WIDTH, ATT_KV_WIDTH,
             GLA_K_WIDTH, GLA_K_WIDTH, GLA_V_WIDTH, GLA_V_WIDTH,
             GLA_GATE_RANK, GLA_GATE_RANK,
             S5_WIDTH,
             N_BRANCH * D_MODEL)
D_IN = sum(IN_SPLITS)
IN_SPLIT_POINTS = tuple(int(v) for v in np.cumsum(IN_SPLITS)[:-1])

kernel_name = "hybrid_gla_s5_gqa_prefix_dit"


def rms_norm(x, gain):
    xf = x.astype(jnp.float32)
    xf = xf * lax.rsqrt(jnp.mean(xf * xf, axis=-1, keepdims=True) + EPS)
    return (xf * gain.astype(jnp.float32)).astype(x.dtype)


def modulate(h, shift, scale):
    return h * (1.0 + scale) + shift


def axial_rope_tables(n_tokens):
    rows = n_tokens // GRID_W
    row = jnp.repeat(jnp.arange(rows, dtype=jnp.float32), GRID_W)
    col = jnp.tile(jnp.arange(GRID_W, dtype=jnp.float32), rows)
    n_freq = ATT_HEAD_DIM // 4
    inv_freq = ROPE_THETA ** (-jnp.arange(n_freq, dtype=jnp.float32) / n_freq)
    ang = jnp.stack([row[:, None] * inv_freq, col[:, None] * inv_freq], axis=1)
    return jnp.cos(ang), jnp.sin(ang)


def apply_axial_rope(x, cos, sin):
    bsz, n_t, nh, hd = x.shape
    xr = x.astype(jnp.float32).reshape(bsz, n_t, nh, 2, 2, hd // 4)
    x1, x2 = xr[..., 0, :], xr[..., 1, :]
    cs, sn = cos[None, :, None], sin[None, :, None]
    out = jnp.stack([x1 * cs - x2 * sn, x2 * cs + x1 * sn], axis=-2)
    return out.reshape(bsz, n_t, nh, hd).astype(x.dtype)


def blocked_gqa(q, k, v):
    bsz, n_q = q.shape[0], q.shape[1]
    n_blk = n_q // Q_BLOCK
    qb = q.reshape(bsz, n_blk, Q_BLOCK, ATT_KV_HEADS, ATT_GROUP, ATT_HEAD_DIM)
    qb = jnp.moveaxis(qb, 1, 0)
    scale = ATT_HEAD_DIM ** -0.5

    def attend(q_blk):
        s = jnp.einsum("bqkgd,btkd->bkgqt", q_blk, k, preferred_element_type=jnp.float32) * scale
        p = jax.nn.softmax(s, axis=-1).astype(v.dtype)
        return jnp.einsum("bkgqt,btkd->bqkgd", p, v)

    o = lax.map(attend, qb)
    return jnp.moveaxis(o, 0, 1).reshape(bsz, n_q, ATT_WIDTH)


def gla_chunk_scan(q, k, v, g, s0):
    bsz, nh, n_t, _ = q.shape
    n_c = n_t // GLA_CHUNK

    def chunks(a):
        return a.astype(jnp.float32).reshape(bsz, nh, n_c, GLA_CHUNK, a.shape[-1])

    qc, kc, vc, gc = chunks(q), chunks(k), chunks(v), chunks(g)
    b = jnp.cumsum(gc, axis=3)
    b_end = b[:, :, :, -1:, :]
    q_in = qc * jnp.exp(b)
    k_in = kc * jnp.exp(-b)
    k_end = kc * jnp.exp(b_end - b)
    causal = jnp.tril(jnp.ones((GLA_CHUNK, GLA_CHUNK), dtype=bool))
    att = jnp.where(causal, jnp.einsum("bhnid,bhnjd->bhnij", q_in, k_in), 0.0)
    o_intra = jnp.einsum("bhnij,bhnje->bhnie", att, vc)
    kv_chunk = jnp.einsum("bhnjd,bhnje->nbhde", k_end, vc)
    decay = jnp.moveaxis(jnp.exp(b_end[:, :, :, 0, :]), 2, 0)

    def step(state, inp):
        dec, kv = inp
        return dec[..., None] * state + kv, state

    s_final, s_before = lax.scan(step, s0, (decay, kv_chunk))
    o_inter = jnp.einsum("bhnid,nbhde->bhnie", q_in, s_before)
    o = (o_intra + o_inter).reshape(bsz, nh, n_t, GLA_DV)
    return o.astype(q.dtype), s_final


def gla_bidir(q, k, v, g_f, g_b, s0_f, s0_b):
    o_f, s_f = gla_chunk_scan(q, k, v, g_f, s0_f)
    flip = lambda a: jnp.flip(a, axis=2)
    o_b, s_b = gla_chunk_scan(flip(q), flip(k), flip(v), flip(g_b), s0_b)
    return o_f + flip(o_b), s_f, s_b


def s5_discretize(a_re, a_im, log_dt, b_re, b_im):
    f32 = jnp.float32
    a_re, a_im, b_re, b_im = a_re.astype(f32), a_im.astype(f32), b_re.astype(f32), b_im.astype(f32)
    dt = jnp.exp(log_dt.astype(f32))[:, None]
    mag = jnp.exp(a_re * dt)
    ab_re, ab_im = mag * jnp.cos(a_im * dt), mag * jnp.sin(a_im * dt)
    den = a_re * a_re + a_im * a_im
    f_re = ((ab_re - 1.0) * a_re + ab_im * a_im) / den
    f_im = (ab_im * a_re - (ab_re - 1.0) * a_im) / den
    bb_re = f_re[..., None] * b_re - f_im[..., None] * b_im
    bb_im = f_re[..., None] * b_im + f_im[..., None] * b_re
    return ab_re, ab_im, bb_re, bb_im


def complex_affine_combine(e1, e2):
    a1r, a1i, b1r, b1i = e1
    a2r, a2i, b2r, b2i = e2
    return (a2r * a1r - a2i * a1i, a2r * a1i + a2i * a1r,
            a2r * b1r - a2i * b1i + b2r, a2r * b1i + a2i * b1r + b2i)


def s5_direction(u, lp, d, s0):
    ab_re, ab_im, bb_re, bb_im = s5_discretize(lp["s5_a_re"][d], lp["s5_a_im"][d], lp["s5_log_dt"][d],
                                               lp["s5_b_re"][d], lp["s5_b_im"][d])
    bu_re = jnp.einsum("btgh,gph->btgp", u, bb_re)
    bu_im = jnp.einsum("btgh,gph->btgp", u, bb_im)
    s0_re, s0_im = s0
    bu_re = bu_re.at[:, 0].add(ab_re * s0_re - ab_im * s0_im)
    bu_im = bu_im.at[:, 0].add(ab_re * s0_im + ab_im * s0_re)
    a_re = jnp.broadcast_to(ab_re, bu_re.shape)
    a_im = jnp.broadcast_to(ab_im, bu_im.shape)
    _, _, s_re, s_im = lax.associative_scan(complex_affine_combine, (a_re, a_im, bu_re, bu_im), axis=1)
    c_re = lp["s5_c_re"][d].astype(jnp.float32)
    c_im = lp["s5_c_im"][d].astype(jnp.float32)
    y = jnp.einsum("btgp,ghp->btgh", s_re, c_re) - jnp.einsum("btgp,ghp->btgh", s_im, c_im)
    return y, (s_re[:, -1], s_im[:, -1])


def s5_bidir(u, lp, init_f, init_b):
    f32 = jnp.float32
    bsz, n_t, _ = u.shape
    uf = u.astype(f32)
    ug = uf.reshape(bsz, n_t, S5_GROUPS, S5_GROUP_CH)
    y_f, st_f = s5_direction(ug, lp, 0, init_f)
    y_b, st_b = s5_direction(ug[:, ::-1], lp, 1, init_b)
    y = (y_f + y_b[:, ::-1]).reshape(bsz, n_t, S5_WIDTH) + lp["s5_d"].astype(f32) * uf
    y = jax.nn.gelu(y)
    y = y * jax.nn.sigmoid(y @ lp["s5_glu_w"].astype(f32) + lp["s5_glu_b"].astype(f32))
    return y.astype(u.dtype), st_f, st_b


def mixer_inputs(h, lp, rope):
    bsz, n_t, _ = h.shape
    (aq, ak, av, gq, gk, gv, gr, glf, glb, su, bg) = jnp.split(h @ lp["w_in"], IN_SPLIT_POINTS, axis=-1)
    aq = rms_norm(aq.reshape(bsz, n_t, ATT_HEADS, ATT_HEAD_DIM), lp["q_norm"])
    ak = rms_norm(ak.reshape(bsz, n_t, ATT_KV_HEADS, ATT_HEAD_DIM), lp["k_norm"])
    av = av.reshape(bsz, n_t, ATT_KV_HEADS, ATT_HEAD_DIM)
    if rope is not None:
        aq = apply_axial_rope(aq, rope[0], rope[1])
        ak = apply_axial_rope(ak, rope[0], rope[1])

    def heads(a, dim):
        return a.reshape(bsz, n_t, GLA_HEADS, dim).transpose(0, 2, 1, 3)

    def log_decay(low, d):
        z = (low @ lp["gla_gate_w"][d] + lp["gla_gate_b"][d]).astype(jnp.float32)
        return heads(jax.nn.log_sigmoid(z) / GLA_GATE_NORM, GLA_DK)

    return dict(att_q=aq, att_k=ak, att_v=av,
                gla_q=heads(gq, GLA_DK) * (GLA_DK ** -0.5), gla_k=heads(gk, GLA_DK), gla_v=heads(gv, GLA_DV),
                gla_r=gr, gla_gf=log_decay(glf, 0), gla_gb=log_decay(glb, 1),
                s5_u=su, gates=bg)


def merge_branches(o_att, o_gla, gla_r, o_s5, gates, lp):
    bsz, n_t, _ = o_att.shape
    o_gla = rms_norm(o_gla.transpose(0, 2, 1, 3), lp["gla_out_norm"]).reshape(bsz, n_t, GLA_V_WIDTH)
    o_gla = o_gla * jax.nn.silu(gla_r)
    g_att, g_gla, g_s5 = jnp.split(jax.nn.sigmoid(gates), N_BRANCH, axis=-1)
    merged = (g_att * (o_att @ lp["w_br_att"]) + g_gla * (o_gla @ lp["w_br_gla"])
              + g_s5 * (o_s5 @ lp["w_br_s5"]))
    return merged @ lp["w_out"]


def token_mixer(h_ctx, h_lat, lp, rope, with_ctx_out):
    ic = mixer_inputs(h_ctx, lp, None)
    il = mixer_inputs(h_lat, lp, rope)
    bsz = h_lat.shape[0]
    zero_gla = jnp.zeros((bsz, GLA_HEADS, GLA_DK, GLA_DV), jnp.float32)
    zero_s5 = (jnp.zeros((bsz, S5_GROUPS, S5_STATE), jnp.float32),
               jnp.zeros((bsz, S5_GROUPS, S5_STATE), jnp.float32))
    o_gla_c, gla_st_f, gla_st_b = gla_bidir(ic["gla_q"], ic["gla_k"], ic["gla_v"], ic["gla_gf"], ic["gla_gb"],
                                            zero_gla, zero_gla)
    o_s5_c, s5_st_f, s5_st_b = s5_bidir(ic["s5_u"], lp, zero_s5, zero_s5)
    k_all = jnp.concatenate([ic["att_k"], il["att_k"]], axis=1)
    v_all = jnp.concatenate([ic["att_v"], il["att_v"]], axis=1)
    o_att_l = blocked_gqa(il["att_q"], k_all, v_all)
    o_gla_l, _, _ = gla_bidir(il["gla_q"], il["gla_k"], il["gla_v"], il["gla_gf"], il["gla_gb"],
                              gla_st_f, gla_st_b)
    o_s5_l, _, _ = s5_bidir(il["s5_u"], lp, s5_st_f, s5_st_b)
    y_lat = merge_branches(o_att_l, o_gla_l, il["gla_r"], o_s5_l, il["gates"], lp)
    if not with_ctx_out:
        return None, y_lat
    o_att_c = blocked_gqa(ic["att_q"], ic["att_k"], ic["att_v"])
    y_ctx = merge_branches(o_att_c, o_gla_c, ic["gla_r"], o_s5_c, ic["gates"], lp)
    return y_ctx, y_lat


def dwconv3(x, w, b):
    xp = jnp.pad(x, ((0, 0), (1, 1), (0, 0)))
    return xp[:, :-2] * w[0] + xp[:, 1:-1] * w[1] + xp[:, 2:] * w[2] + b


def conv_ffn(h, lp):
    u = dwconv3(h @ lp["ffn_up"], lp["ffn_conv_w"], lp["ffn_conv_b"])
    a, v = jnp.split(u, 2, axis=-1)
    return (jax.nn.silu(a) * v) @ lp["ffn_down"]


def _fwd_setup_inputs(seed: int = 0) -> dict:
    key = jax.random.key(seed)
    ks = iter(jax.random.split(key, 48))
    f32 = jnp.float32
    L = DEPTH

    def nrm(shape, scale):
        return jax.random.normal(next(ks), shape, f32) * scale

    def gain(shape):
        return 1.0 + nrm(shape, 0.05)

    n_idx = jnp.arange(S5_STATE, dtype=f32)
    s5_shape = (L, 2, S5_GROUPS, S5_STATE)
    return {
        "x": nrm((BATCH, SEQ, D_MODEL), 1.0),
        "c": nrm((BATCH, D_MODEL), 1.0),
        "ctx": nrm((BATCH, CTX_LEN, D_MODEL), 1.0),
        "c_ctx": nrm((D_MODEL,), 1.0),
        "ada_w": nrm((L, D_MODEL, 6 * D_MODEL), D_MODEL ** -0.5),
        "ada_b": nrm((L, 6 * D_MODEL), 0.02),
        "norm_mix_pre": gain((L, D_MODEL)),
        "norm_mix_post": gain((L, D_MODEL)),
        "norm_ffn_pre": gain((L, D_MODEL)),
        "norm_ffn_post": gain((L, D_MODEL)),
        "w_in": nrm((L, D_MODEL, D_IN), D_MODEL ** -0.5),
        "q_norm": gain((L, ATT_HEAD_DIM)),
        "k_norm": gain((L, ATT_HEAD_DIM)),
        "gla_gate_w": nrm((L, 2, GLA_GATE_RANK, GLA_K_WIDTH), GLA_GATE_RANK ** -0.5),
        "gla_gate_b": nrm((L, 2, GLA_K_WIDTH), 0.1),
        "gla_out_norm": gain((L, GLA_DV)),
        "s5_a_re": -0.5 * jnp.exp(nrm(s5_shape, 0.01)),
        "s5_a_im": math.pi * n_idx + nrm(s5_shape, 0.01),
        "s5_log_dt": jax.random.uniform(next(ks), (L, 2, S5_GROUPS), f32,
                                        math.log(S5_DT_MIN), math.log(S5_DT_MAX)),
        "s5_b_re": nrm((L, 2, S5_GROUPS, S5_STATE, S5_GROUP_CH), (2.0 * S5_GROUP_CH) ** -0.5),
        "s5_b_im": nrm((L, 2, S5_GROUPS, S5_STATE, S5_GROUP_CH), (2.0 * S5_GROUP_CH) ** -0.5),
        "s5_c_re": nrm((L, 2, S5_GROUPS, S5_GROUP_CH, S5_STATE), S5_STATE ** -0.5),
        "s5_c_im": nrm((L, 2, S5_GROUPS, S5_GROUP_CH, S5_STATE), S5_STATE ** -0.5),
        "s5_d": nrm((L, S5_WIDTH), 1.0),
        "s5_glu_w": nrm((L, S5_WIDTH, S5_WIDTH), S5_WIDTH ** -0.5),
        "s5_glu_b": nrm((L, S5_WIDTH), 0.02),
        "w_br_att": nrm((L, ATT_WIDTH, D_MODEL), ATT_WIDTH ** -0.5),
        "w_br_gla": nrm((L, GLA_V_WIDTH, D_MODEL), GLA_V_WIDTH ** -0.5),
        "w_br_s5": nrm((L, S5_WIDTH, D_MODEL), S5_WIDTH ** -0.5),
        "w_out": nrm((L, D_MODEL, D_MODEL), D_MODEL ** -0.5),
        "ffn_up": nrm((L, D_MODEL, 2 * D_FF), D_MODEL ** -0.5),
        "ffn_conv_w": nrm((L, 3, 2 * D_FF), 3.0 ** -0.5),
        "ffn_conv_b": nrm((L, 2 * D_FF), 0.02),
        "ffn_down": nrm((L, D_FF, D_MODEL), D_FF ** -0.5),
    }


def _fwd_reference(x, c, ctx, c_ctx, ada_w, ada_b, norm_mix_pre, norm_mix_post, norm_ffn_pre, norm_ffn_post,
              w_in, q_norm, k_norm, gla_gate_w, gla_gate_b, gla_out_norm,
              s5_a_re, s5_a_im, s5_log_dt, s5_b_re, s5_b_im, s5_c_re, s5_c_im, s5_d, s5_glu_w, s5_glu_b,
              w_br_att, w_br_gla, w_br_s5, w_out, ffn_up, ffn_conv_w, ffn_conv_b, ffn_down):
    rope = axial_rope_tables(x.shape[1])
    for i in range(DEPTH):
        last = i == DEPTH - 1
        lp = {
            "w_in": w_in[i], "q_norm": q_norm[i], "k_norm": k_norm[i],
            "gla_gate_w": gla_gate_w[i], "gla_gate_b": gla_gate_b[i], "gla_out_norm": gla_out_norm[i],
            "s5_a_re": s5_a_re[i], "s5_a_im": s5_a_im[i], "s5_log_dt": s5_log_dt[i],
            "s5_b_re": s5_b_re[i], "s5_b_im": s5_b_im[i], "s5_c_re": s5_c_re[i], "s5_c_im": s5_c_im[i],
            "s5_d": s5_d[i], "s5_glu_w": s5_glu_w[i], "s5_glu_b": s5_glu_b[i],
            "w_br_att": w_br_att[i], "w_br_gla": w_br_gla[i], "w_br_s5": w_br_s5[i], "w_out": w_out[i],
            "ffn_up": ffn_up[i], "ffn_conv_w": ffn_conv_w[i], "ffn_conv_b": ffn_conv_b[i],
            "ffn_down": ffn_down[i],
        }
        m_lat = jnp.split((jax.nn.silu(c) @ ada_w[i] + ada_b[i])[:, None, :], 6, axis=-1)
        m_ctx = jnp.split((jax.nn.silu(c_ctx) @ ada_w[i] + ada_b[i])[None, None, :], 6, axis=-1)

        h_lat = modulate(rms_norm(x, norm_mix_pre[i]), m_lat[0], m_lat[1])
        h_ctx = modulate(rms_norm(ctx, norm_mix_pre[i]), m_ctx[0], m_ctx[1])
        y_ctx, y_lat = token_mixer(h_ctx, h_lat, lp, rope, not last)
        x = x + m_lat[2] * rms_norm(y_lat, norm_mix_post[i])
        h_lat = modulate(rms_norm(x, norm_ffn_pre[i]), m_lat[3], m_lat[4])
        x = x + m_lat[5] * rms_norm(conv_ffn(h_lat, lp), norm_ffn_post[i])

        if not last:
            ctx = ctx + m_ctx[2] * rms_norm(y_ctx, norm_mix_post[i])
            h_ctx = modulate(rms_norm(ctx, norm_ffn_pre[i]), m_ctx[3], m_ctx[4])
            ctx = ctx + m_ctx[5] * rms_norm(conv_ffn(h_ctx, lp), norm_ffn_post[i])
    return x


import jax as _jax
import jax.numpy as _jnp

TWIN_FORMAT = 'train_step'
FWD_PARAMS = ['x', 'c', 'ctx', 'c_ctx', 'ada_w', 'ada_b', 'norm_mix_pre', 'norm_mix_post', 'norm_ffn_pre', 'norm_ffn_post', 'w_in', 'q_norm', 'k_norm', 'gla_gate_w', 'gla_gate_b', 'gla_out_norm', 's5_a_re', 's5_a_im', 's5_log_dt', 's5_b_re', 's5_b_im', 's5_c_re', 's5_c_im', 's5_d', 's5_glu_w', 's5_glu_b', 'w_br_att', 'w_br_gla', 'w_br_s5', 'w_out', 'ffn_up', 'ffn_conv_w', 'ffn_conv_b', 'ffn_down']
TWIN_WEIGHTS = ['c_ctx', 'ada_w', 'ada_b', 'norm_mix_pre', 'norm_mix_post', 'norm_ffn_pre', 'norm_ffn_post', 'w_in', 'q_norm', 'k_norm', 'gla_gate_w', 'gla_gate_b', 'gla_out_norm', 's5_a_re', 's5_a_im', 's5_log_dt', 's5_b_re', 's5_b_im', 's5_c_re', 's5_c_im', 's5_d', 's5_glu_w', 's5_glu_b', 'w_br_att', 'w_br_gla', 'w_br_s5', 'w_out', 'ffn_up', 'ffn_conv_w', 'ffn_conv_b', 'ffn_down']
TWIN_DIFF_INPUT = 'x'
TWIN_INPUTS = ['x', 'c', 'ctx', 'c_ctx', 'ada_w', 'ada_b', 'norm_mix_pre', 'norm_mix_post', 'norm_ffn_pre', 'norm_ffn_post', 'w_in', 'q_norm', 'k_norm', 'gla_gate_w', 'gla_gate_b', 'gla_out_norm', 's5_a_re', 's5_a_im', 's5_log_dt', 's5_b_re', 's5_b_im', 's5_c_re', 's5_c_im', 's5_d', 's5_glu_w', 's5_glu_b', 'w_br_att', 'w_br_gla', 'w_br_s5', 'w_out', 'ffn_up', 'ffn_conv_w', 'ffn_conv_b', 'ffn_down', 'loss_target', 'm_c_ctx', 'm_ada_w', 'm_ada_b', 'm_norm_mix_pre', 'm_norm_mix_post', 'm_norm_ffn_pre', 'm_norm_ffn_post', 'm_w_in', 'm_q_norm', 'm_k_norm', 'm_gla_gate_w', 'm_gla_gate_b', 'm_gla_out_norm', 'm_s5_a_re', 'm_s5_a_im', 'm_s5_log_dt', 'm_s5_b_re', 'm_s5_b_im', 'm_s5_c_re', 'm_s5_c_im', 'm_s5_d', 'm_s5_glu_w', 'm_s5_glu_b', 'm_w_br_att', 'm_w_br_gla', 'm_w_br_s5', 'm_w_out', 'm_ffn_up', 'm_ffn_conv_w', 'm_ffn_conv_b', 'm_ffn_down', 'v_c_ctx', 'v_ada_w', 'v_ada_b', 'v_norm_mix_pre', 'v_norm_mix_post', 'v_norm_ffn_pre', 'v_norm_ffn_post', 'v_w_in', 'v_q_norm', 'v_k_norm', 'v_gla_gate_w', 'v_gla_gate_b', 'v_gla_out_norm', 'v_s5_a_re', 'v_s5_a_im', 'v_s5_log_dt', 'v_s5_b_re', 'v_s5_b_im', 'v_s5_c_re', 'v_s5_c_im', 'v_s5_d', 'v_s5_glu_w', 'v_s5_glu_b', 'v_w_br_att', 'v_w_br_gla', 'v_w_br_s5', 'v_w_out', 'v_ffn_up', 'v_ffn_conv_w', 'v_ffn_conv_b', 'v_ffn_down']
TWIN_OUTPUTS = ['loss', 'grad_x', 'grad_c_ctx', 'grad_ada_w', 'grad_ada_b', 'grad_norm_mix_pre', 'grad_norm_mix_post', 'grad_norm_ffn_pre', 'grad_norm_ffn_post', 'grad_w_in', 'grad_q_norm', 'grad_k_norm', 'grad_gla_gate_w', 'grad_gla_gate_b', 'grad_gla_out_norm', 'grad_s5_a_re', 'grad_s5_a_im', 'grad_s5_log_dt', 'grad_s5_b_re', 'grad_s5_b_im', 'grad_s5_c_re', 'grad_s5_c_im', 'grad_s5_d', 'grad_s5_glu_w', 'grad_s5_glu_b', 'grad_w_br_att', 'grad_w_br_gla', 'grad_w_br_s5', 'grad_w_out', 'grad_ffn_up', 'grad_ffn_conv_w', 'grad_ffn_conv_b', 'grad_ffn_down', 'delta_c_ctx', 'delta_ada_w', 'delta_ada_b', 'delta_norm_mix_pre', 'delta_norm_mix_post', 'delta_norm_ffn_pre', 'delta_norm_ffn_post', 'delta_w_in', 'delta_q_norm', 'delta_k_norm', 'delta_gla_gate_w', 'delta_gla_gate_b', 'delta_gla_out_norm', 'delta_s5_a_re', 'delta_s5_a_im', 'delta_s5_log_dt', 'delta_s5_b_re', 'delta_s5_b_im', 'delta_s5_c_re', 'delta_s5_c_im', 'delta_s5_d', 'delta_s5_glu_w', 'delta_s5_glu_b', 'delta_w_br_att', 'delta_w_br_gla', 'delta_w_br_s5', 'delta_w_out', 'delta_ffn_up', 'delta_ffn_conv_w', 'delta_ffn_conv_b', 'delta_ffn_down', 'new_m_c_ctx', 'new_m_ada_w', 'new_m_ada_b', 'new_m_norm_mix_pre', 'new_m_norm_mix_post', 'new_m_norm_ffn_pre', 'new_m_norm_ffn_post', 'new_m_w_in', 'new_m_q_norm', 'new_m_k_norm', 'new_m_gla_gate_w', 'new_m_gla_gate_b', 'new_m_gla_out_norm', 'new_m_s5_a_re', 'new_m_s5_a_im', 'new_m_s5_log_dt', 'new_m_s5_b_re', 'new_m_s5_b_im', 'new_m_s5_c_re', 'new_m_s5_c_im', 'new_m_s5_d', 'new_m_s5_glu_w', 'new_m_s5_glu_b', 'new_m_w_br_att', 'new_m_w_br_gla', 'new_m_w_br_s5', 'new_m_w_out', 'new_m_ffn_up', 'new_m_ffn_conv_w', 'new_m_ffn_conv_b', 'new_m_ffn_down', 'new_v_c_ctx', 'new_v_ada_w', 'new_v_ada_b', 'new_v_norm_mix_pre', 'new_v_norm_mix_post', 'new_v_norm_ffn_pre', 'new_v_norm_ffn_post', 'new_v_w_in', 'new_v_q_norm', 'new_v_k_norm', 'new_v_gla_gate_w', 'new_v_gla_gate_b', 'new_v_gla_out_norm', 'new_v_s5_a_re', 'new_v_s5_a_im', 'new_v_s5_log_dt', 'new_v_s5_b_re', 'new_v_s5_b_im', 'new_v_s5_c_re', 'new_v_s5_c_im', 'new_v_s5_d', 'new_v_s5_glu_w', 'new_v_s5_glu_b', 'new_v_w_br_att', 'new_v_w_br_gla', 'new_v_w_br_s5', 'new_v_w_out', 'new_v_ffn_up', 'new_v_ffn_conv_w', 'new_v_ffn_conv_b', 'new_v_ffn_down']
TWIN_LEAF_KINDS = {'loss': 'loss', 'grad_x': 'grad_x', 'grad_c_ctx': 'grad_w', 'grad_ada_w': 'grad_w', 'grad_ada_b': 'grad_w', 'grad_norm_mix_pre': 'grad_w', 'grad_norm_mix_post': 'grad_w', 'grad_norm_ffn_pre': 'grad_w', 'grad_norm_ffn_post': 'grad_w', 'grad_w_in': 'grad_w', 'grad_q_norm': 'grad_w', 'grad_k_norm': 'grad_w', 'grad_gla_gate_w': 'grad_w', 'grad_gla_gate_b': 'grad_w', 'grad_gla_out_norm': 'grad_w', 'grad_s5_a_re': 'grad_w', 'grad_s5_a_im': 'grad_w', 'grad_s5_log_dt': 'grad_w', 'grad_s5_b_re': 'grad_w', 'grad_s5_b_im': 'grad_w', 'grad_s5_c_re': 'grad_w', 'grad_s5_c_im': 'grad_w', 'grad_s5_d': 'grad_w', 'grad_s5_glu_w': 'grad_w', 'grad_s5_glu_b': 'grad_w', 'grad_w_br_att': 'grad_w', 'grad_w_br_gla': 'grad_w', 'grad_w_br_s5': 'grad_w', 'grad_w_out': 'grad_w', 'grad_ffn_up': 'grad_w', 'grad_ffn_conv_w': 'grad_w', 'grad_ffn_conv_b': 'grad_w', 'grad_ffn_down': 'grad_w', 'delta_c_ctx': 'delta_w', 'delta_ada_w': 'delta_w', 'delta_ada_b': 'delta_w', 'delta_norm_mix_pre': 'delta_w', 'delta_norm_mix_post': 'delta_w', 'delta_norm_ffn_pre': 'delta_w', 'delta_norm_ffn_post': 'delta_w', 'delta_w_in': 'delta_w', 'delta_q_norm': 'delta_w', 'delta_k_norm': 'delta_w', 'delta_gla_gate_w': 'delta_w', 'delta_gla_gate_b': 'delta_w', 'delta_gla_out_norm': 'delta_w', 'delta_s5_a_re': 'delta_w', 'delta_s5_a_im': 'delta_w', 'delta_s5_log_dt': 'delta_w', 'delta_s5_b_re': 'delta_w', 'delta_s5_b_im': 'delta_w', 'delta_s5_c_re': 'delta_w', 'delta_s5_c_im': 'delta_w', 'delta_s5_d': 'delta_w', 'delta_s5_glu_w': 'delta_w', 'delta_s5_glu_b': 'delta_w', 'delta_w_br_att': 'delta_w', 'delta_w_br_gla': 'delta_w', 'delta_w_br_s5': 'delta_w', 'delta_w_out': 'delta_w', 'delta_ffn_up': 'delta_w', 'delta_ffn_conv_w': 'delta_w', 'delta_ffn_conv_b': 'delta_w', 'delta_ffn_down': 'delta_w', 'new_m_c_ctx': 'new_m', 'new_m_ada_w': 'new_m', 'new_m_ada_b': 'new_m', 'new_m_norm_mix_pre': 'new_m', 'new_m_norm_mix_post': 'new_m', 'new_m_norm_ffn_pre': 'new_m', 'new_m_norm_ffn_post': 'new_m', 'new_m_w_in': 'new_m', 'new_m_q_norm': 'new_m', 'new_m_k_norm': 'new_m', 'new_m_gla_gate_w': 'new_m', 'new_m_gla_gate_b': 'new_m', 'new_m_gla_out_norm': 'new_m', 'new_m_s5_a_re': 'new_m', 'new_m_s5_a_im': 'new_m', 'new_m_s5_log_dt': 'new_m', 'new_m_s5_b_re': 'new_m', 'new_m_s5_b_im': 'new_m', 'new_m_s5_c_re': 'new_m', 'new_m_s5_c_im': 'new_m', 'new_m_s5_d': 'new_m', 'new_m_s5_glu_w': 'new_m', 'new_m_s5_glu_b': 'new_m', 'new_m_w_br_att': 'new_m', 'new_m_w_br_gla': 'new_m', 'new_m_w_br_s5': 'new_m', 'new_m_w_out': 'new_m', 'new_m_ffn_up': 'new_m', 'new_m_ffn_conv_w': 'new_m', 'new_m_ffn_conv_b': 'new_m', 'new_m_ffn_down': 'new_m', 'new_v_c_ctx': 'new_v', 'new_v_ada_w': 'new_v', 'new_v_ada_b': 'new_v', 'new_v_norm_mix_pre': 'new_v', 'new_v_norm_mix_post': 'new_v', 'new_v_norm_ffn_pre': 'new_v', 'new_v_norm_ffn_post': 'new_v', 'new_v_w_in': 'new_v', 'new_v_q_norm': 'new_v', 'new_v_k_norm': 'new_v', 'new_v_gla_gate_w': 'new_v', 'new_v_gla_gate_b': 'new_v', 'new_v_gla_out_norm': 'new_v', 'new_v_s5_a_re': 'new_v', 'new_v_s5_a_im': 'new_v', 'new_v_s5_log_dt': 'new_v', 'new_v_s5_b_re': 'new_v', 'new_v_s5_b_im': 'new_v', 'new_v_s5_c_re': 'new_v', 'new_v_s5_c_im': 'new_v', 'new_v_s5_d': 'new_v', 'new_v_s5_glu_w': 'new_v', 'new_v_s5_glu_b': 'new_v', 'new_v_w_br_att': 'new_v', 'new_v_w_br_gla': 'new_v', 'new_v_w_br_s5': 'new_v', 'new_v_w_out': 'new_v', 'new_v_ffn_up': 'new_v', 'new_v_ffn_conv_w': 'new_v', 'new_v_ffn_conv_b': 'new_v', 'new_v_ffn_down': 'new_v'}


def _forward(args):
    return _fwd_reference(*[args[k] for k in FWD_PARAMS])


def _output_shape():
    def fwd():
        inp = _fwd_setup_inputs(0)
        return _fwd_reference(*[inp[k] for k in FWD_PARAMS])
    out = _jax.eval_shape(fwd)
    return out.shape, out.dtype

N_MICROBATCH = 1
ADAM_LR = 0.001
ADAM_B1 = 0.9
ADAM_B2 = 0.999
ADAM_EPS = 1e-08
ADAM_WD = 0.01
ADAM_STEP = 10
PER_EXAMPLE_BATCH_AXIS = {'x': 0, 'c': 0, 'ctx': 0, 'loss_target': 0}
SHARED_INPUTS = []
_WEIGHT_DTYPES = {'c_ctx': _jnp.float32, 'ada_w': _jnp.float32, 'ada_b': _jnp.float32, 'norm_mix_pre': _jnp.float32, 'norm_mix_post': _jnp.float32, 'norm_ffn_pre': _jnp.float32, 'norm_ffn_post': _jnp.float32, 'w_in': _jnp.float32, 'q_norm': _jnp.float32, 'k_norm': _jnp.float32, 'gla_gate_w': _jnp.float32, 'gla_gate_b': _jnp.float32, 'gla_out_norm': _jnp.float32, 's5_a_re': _jnp.float32, 's5_a_im': _jnp.float32, 's5_log_dt': _jnp.float32, 's5_b_re': _jnp.float32, 's5_b_im': _jnp.float32, 's5_c_re': _jnp.float32, 's5_c_im': _jnp.float32, 's5_d': _jnp.float32, 's5_glu_w': _jnp.float32, 's5_glu_b': _jnp.float32, 'w_br_att': _jnp.float32, 'w_br_gla': _jnp.float32, 'w_br_s5': _jnp.float32, 'w_out': _jnp.float32, 'ffn_up': _jnp.float32, 'ffn_conv_w': _jnp.float32, 'ffn_conv_b': _jnp.float32, 'ffn_down': _jnp.float32}
MOMENT_SCALE = {'c_ctx': 1.786478e+00, 'ada_w': 7.248410e+00, 'ada_b': 1.355179e+01, 'norm_mix_pre': 2.972087e+00, 'norm_mix_post': 2.782584e+01, 'norm_ffn_pre': 2.814050e+00, 'norm_ffn_post': 2.697358e+01, 'w_in': 3.204874e+00, 'q_norm': 1.261948e+00, 'k_norm': 1.199926e+00, 'gla_gate_w': 7.629934e-01, 'gla_gate_b': 1.054483e+00, 'gla_out_norm': 5.188803e+00, 's5_a_re': 6.739019e-01, 's5_a_im': 6.841735e-01, 's5_log_dt': 1.724111e+01, 's5_b_re': 4.078043e-01, 's5_b_im': 4.639869e-01, 's5_c_re': 6.016038e-01, 's5_c_im': 6.432093e-01, 's5_d': 6.759586e+00, 's5_glu_w': 1.599110e+00, 's5_glu_b': 2.732935e+00, 'w_br_att': 6.267618e+00, 'w_br_gla': 1.873629e+00, 'w_br_s5': 3.294675e+00, 'w_out': 8.155137e+00, 'ffn_up': 2.421230e+00, 'ffn_conv_w': 2.692111e+00, 'ffn_conv_b': 3.463156e+00, 'ffn_down': 4.776888e+00}


def _to_microbatches(a, axis):
    t = _jnp.moveaxis(a, axis, 0)
    t = t.reshape((N_MICROBATCH, t.shape[0] // N_MICROBATCH) + t.shape[1:])
    return _jnp.moveaxis(t, 1, axis + 1)


def setup_inputs(seed: int = 0) -> dict:
    inp = _fwd_setup_inputs(seed)
    key = _jax.random.fold_in(_jax.random.key(seed), 7919)
    shape, _ = _output_shape()
    out = dict(inp)
    out["loss_target"] = _jax.random.normal(_jax.random.fold_in(key, 0), shape, _jnp.float32)
    for i, name in enumerate(TWIN_WEIGHTS):
        w = inp[name].astype(_jnp.float32)
        if MOMENT_SCALE is None:
            s = _jnp.sqrt(_jnp.mean(_jnp.square(w)) + 1e-30)
        else:
            s = MOMENT_SCALE[name]
        km, kv = _jax.random.split(_jax.random.fold_in(key, i + 1))
        out[name] = w
        out["m_" + name] = s * _jax.random.normal(km, w.shape, _jnp.float32)
        out["v_" + name] = (s * s) * _jax.random.uniform(kv, w.shape, _jnp.float32, 0.5, 1.5)
    if N_MICROBATCH > 1:
        for name, axis in PER_EXAMPLE_BATCH_AXIS.items():
            out[name] = _to_microbatches(out[name], axis)
    return {'x': out['x'], 'c': out['c'], 'ctx': out['ctx'], 'c_ctx': out['c_ctx'], 'ada_w': out['ada_w'], 'ada_b': out['ada_b'], 'norm_mix_pre': out['norm_mix_pre'], 'norm_mix_post': out['norm_mix_post'], 'norm_ffn_pre': out['norm_ffn_pre'], 'norm_ffn_post': out['norm_ffn_post'], 'w_in': out['w_in'], 'q_norm': out['q_norm'], 'k_norm': out['k_norm'], 'gla_gate_w': out['gla_gate_w'], 'gla_gate_b': out['gla_gate_b'], 'gla_out_norm': out['gla_out_norm'], 's5_a_re': out['s5_a_re'], 's5_a_im': out['s5_a_im'], 's5_log_dt': out['s5_log_dt'], 's5_b_re': out['s5_b_re'], 's5_b_im': out['s5_b_im'], 's5_c_re': out['s5_c_re'], 's5_c_im': out['s5_c_im'], 's5_d': out['s5_d'], 's5_glu_w': out['s5_glu_w'], 's5_glu_b': out['s5_glu_b'], 'w_br_att': out['w_br_att'], 'w_br_gla': out['w_br_gla'], 'w_br_s5': out['w_br_s5'], 'w_out': out['w_out'], 'ffn_up': out['ffn_up'], 'ffn_conv_w': out['ffn_conv_w'], 'ffn_conv_b': out['ffn_conv_b'], 'ffn_down': out['ffn_down'], 'loss_target': out['loss_target'], 'm_c_ctx': out['m_c_ctx'], 'm_ada_w': out['m_ada_w'], 'm_ada_b': out['m_ada_b'], 'm_norm_mix_pre': out['m_norm_mix_pre'], 'm_norm_mix_post': out['m_norm_mix_post'], 'm_norm_ffn_pre': out['m_norm_ffn_pre'], 'm_norm_ffn_post': out['m_norm_ffn_post'], 'm_w_in': out['m_w_in'], 'm_q_norm': out['m_q_norm'], 'm_k_norm': out['m_k_norm'], 'm_gla_gate_w': out['m_gla_gate_w'], 'm_gla_gate_b': out['m_gla_gate_b'], 'm_gla_out_norm': out['m_gla_out_norm'], 'm_s5_a_re': out['m_s5_a_re'], 'm_s5_a_im': out['m_s5_a_im'], 'm_s5_log_dt': out['m_s5_log_dt'], 'm_s5_b_re': out['m_s5_b_re'], 'm_s5_b_im': out['m_s5_b_im'], 'm_s5_c_re': out['m_s5_c_re'], 'm_s5_c_im': out['m_s5_c_im'], 'm_s5_d': out['m_s5_d'], 'm_s5_glu_w': out['m_s5_glu_w'], 'm_s5_glu_b': out['m_s5_glu_b'], 'm_w_br_att': out['m_w_br_att'], 'm_w_br_gla': out['m_w_br_gla'], 'm_w_br_s5': out['m_w_br_s5'], 'm_w_out': out['m_w_out'], 'm_ffn_up': out['m_ffn_up'], 'm_ffn_conv_w': out['m_ffn_conv_w'], 'm_ffn_conv_b': out['m_ffn_conv_b'], 'm_ffn_down': out['m_ffn_down'], 'v_c_ctx': out['v_c_ctx'], 'v_ada_w': out['v_ada_w'], 'v_ada_b': out['v_ada_b'], 'v_norm_mix_pre': out['v_norm_mix_pre'], 'v_norm_mix_post': out['v_norm_mix_post'], 'v_norm_ffn_pre': out['v_norm_ffn_pre'], 'v_norm_ffn_post': out['v_norm_ffn_post'], 'v_w_in': out['v_w_in'], 'v_q_norm': out['v_q_norm'], 'v_k_norm': out['v_k_norm'], 'v_gla_gate_w': out['v_gla_gate_w'], 'v_gla_gate_b': out['v_gla_gate_b'], 'v_gla_out_norm': out['v_gla_out_norm'], 'v_s5_a_re': out['v_s5_a_re'], 'v_s5_a_im': out['v_s5_a_im'], 'v_s5_log_dt': out['v_s5_log_dt'], 'v_s5_b_re': out['v_s5_b_re'], 'v_s5_b_im': out['v_s5_b_im'], 'v_s5_c_re': out['v_s5_c_re'], 'v_s5_c_im': out['v_s5_c_im'], 'v_s5_d': out['v_s5_d'], 'v_s5_glu_w': out['v_s5_glu_w'], 'v_s5_glu_b': out['v_s5_glu_b'], 'v_w_br_att': out['v_w_br_att'], 'v_w_br_gla': out['v_w_br_gla'], 'v_w_br_s5': out['v_w_br_s5'], 'v_w_out': out['v_w_out'], 'v_ffn_up': out['v_ffn_up'], 'v_ffn_conv_w': out['v_ffn_conv_w'], 'v_ffn_conv_b': out['v_ffn_conv_b'], 'v_ffn_down': out['v_ffn_down']}


def _loss(weights, diff, rest, loss_target):
    with _jax.named_scope("forward"):
        args = {**rest, TWIN_DIFF_INPUT: diff, **{k: w.astype(_WEIGHT_DTYPES[k]) for k, w in weights.items()}}
        y = _forward(args)
    with _jax.named_scope("loss_head"):
        err = _jnp.square(y.astype(_jnp.float32) - loss_target)
        return 0.5 * _jnp.sum(_jnp.mean(err, axis=-1)) if err.ndim else 0.5 * err


def _adamw(w, g, m, v):
    m = ADAM_B1 * m + (1.0 - ADAM_B1) * g
    v = ADAM_B2 * v + (1.0 - ADAM_B2) * _jnp.square(g)
    m_hat = m / (1.0 - ADAM_B1 ** ADAM_STEP)
    v_hat = v / (1.0 - ADAM_B2 ** ADAM_STEP)
    delta = -ADAM_LR * (m_hat / (_jnp.sqrt(v_hat) + ADAM_EPS) + ADAM_WD * w)
    return delta, m, v


def reference(x, c, ctx, c_ctx, ada_w, ada_b, norm_mix_pre, norm_mix_post, norm_ffn_pre, norm_ffn_post, w_in, q_norm, k_norm, gla_gate_w, gla_gate_b, gla_out_norm, s5_a_re, s5_a_im, s5_log_dt, s5_b_re, s5_b_im, s5_c_re, s5_c_im, s5_d, s5_glu_w, s5_glu_b, w_br_att, w_br_gla, w_br_s5, w_out, ffn_up, ffn_conv_w, ffn_conv_b, ffn_down, loss_target, m_c_ctx, m_ada_w, m_ada_b, m_norm_mix_pre, m_norm_mix_post, m_norm_ffn_pre, m_norm_ffn_post, m_w_in, m_q_norm, m_k_norm, m_gla_gate_w, m_gla_gate_b, m_gla_out_norm, m_s5_a_re, m_s5_a_im, m_s5_log_dt, m_s5_b_re, m_s5_b_im, m_s5_c_re, m_s5_c_im, m_s5_d, m_s5_glu_w, m_s5_glu_b, m_w_br_att, m_w_br_gla, m_w_br_s5, m_w_out, m_ffn_up, m_ffn_conv_w, m_ffn_conv_b, m_ffn_down, v_c_ctx, v_ada_w, v_ada_b, v_norm_mix_pre, v_norm_mix_post, v_norm_ffn_pre, v_norm_ffn_post, v_w_in, v_q_norm, v_k_norm, v_gla_gate_w, v_gla_gate_b, v_gla_out_norm, v_s5_a_re, v_s5_a_im, v_s5_log_dt, v_s5_b_re, v_s5_b_im, v_s5_c_re, v_s5_c_im, v_s5_d, v_s5_glu_w, v_s5_glu_b, v_w_br_att, v_w_br_gla, v_w_br_s5, v_w_out, v_ffn_up, v_ffn_conv_w, v_ffn_conv_b, v_ffn_down):
    given = dict(x=x, c=c, ctx=ctx, c_ctx=c_ctx, ada_w=ada_w, ada_b=ada_b, norm_mix_pre=norm_mix_pre, norm_mix_post=norm_mix_post, norm_ffn_pre=norm_ffn_pre, norm_ffn_post=norm_ffn_post, w_in=w_in, q_norm=q_norm, k_norm=k_norm, gla_gate_w=gla_gate_w, gla_gate_b=gla_gate_b, gla_out_norm=gla_out_norm, s5_a_re=s5_a_re, s5_a_im=s5_a_im, s5_log_dt=s5_log_dt, s5_b_re=s5_b_re, s5_b_im=s5_b_im, s5_c_re=s5_c_re, s5_c_im=s5_c_im, s5_d=s5_d, s5_glu_w=s5_glu_w, s5_glu_b=s5_glu_b, w_br_att=w_br_att, w_br_gla=w_br_gla, w_br_s5=w_br_s5, w_out=w_out, ffn_up=ffn_up, ffn_conv_w=ffn_conv_w, ffn_conv_b=ffn_conv_b, ffn_down=ffn_down, loss_target=loss_target, m_c_ctx=m_c_ctx, m_ada_w=m_ada_w, m_ada_b=m_ada_b, m_norm_mix_pre=m_norm_mix_pre, m_norm_mix_post=m_norm_mix_post, m_norm_ffn_pre=m_norm_ffn_pre, m_norm_ffn_post=m_norm_ffn_post, m_w_in=m_w_in, m_q_norm=m_q_norm, m_k_norm=m_k_norm, m_gla_gate_w=m_gla_gate_w, m_gla_gate_b=m_gla_gate_b, m_gla_out_norm=m_gla_out_norm, m_s5_a_re=m_s5_a_re, m_s5_a_im=m_s5_a_im, m_s5_log_dt=m_s5_log_dt, m_s5_b_re=m_s5_b_re, m_s5_b_im=m_s5_b_im, m_s5_c_re=m_s5_c_re, m_s5_c_im=m_s5_c_im, m_s5_d=m_s5_d, m_s5_glu_w=m_s5_glu_w, m_s5_glu_b=m_s5_glu_b, m_w_br_att=m_w_br_att, m_w_br_gla=m_w_br_gla, m_w_br_s5=m_w_br_s5, m_w_out=m_w_out, m_ffn_up=m_ffn_up, m_ffn_conv_w=m_ffn_conv_w, m_ffn_conv_b=m_ffn_conv_b, m_ffn_down=m_ffn_down, v_c_ctx=v_c_ctx, v_ada_w=v_ada_w, v_ada_b=v_ada_b, v_norm_mix_pre=v_norm_mix_pre, v_norm_mix_post=v_norm_mix_post, v_norm_ffn_pre=v_norm_ffn_pre, v_norm_ffn_post=v_norm_ffn_post, v_w_in=v_w_in, v_q_norm=v_q_norm, v_k_norm=v_k_norm, v_gla_gate_w=v_gla_gate_w, v_gla_gate_b=v_gla_gate_b, v_gla_out_norm=v_gla_out_norm, v_s5_a_re=v_s5_a_re, v_s5_a_im=v_s5_a_im, v_s5_log_dt=v_s5_log_dt, v_s5_b_re=v_s5_b_re, v_s5_b_im=v_s5_b_im, v_s5_c_re=v_s5_c_re, v_s5_c_im=v_s5_c_im, v_s5_d=v_s5_d, v_s5_glu_w=v_s5_glu_w, v_s5_glu_b=v_s5_glu_b, v_w_br_att=v_w_br_att, v_w_br_gla=v_w_br_gla, v_w_br_s5=v_w_br_s5, v_w_out=v_w_out, v_ffn_up=v_ffn_up, v_ffn_conv_w=v_ffn_conv_w, v_ffn_conv_b=v_ffn_conv_b, v_ffn_down=v_ffn_down)
    weights = {n: given[n] for n in TWIN_WEIGHTS}
    shared = {n: given[n] for n in SHARED_INPUTS}
    per_example = {n: given[n] for n in ['x', 'c', 'ctx']}
    grad_fn = _jax.value_and_grad(_loss, argnums=(0, 1))

    def one_microbatch(ex, loss_target):
        ex = dict(ex)
        diff = ex.pop(TWIN_DIFF_INPUT)
        return grad_fn(weights, diff, {**shared, **ex}, loss_target)

    if N_MICROBATCH == 1:
        loss, (grad_w, grad_x) = one_microbatch(per_example, given["loss_target"])
    else:
        def body(carry, xs):
            loss_sum, grad_sum = carry
            l_k, (gw_k, gx_k) = one_microbatch(xs[0], xs[1])
            with _jax.named_scope("update"):
                return (loss_sum + l_k, _jax.tree.map(_jnp.add, grad_sum, gw_k)), gx_k

        init = (_jnp.zeros((), _jnp.float32), _jax.tree.map(_jnp.zeros_like, weights))
        (loss, grad_w), grad_x = _jax.lax.scan(body, init, (per_example, given["loss_target"]))
    with _jax.named_scope("update"):
        delta_w, new_m, new_v = {}, {}, {}
        for n in TWIN_WEIGHTS:
            delta_w[n], new_m[n], new_v[n] = _adamw(weights[n], grad_w[n], given["m_" + n], given["v_" + n])
    return (loss, grad_x, *[grad_w[n] for n in TWIN_WEIGHTS], *[delta_w[n] for n in TWIN_WEIGHTS],
            *[new_m[n] for n in TWIN_WEIGHTS], *[new_v[n] for n in TWIN_WEIGHTS])
```

```python
import functools
import math

import numpy as np
import jax
import jax.numpy as jnp
from jax import lax
from jax.experimental import pallas as pl
from jax.experimental.pallas import tpu as pltpu

F32 = jnp.float32
BF16 = jnp.bfloat16
MESH = pl.DeviceIdType.MESH
N_DEV = 8

D_MODEL = 1024
GRID_W = 64
ATT_HEADS, ATT_KV_HEADS, ATT_HEAD_DIM = 4, 2, 64
ATT_WIDTH, ATT_KV_WIDTH = 256, 128
ROPE_THETA = 10000.0
GLA_HEADS, GLA_DK, GLA_DV = 4, 64, 128
GLA_K_WIDTH, GLA_V_WIDTH = 256, 512
GLA_GATE_RANK, GLA_GATE_NORM, GLA_CHUNK = 16, 16.0, 64
S5_GROUPS, S5_GROUP_CH, S5_WIDTH, S5_STATE = 16, 16, 256, 64
S5_FLAT = S5_GROUPS * S5_STATE
N_BRANCH = 3
EPS = 1e-6
IN_SPLITS = (ATT_WIDTH, ATT_KV_WIDTH, ATT_KV_WIDTH, GLA_K_WIDTH, GLA_K_WIDTH, GLA_V_WIDTH, GLA_V_WIDTH,
             GLA_GATE_RANK, GLA_GATE_RANK, S5_WIDTH, N_BRANCH * D_MODEL)
D_IN = sum(IN_SPLITS)
D_IN_PAD = 5632

ADAM_LR, ADAM_B1, ADAM_B2, ADAM_EPS, ADAM_WD, ADAM_STEP = 0.001, 0.9, 0.999, 1e-08, 0.01, 10

VMEM_LIMIT = 56 * 1024 * 1024
PACK_W = 512
PACK_UNIT = 16 * PACK_W

WEIGHTS = ['c_ctx', 'ada_w', 'ada_b', 'norm_mix_pre', 'norm_mix_post', 'norm_ffn_pre', 'norm_ffn_post', 'w_in',
           'q_norm', 'k_norm', 'gla_gate_w', 'gla_gate_b', 'gla_out_norm', 's5_a_re', 's5_a_im', 's5_log_dt',
           's5_b_re', 's5_b_im', 's5_c_re', 's5_c_im', 's5_d', 's5_glu_w', 's5_glu_b', 'w_br_att', 'w_br_gla',
           'w_br_s5', 'w_out', 'ffn_up', 'ffn_conv_w', 'ffn_conv_b', 'ffn_down']
FWD_INPUTS = ['x', 'c', 'ctx'] + WEIGHTS
BIG = {'ada_w': 1, 'w_in': 1, 'w_br_att': 1, 'w_br_gla': 1, 'w_br_s5': 1, 'w_out': 0, 'ffn_up': 1, 'ffn_down': 0}
SMALL_SHARDED = {'gla_gate_w': 2, 'gla_gate_b': 1, 's5_glu_w': 0, 'ffn_conv_w': 1}
SHARDED = {**BIG, **SMALL_SHARDED}
REPLICATED = [n for n in WEIGHTS if n not in SHARDED]


def _pick(n, cands):
    for cand in cands:
        if n % cand == 0:
            return cand
    return n


def _params(sem):
    return pltpu.CompilerParams(dimension_semantics=sem, vmem_limit_bytes=VMEM_LIMIT)


_DIMS = {"nn": ((1,), (0,)), "nt": ((1,), (1,)), "tn": ((0,), (0,))}


def _mm(a, b, mode, name):
    if mode == "tn":
        K, M = a.shape
    else:
        M, K = a.shape
    N = b.shape[0] if mode == "nt" else b.shape[1]
    tm = _pick(M, (768, 512, 256, 128))
    tn = _pick(N, (512, 256, 128))
    tk = K if K <= 2816 else _pick(K, (1408, 1024, 768, 512, 256, 128))
    nk = K // tk
    dims = (_DIMS[mode], ((), ()))

    def body(a_ref, b_ref, o_ref):
        acc = lax.dot_general(a_ref[...].astype(BF16), b_ref[...].astype(BF16), dims, preferred_element_type=F32)
        if nk == 1:
            o_ref[...] = acc
        else:
            k = pl.program_id(2)

            @pl.when(k == 0)
            def _():
                o_ref[...] = acc

            @pl.when(k > 0)
            def _():
                o_ref[...] += acc

    a_spec = (pl.BlockSpec((tk, tm), lambda i, j, k: (k, i)) if mode == "tn"
              else pl.BlockSpec((tm, tk), lambda i, j, k: (i, k)))
    b_spec = (pl.BlockSpec((tn, tk), lambda i, j, k: (j, k)) if mode == "nt"
              else pl.BlockSpec((tk, tn), lambda i, j, k: (k, j)))
    return pl.pallas_call(
        body, name=name, grid=(M // tm, N // tn, nk),
        in_specs=[a_spec, b_spec], out_specs=pl.BlockSpec((tm, tn), lambda i, j, k: (i, j)),
        out_shape=jax.ShapeDtypeStruct((M, N), F32),
        compiler_params=_params(("parallel", "parallel", "arbitrary")),
    )(a, b)


@jax.custom_vjp
def matmul(a, w):
    return _mm(a, w, "nn", "mm_fwd")


def _matmul_fwd(a, w):
    return _mm(a, w, "nn", "mm_fwd"), (a, w)


def _matmul_bwd(res, dy):
    a, w = res
    return _mm(dy, w, "nt", "mm_dx"), _mm(a, dy, "tn", "mm_dw")


matmul.defvjp(_matmul_fwd, _matmul_bwd)


@jax.custom_vjp
def linear(a, w, w_grad_slot):
    del w_grad_slot
    return _mm(a, w, "nn", "lin_fwd")


def _linear_fwd(a, w, w_grad_slot):
    del w_grad_slot
    return _mm(a, w, "nn", "lin_fwd"), (a, w)


def _linear_bwd(res, dy):
    a, w = res
    return _mm(dy, w, "nt", "lin_dx"), jnp.zeros_like(w), _mm(a, dy, "tn", "lin_dw")


linear.defvjp(_linear_fwd, _linear_bwd)


def _attn_fwd_call(q, k, v, scale):
    H, Tq, d = q.shape
    KV, Tk, _ = k.shape
    G = H // KV
    tq = _pick(Tq, (256, 128, 64))

    def body(q_ref, k_ref, v_ref, o_ref, lse_ref):
        s = lax.dot_general(q_ref[0], k_ref[0], (_DIMS["nt"], ((), ())), preferred_element_type=F32) * scale
        m = jnp.max(s, axis=1, keepdims=True)
        p = jnp.exp(s - m)
        l = jnp.sum(p, axis=1, keepdims=True)
        pn = (p * (1.0 / l)).astype(BF16)
        o_ref[0] = jnp.dot(pn, v_ref[0], preferred_element_type=F32)
        lse_ref[0] = m + jnp.log(l)

    return pl.pallas_call(
        body, name="attn_fwd", grid=(H, Tq // tq),
        in_specs=[pl.BlockSpec((1, tq, d), lambda h, i: (h, i, 0)),
                  pl.BlockSpec((1, Tk, d), lambda h, i: (h // G, 0, 0)),
                  pl.BlockSpec((1, Tk, d), lambda h, i: (h // G, 0, 0))],
        out_specs=[pl.BlockSpec((1, tq, d), lambda h, i: (h, i, 0)),
                   pl.BlockSpec((1, tq, 1), lambda h, i: (h, i, 0))],
        out_shape=[jax.ShapeDtypeStruct((H, Tq, d), F32), jax.ShapeDtypeStruct((H, Tq, 1), F32)],
        compiler_params=_params(("parallel", "parallel")),
    )(q, k, v)


def _attn_bwd_call(q, k, v, o, lse, do, scale):
    H, Tq, d = q.shape
    KV, Tk, _ = k.shape
    G = H // KV
    tq = _pick(Tq, (256, 128, 64))
    ck = _pick(Tk, (1408, 1024, 512, 256, 128, 64))
    nck = Tk // ck

    def body(q_ref, k_ref, v_ref, o_ref, lse_ref, do_ref, dq_ref, dk_ref, dv_ref):
        @pl.when((pl.program_id(1) == 0) & (pl.program_id(2) == 0))
        def _():
            dk_ref[...] = jnp.zeros_like(dk_ref)
            dv_ref[...] = jnp.zeros_like(dv_ref)

        qb = q_ref[0]
        do = do_ref[0]
        dob = do.astype(BF16)
        delta = jnp.sum(do * o_ref[0], axis=1, keepdims=True)
        lse = lse_ref[0]
        dq = jnp.zeros((tq, d), F32)
        for cidx in range(nck):
            rows = slice(cidx * ck, (cidx + 1) * ck)
            ks = k_ref[0, rows, :]
            vs = v_ref[0, rows, :]
            s = lax.dot_general(qb, ks, (_DIMS["nt"], ((), ())), preferred_element_type=F32) * scale
            p = jnp.exp(s - lse)
            dv_ref[0, rows, :] += lax.dot_general(p.astype(BF16), dob, (_DIMS["tn"], ((), ())),
                                                  preferred_element_type=F32)
            dp = lax.dot_general(dob, vs, (_DIMS["nt"], ((), ())), preferred_element_type=F32)
            dsb = (p * (dp - delta) * scale).astype(BF16)
            dq = dq + jnp.dot(dsb, ks, preferred_element_type=F32)
            dk_ref[0, rows, :] += lax.dot_general(dsb, qb, (_DIMS["tn"], ((), ())), preferred_element_type=F32)
        dq_ref[0] = dq

    q_spec = pl.BlockSpec((1, tq, d), lambda kv, g, i: (kv * G + g, i, 0))
    kv_spec = pl.BlockSpec((1, Tk, d), lambda kv, g, i: (kv, 0, 0))
    return pl.pallas_call(
        body, name="attn_bwd", grid=(KV, G, Tq // tq),
        in_specs=[q_spec, kv_spec, kv_spec, q_spec,
                  pl.BlockSpec((1, tq, 1), lambda kv, g, i: (kv * G + g, i, 0)), q_spec],
        out_specs=[q_spec, kv_spec, kv_spec],
        out_shape=[jax.ShapeDtypeStruct((H, Tq, d), F32), jax.ShapeDtypeStruct((KV, Tk, d), F32),
                   jax.ShapeDtypeStruct((KV, Tk, d), F32)],
        compiler_params=_params(("arbitrary", "arbitrary", "arbitrary")),
    )(q, k, v, o, lse, do)


@jax.custom_vjp
def attention(q, k, v):
    return _attn_fwd_call(q.astype(BF16), k.astype(BF16), v.astype(BF16), ATT_HEAD_DIM ** -0.5)[0]


def _attention_fwd(q, k, v):
    qb, kb, vb = q.astype(BF16), k.astype(BF16), v.astype(BF16)
    o, lse = _attn_fwd_call(qb, kb, vb, ATT_HEAD_DIM ** -0.5)
    return o, (qb, kb, vb, o, lse)


def _attention_bwd(res, do):
    qb, kb, vb, o, lse = res
    return _attn_bwd_call(qb, kb, vb, o, lse, do, ATT_HEAD_DIM ** -0.5)


attention.defvjp(_attention_fwd, _attention_bwd)


_ORDER_DOWN = {0: False, 1: True, 2: True, 3: False}
_ORDER_ADJOINT = {0: 2, 1: 3}


def _scan_tables(a_re, a_im, down):
    pw_re, pw_im = [a_re], [a_im]
    for _ in range(7):
        pw_re, pw_im = (pw_re + [pw_re[-1] * a_re - pw_im[-1] * a_im],
                        pw_im + [pw_re[-1] * a_im + pw_im[-1] * a_re])
    carry_rows = list(range(7, -1, -1)) if down else list(range(8))
    rows_re = [pw_re[r] for r in carry_rows] + [pw_re[0], pw_re[1], pw_re[3]]
    rows_im = [pw_im[r] for r in carry_rows] + [pw_im[0], pw_im[1], pw_im[3]]
    tab = jnp.concatenate([jnp.stack(rows_re), jnp.stack(rows_im)], axis=1)
    return jnp.concatenate([tab, jnp.zeros((5, 2 * S5_FLAT), F32)], axis=0)


def _scan_call(bu, a_re, a_im, order, ctx_len):
    T, W2 = bu.shape
    P = W2 // 2
    rb = _pick(math.gcd(ctx_len, T - ctx_len), (256, 128, 64, 32, 16, 8))
    nblk, cb = T // rb, ctx_len // rb
    ntile = rb // 8
    down = _ORDER_DOWN[order]
    tab = _scan_tables(a_re, a_im, down)

    def blk(n):
        if order == 0:
            return n
        if order == 1:
            return jnp.where(n < cb, cb - 1 - n, nblk - 1 - (n - cb))
        if order == 2:
            return nblk - 1 - n
        return jnp.where(n < nblk - cb, cb + n, n - (nblk - cb))

    def body(bu_ref, tab_ref, s_ref, carry_ref):
        @pl.when(pl.program_id(0) == 0)
        def _():
            carry_ref[...] = jnp.zeros_like(carry_ref)

        row = lax.broadcasted_iota(jnp.int32, (8, P), 0)
        cp_re, cp_im = tab_ref[0:8, 0:P], tab_ref[0:8, P:2 * P]
        steps = []
        for j, sh in enumerate((1, 2, 4)):
            keep = (row < 8 - sh) if down else (row >= sh)
            steps.append((8 - sh if down else sh, keep, tab_ref[8 + j:9 + j, 0:P], tab_ref[8 + j:9 + j, P:2 * P]))

        def tile(j, carry):
            c_re, c_im = carry
            t = (ntile - 1 - j) if down else j
            r0 = pl.multiple_of(t * 8, 8)
            x_re = bu_ref[pl.ds(r0, 8), 0:P]
            x_im = bu_ref[pl.ds(r0, 8), P:2 * P]
            for shift, keep, p_re, p_im in steps:
                y_re = jnp.where(keep, pltpu.roll(x_re, shift, 0), 0.0)
                y_im = jnp.where(keep, pltpu.roll(x_im, shift, 0), 0.0)
                x_re, x_im = x_re + p_re * y_re - p_im * y_im, x_im + p_re * y_im + p_im * y_re
            x_re, x_im = x_re + cp_re * c_re - cp_im * c_im, x_im + cp_re * c_im + cp_im * c_re
            s_ref[pl.ds(r0, 8), 0:P] = x_re
            s_ref[pl.ds(r0, 8), P:2 * P] = x_im
            last = 0 if down else 7
            return x_re[last:last + 1, :], x_im[last:last + 1, :]

        c_re, c_im = lax.fori_loop(0, ntile, tile, (carry_ref[0:1, 0:P], carry_ref[0:1, P:2 * P]))
        carry_ref[0:1, 0:P] = c_re
        carry_ref[0:1, P:2 * P] = c_im

    return pl.pallas_call(
        body, name=f"s5_scan_{order}", grid=(nblk,),
        in_specs=[pl.BlockSpec((rb, W2), lambda n: (blk(n), 0)), pl.BlockSpec((16, W2), lambda n: (0, 0))],
        out_specs=pl.BlockSpec((rb, W2), lambda n: (blk(n), 0)),
        out_shape=jax.ShapeDtypeStruct((T, W2), F32),
        scratch_shapes=[pltpu.VMEM((8, W2), F32)],
        compiler_params=_params(("arbitrary",)),
    )(bu, tab)


def _prev_in_order(s, order, ctx_len):
    zero = jnp.zeros_like(s[:1])
    if order == 0:
        return jnp.concatenate([zero, s[:-1]], axis=0)
    return jnp.concatenate([s[1:ctx_len], zero, s[ctx_len + 1:], s[:1]], axis=0)


@functools.partial(jax.custom_vjp, nondiff_argnums=(3, 4))
def s5_scan(bu, a_re, a_im, order, ctx_len):
    return _scan_call(bu, a_re, a_im, order, ctx_len)


def _s5_scan_fwd(bu, a_re, a_im, order, ctx_len):
    s = _scan_call(bu, a_re, a_im, order, ctx_len)
    return s, (s, a_re, a_im)


def _s5_scan_bwd(order, ctx_len, res, ds):
    s, a_re, a_im = res
    lam = _scan_call(ds, a_re, -a_im, _ORDER_ADJOINT[order], ctx_len)
    P = a_re.shape[0]
    sp = _prev_in_order(s, order, ctx_len)
    l_re, l_im, p_re, p_im = lam[:, :P], lam[:, P:], sp[:, :P], sp[:, P:]
    g_re = jnp.sum(l_re * p_re + l_im * p_im, axis=0)
    g_im = jnp.sum(l_im * p_re - l_re * p_im, axis=0)
    return lam, g_re, g_im


s5_scan.defvjp(_s5_scan_fwd, _s5_scan_bwd)


def _dot(a, b, mode, precision=None):
    return lax.dot_general(a, b, (_DIMS[mode], ((), ())), preferred_element_type=F32, precision=precision)


def _gla_chunk_terms(qn, kn, gn, tri):
    b = _dot(tri, gn, "nn", lax.Precision.HIGHEST)
    b_end = b[GLA_CHUNK - 1:GLA_CHUNK, :]
    e_pos, e_neg, e_end = jnp.exp(b), jnp.exp(-b), jnp.exp(b_end - b)
    return b_end, e_pos, e_neg, e_end, qn * e_pos, kn * e_neg, kn * e_end


def _gla_blocking(n_t):
    n_chunks = n_t // GLA_CHUNK
    per_block = _pick(n_chunks, (6, 4, 3, 2, 1))
    return n_chunks, per_block, GLA_CHUNK * per_block, n_chunks // per_block


def _gla_fwd_call(q, k, v, g):
    H, T, dk = q.shape
    dv = v.shape[-1]
    n_chunks, cb, rb, nb = _gla_blocking(T)
    L = GLA_CHUNK

    def body(q_ref, k_ref, v_ref, g_ref, o_ref, sb_ref, s_ref):
        @pl.when(pl.program_id(1) == 0)
        def _():
            s_ref[...] = jnp.zeros_like(s_ref)

        lower = lax.broadcasted_iota(jnp.int32, (L, L), 0) >= lax.broadcasted_iota(jnp.int32, (L, L), 1)
        tri = lower.astype(F32)
        ones = jnp.ones((L, dv), F32)
        state = s_ref[...]
        for n in range(cb):
            rows = slice(n * L, (n + 1) * L)
            qn, kn, vn, gn = q_ref[0, rows, :], k_ref[0, rows, :], v_ref[0, rows, :], g_ref[0, rows, :]
            _, _, _, _, q_in, k_in, k_end = _gla_chunk_terms(qn, kn, gn, tri)
            att = jnp.where(lower, _dot(q_in.astype(BF16), k_in.astype(BF16), "nt"), 0.0)
            vb = vn.astype(BF16)
            sb_ref[0, n] = state
            o_ref[0, rows, :] = (_dot(att.astype(BF16), vb, "nn")
                                 + _dot(q_in.astype(BF16), state.astype(BF16), "nn"))
            decay = jnp.exp(_dot(gn, ones, "tn", lax.Precision.HIGHEST))
            state = decay * state + _dot(k_end.astype(BF16), vb, "tn")
        s_ref[...] = state

    row_k = pl.BlockSpec((1, rb, dk), lambda h, i: (h, i, 0))
    row_v = pl.BlockSpec((1, rb, dv), lambda h, i: (h, i, 0))
    return pl.pallas_call(
        body, name="gla_fwd", grid=(H, nb),
        in_specs=[row_k, row_k, row_v, row_k],
        out_specs=[row_v, pl.BlockSpec((1, cb, dk, dv), lambda h, i: (h, i, 0, 0))],
        out_shape=[jax.ShapeDtypeStruct((H, T, dv), F32), jax.ShapeDtypeStruct((H, n_chunks, dk, dv), F32)],
        scratch_shapes=[pltpu.VMEM((dk, dv), F32)],
        compiler_params=_params(("parallel", "arbitrary")),
    )(q, k, v, g)


def _gla_bwd_call(q, k, v, g, sb, do):
    H, T, dk = q.shape
    dv = v.shape[-1]
    n_chunks, cb, rb, nb = _gla_blocking(T)
    L = GLA_CHUNK

    def body(q_ref, k_ref, v_ref, g_ref, sb_ref, do_ref, dq_ref, dk_ref, dv_ref, dg_ref, ds_ref):
        @pl.when(pl.program_id(1) == 0)
        def _():
            ds_ref[...] = jnp.zeros_like(ds_ref)

        lower = lax.broadcasted_iota(jnp.int32, (L, L), 0) >= lax.broadcasted_iota(jnp.int32, (L, L), 1)
        tri = lower.astype(F32)
        ones = jnp.ones((L, dv), F32)
        ones8 = jnp.ones((8, dv), F32)
        d_state = ds_ref[...]
        for n in reversed(range(cb)):
            rows = slice(n * L, (n + 1) * L)
            qn, kn, vn, gn = q_ref[0, rows, :], k_ref[0, rows, :], v_ref[0, rows, :], g_ref[0, rows, :]
            state = sb_ref[0, n]
            b_end, e_pos, e_neg, e_end, q_in, k_in, k_end = _gla_chunk_terms(qn, kn, gn, tri)
            q_b, k_b, ke_b, vb = q_in.astype(BF16), k_in.astype(BF16), k_end.astype(BF16), vn.astype(BF16)
            dob = do_ref[0, rows, :].astype(BF16)
            dsb = d_state.astype(BF16)
            att = jnp.where(lower, _dot(q_b, k_b, "nt"), 0.0).astype(BF16)
            d_att = jnp.where(lower, _dot(dob, vb, "nt"), 0.0).astype(BF16)
            d_qin = _dot(d_att, k_b, "nn") + _dot(dob, state.astype(BF16), "nt")
            d_kin = _dot(d_att, q_b, "tn")
            d_kend = _dot(vb, dsb, "nt")
            dv_ref[0, rows, :] = _dot(att, dob, "tn") + _dot(ke_b, dsb, "nn")
            through_decay = _dot(ones8, state * d_state, "nt", lax.Precision.HIGHEST)[0:1, :]
            d_bend = jnp.sum(d_kend * k_end, axis=0, keepdims=True) + jnp.exp(b_end) * through_decay
            d_b = d_qin * q_in - d_kin * k_in - d_kend * k_end
            dg_ref[0, rows, :] = _dot(tri, d_b, "tn", lax.Precision.HIGHEST) + d_bend
            dq_ref[0, rows, :] = d_qin * e_pos
            dk_ref[0, rows, :] = d_kin * e_neg + d_kend * e_end
            decay = jnp.exp(_dot(gn, ones, "tn", lax.Precision.HIGHEST))
            d_state = _dot(q_b, dob, "tn") + decay * d_state
        ds_ref[...] = d_state

    row_k = pl.BlockSpec((1, rb, dk), lambda h, i: (h, nb - 1 - i, 0))
    row_v = pl.BlockSpec((1, rb, dv), lambda h, i: (h, nb - 1 - i, 0))
    return pl.pallas_call(
        body, name="gla_bwd", grid=(H, nb),
        in_specs=[row_k, row_k, row_v, row_k, pl.BlockSpec((1, cb, dk, dv), lambda h, i: (h, nb - 1 - i, 0, 0)), row_v],
        out_specs=[row_k, row_k, row_v, row_k],
        out_shape=[jax.ShapeDtypeStruct((H, T, dk), F32), jax.ShapeDtypeStruct((H, T, dk), F32),
                   jax.ShapeDtypeStruct((H, T, dv), F32), jax.ShapeDtypeStruct((H, T, dk), F32)],
        scratch_shapes=[pltpu.VMEM((dk, dv), F32)],
        compiler_params=_params(("parallel", "arbitrary")),
    )(q, k, v, g, sb, do)


@jax.custom_vjp
def gla_scan(q, k, v, g):
    return _gla_fwd_call(q, k, v, g)[0]


def _gla_scan_fwd(q, k, v, g):
    o, sb = _gla_fwd_call(q, k, v, g)
    return o, (q, k, v, g, sb)


def _gla_scan_bwd(res, do):
    q, k, v, g, sb = res
    dq, dk, dv, dg = _gla_bwd_call(q, k, v, g, sb, do)
    return dq, dk, dv, dg


gla_scan.defvjp(_gla_scan_fwd, _gla_scan_bwd)


@functools.partial(jax.custom_vjp, nondiff_argnums=(1,))
def split_cols(x, sizes):
    points = np.cumsum((0,) + tuple(sizes))
    return tuple(x[:, int(a):int(b)] for a, b in zip(points[:-1], points[1:]))


def _split_cols_fwd(x, sizes):
    return split_cols(x, sizes), x.shape[1]


def _split_cols_bwd(sizes, width, cts):
    parts = list(cts)
    if width > sum(sizes):
        parts.append(jnp.zeros((cts[0].shape[0], width - sum(sizes)), cts[0].dtype))
    return (jnp.concatenate(parts, axis=1),)


split_cols.defvjp(_split_cols_fwd, _split_cols_bwd)


def _position():
    return lax.axis_index("x"), lax.axis_index("y"), lax.axis_index("c")


def all_gather_blocks(shard, name):
    R, W = shard.shape

    def body(x_ref, out_ref, send_sems, recv_sems, local_sem):
        x, y, c = _position()
        me, sibling = (x, y, c), (x, y, 1 - c)
        chips = [(1 - x, y), (x, 1 - y), (1 - x, 1 - y)]

        def slot(px, py, pc):
            return out_ref.at[4 * px + 2 * py + pc]

        def copy(k, block, to, src=None):
            return pltpu.make_async_remote_copy(
                src_ref=slot(*block) if src is None else src, dst_ref=slot(*block),
                send_sem=send_sems.at[k], recv_sem=recv_sems.at[k], device_id=to, device_id_type=MESH)

        mine = pltpu.make_async_copy(x_ref, slot(*me), local_sem)
        mine.start()
        first = [copy(0, me, sibling, src=x_ref)]
        first += [copy(1 + j, me, (*chip, c), src=x_ref) for j, chip in enumerate(chips)]
        for cp in first:
            cp.start()
        passed = [copy(4 + j, (*chip, c), sibling) for j, chip in enumerate(chips)]
        for j, chip in enumerate(chips):
            copy(1 + j, (*chip, c), me).wait_recv()
            passed[j].start()
        copy(0, sibling, me).wait_recv()
        for j, chip in enumerate(chips):
            copy(4 + j, (*chip, 1 - c), me).wait_recv()
        for cp in first + passed:
            cp.wait_send()
        mine.wait()

    return pl.pallas_call(
        body, name=name,
        out_shape=jax.ShapeDtypeStruct((N_DEV, R, W), shard.dtype),
        in_specs=[pl.BlockSpec(memory_space=pltpu.HBM)], out_specs=pl.BlockSpec(memory_space=pltpu.HBM),
        scratch_shapes=[pltpu.SemaphoreType.DMA((7,)), pltpu.SemaphoreType.DMA((7,)), pltpu.SemaphoreType.DMA],
    )(shard)


def exchange_blocks(blocks, name):
    _, R, W = blocks.shape
    flips = [(fx, fy, fc) for fx in (0, 1) for fy in (0, 1) for fc in (0, 1)][1:]

    def body(x_ref, out_ref, send_sems, recv_sems, local_sem):
        x, y, c = _position()
        me = 4 * x + 2 * y + c
        mine = pltpu.make_async_copy(x_ref.at[me], out_ref.at[me], local_sem)
        mine.start()
        copies = []
        for k, (fx, fy, fc) in enumerate(flips):
            px, py, pc = x ^ fx, y ^ fy, c ^ fc
            peer = 4 * px + 2 * py + pc
            copies.append((
                pltpu.make_async_remote_copy(src_ref=x_ref.at[peer], dst_ref=out_ref.at[me],
                                             send_sem=send_sems.at[k], recv_sem=recv_sems.at[k],
                                             device_id=(px, py, pc), device_id_type=MESH),
                pltpu.make_async_remote_copy(src_ref=x_ref.at[peer], dst_ref=out_ref.at[peer],
                                             send_sem=send_sems.at[k], recv_sem=recv_sems.at[k],
                                             device_id=(px, py, pc), device_id_type=MESH)))
        for send, _ in copies:
            send.start()
        for _, recv in copies:
            recv.wait_recv()
        for send, _ in copies:
            send.wait_send()
        mine.wait()

    return pl.pallas_call(
        body, name=name,
        out_shape=jax.ShapeDtypeStruct(blocks.shape, blocks.dtype),
        in_specs=[pl.BlockSpec(memory_space=pltpu.HBM)], out_specs=pl.BlockSpec(memory_space=pltpu.HBM),
        scratch_shapes=[pltpu.SemaphoreType.DMA((7,)), pltpu.SemaphoreType.DMA((7,)), pltpu.SemaphoreType.DMA],
    )(blocks)


def sum_adamw(parts, w, m, v, name):
    _, R, W = parts.shape
    tr = _pick(R, (512, 256, 128, 64, 32, 16, 8))

    def body(p_ref, w_ref, m_ref, v_ref, g_out, d_out, m_out, v_out):
        g = p_ref[0]
        for j in range(1, N_DEV):
            g = g + p_ref[j]
        m_new = ADAM_B1 * m_ref[...] + (1.0 - ADAM_B1) * g
        v_new = ADAM_B2 * v_ref[...] + (1.0 - ADAM_B2) * (g * g)
        m_hat = m_new / (1.0 - ADAM_B1 ** ADAM_STEP)
        v_hat = v_new / (1.0 - ADAM_B2 ** ADAM_STEP)
        g_out[...] = g
        d_out[...] = -ADAM_LR * (m_hat / (jnp.sqrt(v_hat) + ADAM_EPS) + ADAM_WD * w_ref[...])
        m_out[...] = m_new
        v_out[...] = v_new

    row = pl.BlockSpec((tr, W), lambda i: (i, 0))
    return pl.pallas_call(
        body, name=name, grid=(R // tr,),
        in_specs=[pl.BlockSpec((N_DEV, tr, W), lambda i: (0, i, 0)), row, row, row],
        out_specs=[row, row, row, row],
        out_shape=[jax.ShapeDtypeStruct((R, W), F32)] * 4,
        compiler_params=_params(("parallel",)),
    )(parts, w, m, v)


def _padded(n):
    return -(-n // PACK_UNIT) * PACK_UNIT


def _pack(arrays, dtype):
    segs = []
    for arr in arrays:
        flat = arr.reshape(-1).astype(dtype)
        segs.append(jnp.pad(flat, (0, _padded(flat.size) - flat.size)))
    return jnp.concatenate(segs).reshape(-1, PACK_W)


def _pack_blocks(arrays, dtype):
    segs = []
    for arr in arrays:
        flat = arr.reshape(N_DEV, -1).astype(dtype)
        segs.append(jnp.pad(flat, ((0, 0), (0, _padded(flat.shape[1]) - flat.shape[1]))))
    return jnp.concatenate(segs, axis=1).reshape(N_DEV, -1, PACK_W)


def _unpack(buf, shapes):
    lead = buf.shape[:-2]
    flat = buf.reshape(*lead, -1)
    out, off = [], 0
    for shape in shapes:
        n = int(np.prod(shape))
        out.append(flat[..., off:off + n].reshape(*lead, *shape))
        off += _padded(n)
    return out


def _round_up(n, unit):
    return -(-n // unit) * unit


def _pack_rows(arrays, width, dtype):
    parts = []
    for arr in arrays:
        r, c = arr.shape[-2:]
        pad = [(0, 0)] * (arr.ndim - 2) + [(0, _round_up(r, 16) - r), (0, width - c)]
        parts.append(jnp.pad(arr.astype(dtype), pad))
    return jnp.concatenate(parts, axis=-2)


def _unpack_rows(buf, shapes):
    out, off = [], 0
    for r, c in shapes:
        out.append(buf[..., off:off + r, :c])
        off += _round_up(r, 16)
    return out


def _to_full(blocks, axis):
    moved = jnp.moveaxis(blocks, 0, axis)
    shape = list(moved.shape)
    shape[axis:axis + 2] = [shape[axis] * shape[axis + 1]]
    return moved.reshape(shape)


def _to_blocks(full, axis):
    shape = list(full.shape)
    shape[axis:axis + 1] = [N_DEV, shape[axis] // N_DEV]
    return jnp.moveaxis(full.reshape(shape), axis, 0)


def rms_norm(x, gain):
    return x * lax.rsqrt(jnp.mean(x * x, axis=-1, keepdims=True) + EPS) * gain


def _rope_tables(n_tokens):
    rows = n_tokens // GRID_W
    row = jnp.repeat(jnp.arange(rows, dtype=F32), GRID_W)
    col = jnp.tile(jnp.arange(GRID_W, dtype=F32), rows)
    n_freq = ATT_HEAD_DIM // 4
    inv_freq = ROPE_THETA ** (-jnp.arange(n_freq, dtype=F32) / n_freq)
    ang = jnp.stack([row[:, None] * inv_freq, col[:, None] * inv_freq], axis=1)
    return jnp.cos(ang), jnp.sin(ang)


def _rope(x, cos, sin):
    n_t, nh, hd = x.shape
    xr = x.reshape(n_t, nh, 2, 2, hd // 4)
    x1, x2 = xr[..., 0, :], xr[..., 1, :]
    cs, sn = cos[:, None], sin[:, None]
    return jnp.stack([x1 * cs - x2 * sn, x2 * cs + x1 * sn], axis=-2).reshape(n_t, nh, hd)


W_IN_ORDER = (0, 1, 2, 3, 4, 5, 6, 9, 10, 7, 8)
W_IN_SIZES = tuple(IN_SPLITS[s] for s in W_IN_ORDER)


def _w_in_reorder(w, to_kernel_order):
    if to_kernel_order:
        points = np.cumsum((0,) + IN_SPLITS)
        pieces = [w[..., int(points[s]):int(points[s + 1])] for s in W_IN_ORDER]
    else:
        points = np.cumsum((0,) + W_IN_SIZES)
        where = {s: j for j, s in enumerate(W_IN_ORDER)}
        pieces = [w[..., int(points[where[s]]):int(points[where[s] + 1])] for s in range(len(IN_SPLITS))]
    return jnp.concatenate(pieces, axis=-1)


def _s5_discretize(a_re, a_im, log_dt, b_re, b_im):
    dt = jnp.exp(log_dt)[:, None]
    mag = jnp.exp(a_re * dt)
    ab_re, ab_im = mag * jnp.cos(a_im * dt), mag * jnp.sin(a_im * dt)
    den = a_re * a_re + a_im * a_im
    f_re = ((ab_re - 1.0) * a_re + ab_im * a_im) / den
    f_im = (ab_im * a_re - (ab_re - 1.0) * a_im) / den
    bb_re = f_re[..., None] * b_re - f_im[..., None] * b_im
    bb_im = f_re[..., None] * b_im + f_im[..., None] * b_re
    return ab_re, ab_im, bb_re, bb_im


def _s5_direction(u, lp, d, ctx_len):
    ab_re, ab_im, bb_re, bb_im = _s5_discretize(lp["s5_a_re"][d], lp["s5_a_im"][d], lp["s5_log_dt"][d],
                                                lp["s5_b_re"][d], lp["s5_b_im"][d])
    eye = jnp.eye(S5_GROUPS, dtype=F32)
    b_cat = jnp.concatenate([jnp.einsum("gph,gk->ghkp", bb_re, eye).reshape(S5_WIDTH, S5_FLAT),
                             jnp.einsum("gph,gk->ghkp", bb_im, eye).reshape(S5_WIDTH, S5_FLAT)], axis=1)
    c_cat = jnp.concatenate([jnp.einsum("ghp,gk->gpkh", lp["s5_c_re"][d], eye).reshape(S5_FLAT, S5_WIDTH),
                             -jnp.einsum("ghp,gk->gpkh", lp["s5_c_im"][d], eye).reshape(S5_FLAT, S5_WIDTH)], axis=0)
    bu = matmul(u, b_cat)
    s = s5_scan(bu, ab_re.reshape(-1), ab_im.reshape(-1), d, ctx_len)
    return matmul(s, c_cat)


def _s5_branch(u, lp, ctx_len):
    y = _s5_direction(u, lp, 0, ctx_len) + _s5_direction(u, lp, 1, ctx_len) + lp["s5_d"] * u
    y = jax.nn.gelu(y)
    return y * jax.nn.sigmoid(matmul(y, lp["s5_glu_w"]) + lp["s5_glu_b"])


def _flip_segments(a, ctx_len):
    return jnp.concatenate([a[:, :ctx_len][:, ::-1], a[:, ctx_len:][:, ::-1]], axis=1)


def _heads(a, nh):
    return a.reshape(a.shape[0], nh, a.shape[1] // nh).transpose(1, 0, 2)


def _token_mixer(h, lp, rope, ctx_len, with_ctx_out):
    n_t = h.shape[0]
    proj = linear(h, lp["w_in"], lp["w_in_slot"])
    aq, ak, av, gq, gk, gv, gr, su, bg, glf, glb = split_cols(proj, W_IN_SIZES)

    aq = rms_norm(aq.reshape(n_t, ATT_HEADS, ATT_HEAD_DIM), lp["q_norm"])
    ak = rms_norm(ak.reshape(n_t, ATT_KV_HEADS, ATT_HEAD_DIM), lp["k_norm"])
    aq = jnp.concatenate([aq[:ctx_len], _rope(aq[ctx_len:], *rope)], axis=0).transpose(1, 0, 2)
    ak = jnp.concatenate([ak[:ctx_len], _rope(ak[ctx_len:], *rope)], axis=0).transpose(1, 0, 2)
    av = av.reshape(n_t, ATT_KV_HEADS, ATT_HEAD_DIM).transpose(1, 0, 2)
    o_att_lat = attention(aq[:, ctx_len:], ak, av)
    if with_ctx_out:
        o_att_ctx = attention(aq[:, :ctx_len], ak[:, :ctx_len], av[:, :ctx_len])
        o_att = jnp.concatenate([o_att_ctx, o_att_lat], axis=1)
    else:
        o_att = o_att_lat
    o_att = o_att.transpose(1, 0, 2).reshape(-1, ATT_WIDTH)

    def log_decay(low, d):
        z = jnp.dot(low, lp["gla_gate_w"][d]) + lp["gla_gate_b"][d]
        return _heads(jax.nn.log_sigmoid(z) / GLA_GATE_NORM, GLA_HEADS)

    q_g, k_g, v_g = _heads(gq, GLA_HEADS) * (GLA_DK ** -0.5), _heads(gk, GLA_HEADS), _heads(gv, GLA_HEADS)
    o_f = gla_scan(q_g, k_g, v_g, log_decay(glf, 0))
    flip = functools.partial(_flip_segments, ctx_len=ctx_len)
    o_b = flip(gla_scan(flip(q_g), flip(k_g), flip(v_g), flip(log_decay(glb, 1))))
    o_gla = rms_norm((o_f + o_b).transpose(1, 0, 2), lp["gla_out_norm"]).reshape(n_t, GLA_V_WIDTH)
    o_gla = o_gla * jax.nn.silu(gr)

    o_s5 = _s5_branch(su, lp, ctx_len)

    if not with_ctx_out:
        o_gla, o_s5, bg = o_gla[ctx_len:], o_s5[ctx_len:], bg[ctx_len:]
    g_att, g_gla, g_s5 = split_cols(jax.nn.sigmoid(bg), (D_MODEL,) * N_BRANCH)
    merged = (g_att * linear(o_att, lp["w_br_att"], lp["w_br_att_slot"])
              + g_gla * linear(o_gla, lp["w_br_gla"], lp["w_br_gla_slot"])
              + g_s5 * linear(o_s5, lp["w_br_s5"], lp["w_br_s5_slot"]))
    return linear(merged, lp["w_out"], lp["w_out_slot"])


def _dwconv3(x, w, b, starts):
    n = x.shape[0]
    idx = jnp.arange(n)[:, None]
    first = functools.reduce(jnp.logical_or, [idx == s for s in starts])
    last = functools.reduce(jnp.logical_or, [idx == (e - 1) for e in list(starts[1:]) + [n]])
    prev = jnp.where(first, 0.0, jnp.concatenate([jnp.zeros_like(x[:1]), x[:-1]], axis=0))
    nxt = jnp.where(last, 0.0, jnp.concatenate([x[1:], jnp.zeros_like(x[:1])], axis=0))
    return prev * w[0] + x * w[1] + nxt * w[2] + b


def _conv_ffn(h, lp, starts):
    u = _dwconv3(linear(h, lp["ffn_up"], lp["ffn_up_slot"]), lp["ffn_conv_w"], lp["ffn_conv_b"], starts)
    a, v = split_cols(u, (u.shape[1] // 2,) * 2)
    return linear(jax.nn.silu(a) * v, lp["ffn_down"], lp["ffn_down_slot"])


def _rows(ctx_val, lat_val, ctx_len, n_lat, with_ctx):
    lat = jnp.broadcast_to(lat_val, (n_lat, lat_val.shape[-1]))
    if not with_ctx:
        return lat
    return jnp.concatenate([jnp.broadcast_to(ctx_val, (ctx_len, ctx_val.shape[-1])), lat], axis=0)


def _local_loss(diff, fixed):
    p = {**fixed, **diff}
    x, ctx = p["x"][0], p["ctx"][0]
    n_lat, ctx_len = x.shape[0], ctx.shape[0]
    depth = p["ada_b"].shape[0]
    rope = _rope_tables(n_lat)
    rows = jnp.concatenate([ctx, x], axis=0)
    cond = jnp.zeros((16, D_MODEL), F32).at[0].set(jax.nn.silu(p["c"][0])).at[1].set(jax.nn.silu(p["c_ctx"]))
    layer_names = [n for n in WEIGHTS if n != "c_ctx"]
    for i in range(depth):
        last = i == depth - 1
        lp = {n: p[n][i] for n in layer_names}
        lp.update({n + "_slot": p[n + "_slot"][i] for n in BIG})
        mod = linear(cond, lp["ada_w"], lp["ada_w_slot"]) + lp["ada_b"]
        m_lat, m_ctx = jnp.split(mod[0:1], 6, axis=-1), jnp.split(mod[1:2], 6, axis=-1)
        both = functools.partial(_rows, ctx_len=ctx_len, n_lat=n_lat, with_ctx=True)

        h = rms_norm(rows, lp["norm_mix_pre"]) * (1.0 + both(m_ctx[1], m_lat[1])) + both(m_ctx[0], m_lat[0])
        y = _token_mixer(h, lp, rope, ctx_len, not last)
        if last:
            rows = rows[ctx_len:]
        cur = functools.partial(_rows, ctx_len=ctx_len, n_lat=n_lat, with_ctx=not last)
        starts = [0] if last else [0, ctx_len]
        rows = rows + cur(m_ctx[2], m_lat[2]) * rms_norm(y, lp["norm_mix_post"])
        h = rms_norm(rows, lp["norm_ffn_pre"]) * (1.0 + cur(m_ctx[4], m_lat[4])) + cur(m_ctx[3], m_lat[3])
        rows = rows + cur(m_ctx[5], m_lat[5]) * rms_norm(_conv_ffn(h, lp, starts), lp["norm_ffn_post"])
    err = jnp.square(rows - p["loss_target"][0])
    return 0.5 * jnp.sum(jnp.mean(err, axis=-1))


def kernel(x, c, ctx, c_ctx, ada_w, ada_b, norm_mix_pre, norm_mix_post, norm_ffn_pre, norm_ffn_post, w_in, q_norm, k_norm, gla_gate_w, gla_gate_b, gla_out_norm, s5_a_re, s5_a_im, s5_log_dt, s5_b_re, s5_b_im, s5_c_re, s5_c_im, s5_d, s5_glu_w, s5_glu_b, w_br_att, w_br_gla, w_br_s5, w_out, ffn_up, ffn_conv_w, ffn_conv_b, ffn_down, loss_target, m_c_ctx, m_ada_w, m_ada_b, m_norm_mix_pre, m_norm_mix_post, m_norm_ffn_pre, m_norm_ffn_post, m_w_in, m_q_norm, m_k_norm, m_gla_gate_w, m_gla_gate_b, m_gla_out_norm, m_s5_a_re, m_s5_a_im, m_s5_log_dt, m_s5_b_re, m_s5_b_im, m_s5_c_re, m_s5_c_im, m_s5_d, m_s5_glu_w, m_s5_glu_b, m_w_br_att, m_w_br_gla, m_w_br_s5, m_w_out, m_ffn_up, m_ffn_conv_w, m_ffn_conv_b, m_ffn_down, v_c_ctx, v_ada_w, v_ada_b, v_norm_mix_pre, v_norm_mix_post, v_norm_ffn_pre, v_norm_ffn_post, v_w_in, v_q_norm, v_k_norm, v_gla_gate_w, v_gla_gate_b, v_gla_out_norm, v_s5_a_re, v_s5_a_im, v_s5_log_dt, v_s5_b_re, v_s5_b_im, v_s5_c_re, v_s5_c_im, v_s5_d, v_s5_glu_w, v_s5_glu_b, v_w_br_att, v_w_br_gla, v_w_br_s5, v_w_out, v_ffn_up, v_ffn_conv_w, v_ffn_conv_b, v_ffn_down):
    args = (x, c, ctx, c_ctx, ada_w, ada_b, norm_mix_pre, norm_mix_post, norm_ffn_pre, norm_ffn_post, w_in, q_norm, k_norm, gla_gate_w, gla_gate_b, gla_out_norm, s5_a_re, s5_a_im, s5_log_dt, s5_b_re, s5_b_im, s5_c_re, s5_c_im, s5_d, s5_glu_w, s5_glu_b, w_br_att, w_br_gla, w_br_s5, w_out, ffn_up, ffn_conv_w, ffn_conv_b, ffn_down)
    given = dict(zip(FWD_INPUTS, args))
    given["loss_target"] = loss_target
    m_in = dict(zip(WEIGHTS, (m_c_ctx, m_ada_w, m_ada_b, m_norm_mix_pre, m_norm_mix_post, m_norm_ffn_pre, m_norm_ffn_post, m_w_in, m_q_norm, m_k_norm, m_gla_gate_w, m_gla_gate_b, m_gla_out_norm, m_s5_a_re, m_s5_a_im, m_s5_log_dt, m_s5_b_re, m_s5_b_im, m_s5_c_re, m_s5_c_im, m_s5_d, m_s5_glu_w, m_s5_glu_b, m_w_br_att, m_w_br_gla, m_w_br_s5, m_w_out, m_ffn_up, m_ffn_conv_w, m_ffn_conv_b, m_ffn_down)))
    v_in = dict(zip(WEIGHTS, (v_c_ctx, v_ada_w, v_ada_b, v_norm_mix_pre, v_norm_mix_post, v_norm_ffn_pre, v_norm_ffn_post, v_w_in, v_q_norm, v_k_norm, v_gla_gate_w, v_gla_gate_b, v_gla_out_norm, v_s5_a_re, v_s5_a_im, v_s5_log_dt, v_s5_b_re, v_s5_b_im, v_s5_c_re, v_s5_c_im, v_s5_d, v_s5_glu_w, v_s5_glu_b, v_w_br_att, v_w_br_gla, v_w_br_s5, v_w_out, v_ffn_up, v_ffn_conv_w, v_ffn_conv_b, v_ffn_down)))
    depth = ada_b.shape[0]
    big_names, small_names = list(BIG), list(SMALL_SHARDED)
    sharded_names = big_names + small_names

    full = {}
    groups = {}
    for n in big_names:
        groups.setdefault(_round_up(given[n].shape[2], 128), []).append(n)
    per_layer = {n: [] for n in big_names}
    for i in range(depth):
        for width, names in groups.items():
            gathered = all_gather_blocks(_pack_rows([given[n][i] for n in names], width, BF16),
                                         f"gather_weights_{width}")
            for n, blocks in zip(names, _unpack_rows(gathered, [given[n].shape[1:] for n in names])):
                per_layer[n].append(_to_full(blocks, BIG[n]))
    for n in big_names:
        full[n] = jnp.stack(per_layer[n])
    small_shapes = [given[n].shape for n in small_names]
    gathered = all_gather_blocks(_pack([given[n] for n in small_names], F32), "gather_small")
    for n, blocks in zip(small_names, _unpack(gathered, small_shapes)):
        full[n] = _to_full(blocks, SMALL_SHARDED[n] + 1)

    w_in_full = jnp.pad(_w_in_reorder(full["w_in"], True), ((0, 0), (0, 0), (0, D_IN_PAD - D_IN)))

    diff = {"x": x}
    diff.update({n: given[n] for n in REPLICATED})
    diff.update({n: full[n] for n in small_names})
    fixed = {"c": c, "ctx": ctx, "loss_target": loss_target}
    for n in big_names:
        fixed[n] = w_in_full if n == "w_in" else full[n]
        diff[n + "_slot"] = jnp.zeros(fixed[n].shape, F32)
    loss_local, grads = jax.value_and_grad(_local_loss)(diff, fixed)
    loss = lax.psum(loss_local, ("x", "y", "c"))

    g_full = {n: grads[n] for n in small_names}
    for n in big_names:
        g_full[n] = grads[n + "_slot"]
    g_full["w_in"] = _w_in_reorder(g_full["w_in"], False)

    out_g, out_d, out_m, out_v = {}, {}, {}, {}
    layer_out = {n: ([], [], [], []) for n in big_names}
    for i in range(depth):
        for width, names in groups.items():
            send = _pack_rows([_to_blocks(g_full[n][i], BIG[n]) for n in names], width, F32)
            parts = exchange_blocks(send, f"exchange_grads_{width}")
            results = sum_adamw(parts, _pack_rows([given[n][i] for n in names], width, F32),
                                _pack_rows([m_in[n][i] for n in names], width, F32),
                                _pack_rows([v_in[n][i] for n in names], width, F32), f"adamw_{width}")
            for k, res in enumerate(results):
                for n, arr in zip(names, _unpack_rows(res, [given[n].shape[1:] for n in names])):
                    layer_out[n][k].append(arr)
    for n in big_names:
        out_g[n], out_d[n], out_m[n], out_v[n] = (jnp.stack(parts_k) for parts_k in layer_out[n])
    send = _pack_blocks([_to_blocks(g_full[n], SMALL_SHARDED[n] + 1) for n in small_names], F32)
    parts = exchange_blocks(send, "exchange_small")
    results = sum_adamw(parts, _pack([given[n] for n in small_names], F32), _pack([m_in[n] for n in small_names], F32),
                        _pack([v_in[n] for n in small_names], F32), "adamw_small")
    for store, res in zip((out_g, out_d, out_m, out_v), results):
        for n, arr in zip(small_names, _unpack(res, small_shapes)):
            store[n] = arr

    rep_shapes = [given[n].shape for n in REPLICATED]
    parts = all_gather_blocks(_pack([grads[n] for n in REPLICATED], F32), "gather_rep_grads")
    results = sum_adamw(parts, _pack([given[n] for n in REPLICATED], F32), _pack([m_in[n] for n in REPLICATED], F32),
                        _pack([v_in[n] for n in REPLICATED], F32), "adamw_replicated")
    for store, res in zip((out_g, out_d, out_m, out_v), results):
        for n, arr in zip(REPLICATED, _unpack(res, rep_shapes)):
            store[n] = arr

    return (loss, grads["x"], *[out_g[n] for n in WEIGHTS], *[out_d[n] for n in WEIGHTS],
            *[out_m[n] for n in WEIGHTS], *[out_v[n] for n in WEIGHTS])
```

```python
import functools
import math

import numpy as np
import jax
import jax.numpy as jnp
from jax import lax
from jax.experimental import pallas as pl
from jax.experimental.pallas import tpu as pltpu

F32 = jnp.float32
BF16 = jnp.bfloat16
MESH = pl.DeviceIdType.MESH
N_DEV = 8

D_MODEL = 1024
GRID_W = 64
ATT_HEADS, ATT_KV_HEADS, ATT_HEAD_DIM = 4, 2, 64
ATT_WIDTH, ATT_KV_WIDTH = 256, 128
ROPE_THETA = 10000.0
GLA_HEADS, GLA_DK, GLA_DV = 4, 64, 128
GLA_K_WIDTH, GLA_V_WIDTH = 256, 512
GLA_GATE_RANK, GLA_GATE_NORM, GLA_CHUNK = 16, 16.0, 64
S5_GROUPS, S5_GROUP_CH, S5_WIDTH, S5_STATE = 16, 16, 256, 64
S5_FLAT = S5_GROUPS * S5_STATE
N_BRANCH = 3
EPS = 1e-6
IN_SPLITS = (ATT_WIDTH, ATT_KV_WIDTH, ATT_KV_WIDTH, GLA_K_WIDTH, GLA_K_WIDTH, GLA_V_WIDTH, GLA_V_WIDTH,
             GLA_GATE_RANK, GLA_GATE_RANK, S5_WIDTH, N_BRANCH * D_MODEL)
D_IN = sum(IN_SPLITS)
D_IN_PAD = 5632

ADAM_LR, ADAM_B1, ADAM_B2, ADAM_EPS, ADAM_WD, ADAM_STEP = 0.001, 0.9, 0.999, 1e-08, 0.01, 10

VMEM_LIMIT = 56 * 1024 * 1024
PACK_W = 512
PACK_UNIT = 16 * PACK_W

WEIGHTS = ['c_ctx', 'ada_w', 'ada_b', 'norm_mix_pre', 'norm_mix_post', 'norm_ffn_pre', 'norm_ffn_post', 'w_in',
           'q_norm', 'k_norm', 'gla_gate_w', 'gla_gate_b', 'gla_out_norm', 's5_a_re', 's5_a_im', 's5_log_dt',
           's5_b_re', 's5_b_im', 's5_c_re', 's5_c_im', 's5_d', 's5_glu_w', 's5_glu_b', 'w_br_att', 'w_br_gla',
           'w_br_s5', 'w_out', 'ffn_up', 'ffn_conv_w', 'ffn_conv_b', 'ffn_down']
FWD_INPUTS = ['x', 'c', 'ctx'] + WEIGHTS
BIG = {'ada_w': 1, 'w_in': 1, 'w_br_att': 1, 'w_br_gla': 1, 'w_br_s5': 1, 'w_out': 0, 'ffn_up': 1, 'ffn_down': 0}
SMALL_SHARDED = {'gla_gate_w': 2, 'gla_gate_b': 1, 's5_glu_w': 0, 'ffn_conv_w': 1}
SHARDED = {**BIG, **SMALL_SHARDED}
REPLICATED = [n for n in WEIGHTS if n not in SHARDED]


def _pick(n, cands):
    for cand in cands:
        if n % cand == 0:
            return cand
    return n


def _params(sem):
    return pltpu.CompilerParams(dimension_semantics=sem, vmem_limit_bytes=VMEM_LIMIT)


_DIMS = {"nn": ((1,), (0,)), "nt": ((1,), (1,)), "tn": ((0,), (0,))}


def _mm(a, b, mode, name):
    if mode == "tn":
        K, M = a.shape
    else:
        M, K = a.shape
    N = b.shape[0] if mode == "nt" else b.shape[1]
    tm = _pick(M, (768, 512, 256, 128))
    tn = _pick(N, (512, 256, 128))
    tk = K if K <= 2816 else _pick(K, (1408, 1024, 768, 512, 256, 128))
    nk = K // tk
    dims = (_DIMS[mode], ((), ()))

    def body(a_ref, b_ref, o_ref):
        acc = lax.dot_general(a_ref[...].astype(BF16), b_ref[...].astype(BF16), dims, preferred_element_type=F32)
        if nk == 1:
            o_ref[...] = acc
        else:
            k = pl.program_id(2)

            @pl.when(k == 0)
            def _():
                o_ref[...] = acc

            @pl.when(k > 0)
            def _():
                o_ref[...] += acc

    a_spec = (pl.BlockSpec((tk, tm), lambda i, j, k: (k, i)) if mode == "tn"
              else pl.BlockSpec((tm, tk), lambda i, j, k: (i, k)))
    b_spec = (pl.BlockSpec((tn, tk), lambda i, j, k: (j, k)) if mode == "nt"
              else pl.BlockSpec((tk, tn), lambda i, j, k: (k, j)))
    return pl.pallas_call(
        body, name=name, grid=(M // tm, N // tn, nk),
        in_specs=[a_spec, b_spec], out_specs=pl.BlockSpec((tm, tn), lambda i, j, k: (i, j)),
        out_shape=jax.ShapeDtypeStruct((M, N), F32),
        compiler_params=_params(("parallel", "parallel", "arbitrary")),
    )(a, b)


@jax.custom_vjp
def matmul(a, w):
    return _mm(a, w, "nn", "mm_fwd")


def _matmul_fwd(a, w):
    return _mm(a, w, "nn", "mm_fwd"), (a, w)


def _matmul_bwd(res, dy):
    a, w = res
    return _mm(dy, w, "nt", "mm_dx"), _mm(a, dy, "tn", "mm_dw")


matmul.defvjp(_matmul_fwd, _matmul_bwd)


@jax.custom_vjp
def linear(a, w, w_grad_slot):
    del w_grad_slot
    return _mm(a, w, "nn", "lin_fwd")


def _linear_fwd(a, w, w_grad_slot):
    del w_grad_slot
    return _mm(a, w, "nn", "lin_fwd"), (a, w)


def _linear_bwd(res, dy):
    a, w = res
    return _mm(dy, w, "nt", "lin_dx"), jnp.zeros_like(w), _mm(a, dy, "tn", "lin_dw")


linear.defvjp(_linear_fwd, _linear_bwd)


def _attn_fwd_call(q, k, v, scale):
    H, Tq, d = q.shape
    KV, Tk, _ = k.shape
    G = H // KV
    tq = _pick(Tq, (256, 128, 64))

    def body(q_ref, k_ref, v_ref, o_ref, lse_ref):
        s = lax.dot_general(q_ref[0], k_ref[0], (_DIMS["nt"], ((), ())), preferred_element_type=F32) * scale
        m = jnp.max(s, axis=1, keepdims=True)
        p = jnp.exp(s - m)
        l = jnp.sum(p, axis=1, keepdims=True)
        pn = (p * (1.0 / l)).astype(BF16)
        o_ref[0] = jnp.dot(pn, v_ref[0], preferred_element_type=F32)
        lse_ref[0] = m + jnp.log(l)

    return pl.pallas_call(
        body, name="attn_fwd", grid=(H, Tq // tq),
        in_specs=[pl.BlockSpec((1, tq, d), lambda h, i: (h, i, 0)),
                  pl.BlockSpec((1, Tk, d), lambda h, i: (h // G, 0, 0)),
                  pl.BlockSpec((1, Tk, d), lambda h, i: (h // G, 0, 0))],
        out_specs=[pl.BlockSpec((1, tq, d), lambda h, i: (h, i, 0)),
                   pl.BlockSpec((1, tq, 1), lambda h, i: (h, i, 0))],
        out_shape=[jax.ShapeDtypeStruct((H, Tq, d), F32), jax.ShapeDtypeStruct((H, Tq, 1), F32)],
        compiler_params=_params(("parallel", "parallel")),
    )(q, k, v)


def _attn_bwd_call(q, k, v, o, lse, do, scale):
    H, Tq, d = q.shape
    KV, Tk, _ = k.shape
    G = H // KV
    tq = _pick(Tq, (256, 128, 64))
    ck = _pick(Tk, (1408, 1024, 512, 256, 128, 64))
    nck = Tk // ck

    def body(q_ref, k_ref, v_ref, o_ref, lse_ref, do_ref, dq_ref, dk_ref, dv_ref):
        @pl.when((pl.program_id(1) == 0) & (pl.program_id(2) == 0))
        def _():
            dk_ref[...] = jnp.zeros_like(dk_ref)
            dv_ref[...] = jnp.zeros_like(dv_ref)

        qb = q_ref[0]
        do = do_ref[0]
        dob = do.astype(BF16)
        delta = jnp.sum(do * o_ref[0], axis=1, keepdims=True)
        lse = lse_ref[0]
        dq = jnp.zeros((tq, d), F32)
        for cidx in range(nck):
            rows = slice(cidx * ck, (cidx + 1) * ck)
            ks = k_ref[0, rows, :]
            vs = v_ref[0, rows, :]
            s = lax.dot_general(qb, ks, (_DIMS["nt"], ((), ())), preferred_element_type=F32) * scale
            p = jnp.exp(s - lse)
            dv_ref[0, rows, :] += lax.dot_general(p.astype(BF16), dob, (_DIMS["tn"], ((), ())),
                                                  preferred_element_type=F32)
            dp = lax.dot_general(dob, vs, (_DIMS["nt"], ((), ())), preferred_element_type=F32)
            dsb = (p * (dp - delta) * scale).astype(BF16)
            dq = dq + jnp.dot(dsb, ks, preferred_element_type=F32)
            dk_ref[0, rows, :] += lax.dot_general(dsb, qb, (_DIMS["tn"], ((), ())), preferred_element_type=F32)
        dq_ref[0] = dq

    q_spec = pl.BlockSpec((1, tq, d), lambda kv, g, i: (kv * G + g, i, 0))
    kv_spec = pl.BlockSpec((1, Tk, d), lambda kv, g, i: (kv, 0, 0))
    return pl.pallas_call(
        body, name="attn_bwd", grid=(KV, G, Tq // tq),
        in_specs=[q_spec, kv_spec, kv_spec, q_spec,
                  pl.BlockSpec((1, tq, 1), lambda kv, g, i: (kv * G + g, i, 0)), q_spec],
        out_specs=[q_spec, kv_spec, kv_spec],
        out_shape=[jax.ShapeDtypeStruct((H, Tq, d), F32), jax.ShapeDtypeStruct((KV, Tk, d), F32),
                   jax.ShapeDtypeStruct((KV, Tk, d), F32)],
        compiler_params=_params(("arbitrary", "arbitrary", "arbitrary")),
    )(q, k, v, o, lse, do)


@jax.custom_vjp
def attention(q, k, v):
    return _attn_fwd_call(q.astype(BF16), k.astype(BF16), v.astype(BF16), ATT_HEAD_DIM ** -0.5)[0]


def _attention_fwd(q, k, v):
    qb, kb, vb = q.astype(BF16), k.astype(BF16), v.astype(BF16)
    o, lse = _attn_fwd_call(qb, kb, vb, ATT_HEAD_DIM ** -0.5)
    return o, (qb, kb, vb, o, lse)


def _attention_bwd(res, do):
    qb, kb, vb, o, lse = res
    return _attn_bwd_call(qb, kb, vb, o, lse, do, ATT_HEAD_DIM ** -0.5)


attention.defvjp(_attention_fwd, _attention_bwd)


_ORDER_DOWN = {0: False, 1: True, 2: True, 3: False}
_ORDER_ADJOINT = {0: 2, 1: 3}


def _scan_tables(a_re, a_im, down):
    pw_re, pw_im = [a_re], [a_im]
    for _ in range(7):
        pw_re, pw_im = (pw_re + [pw_re[-1] * a_re - pw_im[-1] * a_im],
                        pw_im + [pw_re[-1] * a_im + pw_im[-1] * a_re])
    carry_rows = list(range(7, -1, -1)) if down else list(range(8))
    rows_re = [pw_re[r] for r in carry_rows] + [pw_re[0], pw_re[1], pw_re[3]]
    rows_im = [pw_im[r] for r in carry_rows] + [pw_im[0], pw_im[1], pw_im[3]]
    tab = jnp.concatenate([jnp.stack(rows_re), jnp.stack(rows_im)], axis=1)
    return jnp.concatenate([tab, jnp.zeros((5, 2 * S5_FLAT), F32)], axis=0)


def _scan_call(bu, a_re, a_im, order, ctx_len):
    T, W2 = bu.shape
    P = W2 // 2
    rb = _pick(math.gcd(ctx_len, T - ctx_len), (256, 128, 64, 32, 16, 8))
    nblk, cb = T // rb, ctx_len // rb
    ntile = rb // 8
    down = _ORDER_DOWN[order]
    tab = _scan_tables(a_re, a_im, down)

    def blk(n):
        return _block_in_order(n, nblk, cb, order)

    def body(bu_ref, tab_ref, s_ref, carry_ref):
        @pl.when(pl.program_id(0) == 0)
        def _():
            carry_ref[...] = jnp.zeros_like(carry_ref)

        row = lax.broadcasted_iota(jnp.int32, (8, P), 0)
        cp_re, cp_im = tab_ref[0:8, 0:P], tab_ref[0:8, P:2 * P]
        steps = []
        for j, sh in enumerate((1, 2, 4)):
            keep = (row < 8 - sh) if down else (row >= sh)
            steps.append((8 - sh if down else sh, keep, tab_ref[8 + j:9 + j, 0:P], tab_ref[8 + j:9 + j, P:2 * P]))

        def tile(j, carry):
            c_re, c_im = carry
            t = (ntile - 1 - j) if down else j
            r0 = pl.multiple_of(t * 8, 8)
            x_re = bu_ref[pl.ds(r0, 8), 0:P]
            x_im = bu_ref[pl.ds(r0, 8), P:2 * P]
            for shift, keep, p_re, p_im in steps:
                y_re = jnp.where(keep, pltpu.roll(x_re, shift, 0), 0.0)
                y_im = jnp.where(keep, pltpu.roll(x_im, shift, 0), 0.0)
                x_re, x_im = x_re + p_re * y_re - p_im * y_im, x_im + p_re * y_im + p_im * y_re
            x_re, x_im = x_re + cp_re * c_re - cp_im * c_im, x_im + cp_re * c_im + cp_im * c_re
            s_ref[pl.ds(r0, 8), 0:P] = x_re
            s_ref[pl.ds(r0, 8), P:2 * P] = x_im
            last = 0 if down else 7
            return x_re[last:last + 1, :], x_im[last:last + 1, :]

        c_re, c_im = lax.fori_loop(0, ntile, tile, (carry_ref[0:1, 0:P], carry_ref[0:1, P:2 * P]))
        carry_ref[0:1, 0:P] = c_re
        carry_ref[0:1, P:2 * P] = c_im

    return pl.pallas_call(
        body, name=f"s5_scan_{order}", grid=(nblk,),
        in_specs=[pl.BlockSpec((rb, W2), lambda n: (blk(n), 0)), pl.BlockSpec((16, W2), lambda n: (0, 0))],
        out_specs=pl.BlockSpec((rb, W2), lambda n: (blk(n), 0)),
        out_shape=jax.ShapeDtypeStruct((T, W2), F32),
        scratch_shapes=[pltpu.VMEM((8, W2), F32)],
        compiler_params=_params(("arbitrary",)),
    )(bu, tab)


def _prev_in_order(s, order, ctx_len):
    zero = jnp.zeros_like(s[:1])
    if order == 0:
        return jnp.concatenate([zero, s[:-1]], axis=0)
    return jnp.concatenate([s[1:ctx_len], zero, s[ctx_len + 1:], s[:1]], axis=0)


@functools.partial(jax.custom_vjp, nondiff_argnums=(3, 4))
def s5_scan(bu, a_re, a_im, order, ctx_len):
    return _scan_call(bu, a_re, a_im, order, ctx_len)


def _s5_scan_fwd(bu, a_re, a_im, order, ctx_len):
    s = _scan_call(bu, a_re, a_im, order, ctx_len)
    return s, (s, a_re, a_im)


def _s5_scan_bwd(order, ctx_len, res, ds):
    s, a_re, a_im = res
    lam = _scan_call(ds, a_re, -a_im, _ORDER_ADJOINT[order], ctx_len)
    P = a_re.shape[0]
    sp = _prev_in_order(s, order, ctx_len)
    l_re, l_im, p_re, p_im = lam[:, :P], lam[:, P:], sp[:, :P], sp[:, P:]
    g_re = jnp.sum(l_re * p_re + l_im * p_im, axis=0)
    g_im = jnp.sum(l_im * p_re - l_re * p_im, axis=0)
    return lam, g_re, g_im


s5_scan.defvjp(_s5_scan_fwd, _s5_scan_bwd)


def _dot(a, b, mode, precision=None):
    return lax.dot_general(a, b, (_DIMS[mode], ((), ())), preferred_element_type=F32, precision=precision)


def _block_in_order(n, nblk, cblk, order):
    if order == 0:
        return n
    if order == 1:
        return jnp.where(n < cblk, cblk - 1 - n, nblk - 1 - (n - cblk))
    if order == 2:
        return nblk - 1 - n
    return jnp.where(n < nblk - cblk, cblk + n, n - (nblk - cblk))


def _gla_chunk_terms(qn, kn, gn, tri, reverse):
    b = _dot(tri, gn, "tn" if reverse else "nn", lax.Precision.HIGHEST)
    edge = 0 if reverse else GLA_CHUNK - 1
    b_end = b[edge:edge + 1, :]
    e_pos, e_neg, e_end = jnp.exp(b), jnp.exp(-b), jnp.exp(b_end - b)
    return b_end, e_pos, e_neg, e_end, qn * e_pos, kn * e_neg, kn * e_end


def _gla_masks():
    L = GLA_CHUNK
    rows, cols = lax.broadcasted_iota(jnp.int32, (L, L), 0), lax.broadcasted_iota(jnp.int32, (L, L), 1)
    return rows >= cols, rows <= cols


def _gla_blocking(n_t, ctx_len):
    n_chunks, ctx_chunks = n_t // GLA_CHUNK, ctx_len // GLA_CHUNK
    per_block = _pick(math.gcd(ctx_chunks, n_chunks - ctx_chunks), (4, 2, 1))
    return n_chunks, per_block, GLA_CHUNK * per_block, n_chunks // per_block, ctx_chunks // per_block


def _gla_fwd_call(q, k, v, g, reverse, ctx_len):
    H, T, dk = q.shape
    dv = v.shape[-1]
    n_chunks, cb, rb, nb, cblk = _gla_blocking(T, ctx_len)
    L = GLA_CHUNK
    order = 1 if reverse else 0

    def body(q_ref, k_ref, v_ref, g_ref, o_ref, sb_ref, s_ref):
        @pl.when(pl.program_id(1) == 0)
        def _():
            s_ref[...] = jnp.zeros_like(s_ref)

        lower, upper = _gla_masks()
        tri = lower.astype(F32)
        seen = upper if reverse else lower
        ones = jnp.ones((L, dv), F32)
        state = s_ref[...]
        for n in (reversed(range(cb)) if reverse else range(cb)):
            rows = slice(n * L, (n + 1) * L)
            qn, kn, vn, gn = q_ref[0, rows, :], k_ref[0, rows, :], v_ref[0, rows, :], g_ref[0, rows, :]
            _, _, _, _, q_in, k_in, k_end = _gla_chunk_terms(qn, kn, gn, tri, reverse)
            att = jnp.where(seen, _dot(q_in.astype(BF16), k_in.astype(BF16), "nt"), 0.0)
            vb = vn.astype(BF16)
            sb_ref[0, n] = state
            o_ref[0, rows, :] = (_dot(att.astype(BF16), vb, "nn")
                                 + _dot(q_in.astype(BF16), state.astype(BF16), "nn"))
            decay = jnp.exp(_dot(gn, ones, "tn", lax.Precision.HIGHEST))
            state = decay * state + _dot(k_end.astype(BF16), vb, "tn")
        s_ref[...] = state

    def at(h, i):
        return h, _block_in_order(i, nb, cblk, order), 0

    row_k, row_v = pl.BlockSpec((1, rb, dk), at), pl.BlockSpec((1, rb, dv), at)
    return pl.pallas_call(
        body, name="gla_fwd", grid=(H, nb),
        in_specs=[row_k, row_k, row_v, row_k],
        out_specs=[row_v, pl.BlockSpec((1, cb, dk, dv), lambda h, i: (*at(h, i), 0))],
        out_shape=[jax.ShapeDtypeStruct((H, T, dv), F32), jax.ShapeDtypeStruct((H, n_chunks, dk, dv), F32)],
        scratch_shapes=[pltpu.VMEM((dk, dv), F32)],
        compiler_params=_params(("parallel", "arbitrary")),
    )(q, k, v, g)


def _gla_bwd_call(q, k, v, g, sb, do, reverse, ctx_len):
    H, T, dk = q.shape
    dv = v.shape[-1]
    n_chunks, cb, rb, nb, cblk = _gla_blocking(T, ctx_len)
    L = GLA_CHUNK
    order = 3 if reverse else 2

    def body(q_ref, k_ref, v_ref, g_ref, sb_ref, do_ref, dq_ref, dk_ref, dv_ref, dg_ref, ds_ref):
        @pl.when(pl.program_id(1) == 0)
        def _():
            ds_ref[...] = jnp.zeros_like(ds_ref)

        lower, upper = _gla_masks()
        tri = lower.astype(F32)
        seen = upper if reverse else lower
        ones = jnp.ones((L, dv), F32)
        ones8 = jnp.ones((8, dv), F32)
        d_state = ds_ref[...]
        for n in (range(cb) if reverse else reversed(range(cb))):
            rows = slice(n * L, (n + 1) * L)
            qn, kn, vn, gn = q_ref[0, rows, :], k_ref[0, rows, :], v_ref[0, rows, :], g_ref[0, rows, :]
            state = sb_ref[0, n]
            b_end, e_pos, e_neg, e_end, q_in, k_in, k_end = _gla_chunk_terms(qn, kn, gn, tri, reverse)
            q_b, k_b, ke_b, vb = q_in.astype(BF16), k_in.astype(BF16), k_end.astype(BF16), vn.astype(BF16)
            dob = do_ref[0, rows, :].astype(BF16)
            dsb = d_state.astype(BF16)
            att = jnp.where(seen, _dot(q_b, k_b, "nt"), 0.0).astype(BF16)
            d_att = jnp.where(seen, _dot(dob, vb, "nt"), 0.0).astype(BF16)
            d_qin = _dot(d_att, k_b, "nn") + _dot(dob, state.astype(BF16), "nt")
            d_kin = _dot(d_att, q_b, "tn")
            d_kend = _dot(vb, dsb, "nt")
            dv_ref[0, rows, :] = _dot(att, dob, "tn") + _dot(ke_b, dsb, "nn")
            through_decay = _dot(ones8, state * d_state, "nt", lax.Precision.HIGHEST)[0:1, :]
            d_bend = jnp.sum(d_kend * k_end, axis=0, keepdims=True) + jnp.exp(b_end) * through_decay
            d_b = d_qin * q_in - d_kin * k_in - d_kend * k_end
            dg_ref[0, rows, :] = _dot(tri, d_b, "nn" if reverse else "tn", lax.Precision.HIGHEST) + d_bend
            dq_ref[0, rows, :] = d_qin * e_pos
            dk_ref[0, rows, :] = d_kin * e_neg + d_kend * e_end
            decay = jnp.exp(_dot(gn, ones, "tn", lax.Precision.HIGHEST))
            d_state = _dot(q_b, dob, "tn") + decay * d_state
        ds_ref[...] = d_state

    def at(h, i):
        return h, _block_in_order(i, nb, cblk, order), 0

    row_k, row_v = pl.BlockSpec((1, rb, dk), at), pl.BlockSpec((1, rb, dv), at)
    return pl.pallas_call(
        body, name="gla_bwd", grid=(H, nb),
        in_specs=[row_k, row_k, row_v, row_k, pl.BlockSpec((1, cb, dk, dv), lambda h, i: (*at(h, i), 0)), row_v],
        out_specs=[row_k, row_k, row_v, row_k],
        out_shape=[jax.ShapeDtypeStruct((H, T, dk), F32), jax.ShapeDtypeStruct((H, T, dk), F32),
                   jax.ShapeDtypeStruct((H, T, dv), F32), jax.ShapeDtypeStruct((H, T, dk), F32)],
        scratch_shapes=[pltpu.VMEM((dk, dv), F32)],
        compiler_params=_params(("parallel", "arbitrary")),
    )(q, k, v, g, sb, do)


@functools.partial(jax.custom_vjp, nondiff_argnums=(4, 5))
def gla_scan(q, k, v, g, reverse, ctx_len):
    return _gla_fwd_call(q, k, v, g, reverse, ctx_len)[0]


def _gla_scan_fwd(q, k, v, g, reverse, ctx_len):
    o, sb = _gla_fwd_call(q, k, v, g, reverse, ctx_len)
    return o, (q, k, v, g, sb)


def _gla_scan_bwd(reverse, ctx_len, res, do):
    q, k, v, g, sb = res
    return _gla_bwd_call(q, k, v, g, sb, do, reverse, ctx_len)


gla_scan.defvjp(_gla_scan_fwd, _gla_scan_bwd)


@functools.partial(jax.custom_vjp, nondiff_argnums=(1,))
def split_cols(x, sizes):
    points = np.cumsum((0,) + tuple(sizes))
    return tuple(x[:, int(a):int(b)] for a, b in zip(points[:-1], points[1:]))


def _split_cols_fwd(x, sizes):
    return split_cols(x, sizes), x.shape[1]


def _split_cols_bwd(sizes, width, cts):
    parts = list(cts)
    if width > sum(sizes):
        parts.append(jnp.zeros((cts[0].shape[0], width - sum(sizes)), cts[0].dtype))
    return (jnp.concatenate(parts, axis=1),)


split_cols.defvjp(_split_cols_fwd, _split_cols_bwd)


def _position():
    return lax.axis_index("x"), lax.axis_index("y"), lax.axis_index("c")


def all_gather_blocks(shard, name):
    R, W = shard.shape

    def body(x_ref, out_ref, send_sems, recv_sems, local_sem):
        x, y, c = _position()
        me, sibling = (x, y, c), (x, y, 1 - c)
        chips = [(1 - x, y), (x, 1 - y), (1 - x, 1 - y)]

        def slot(px, py, pc):
            return out_ref.at[4 * px + 2 * py + pc]

        def copy(k, block, to, src=None):
            return pltpu.make_async_remote_copy(
                src_ref=slot(*block) if src is None else src, dst_ref=slot(*block),
                send_sem=send_sems.at[k], recv_sem=recv_sems.at[k], device_id=to, device_id_type=MESH)

        mine = pltpu.make_async_copy(x_ref, slot(*me), local_sem)
        mine.start()
        first = [copy(0, me, sibling, src=x_ref)]
        first += [copy(1 + j, me, (*chip, c), src=x_ref) for j, chip in enumerate(chips)]
        for cp in first:
            cp.start()
        passed = [copy(4 + j, (*chip, c), sibling) for j, chip in enumerate(chips)]
        for j, chip in enumerate(chips):
            copy(1 + j, (*chip, c), me).wait_recv()
            passed[j].start()
        copy(0, sibling, me).wait_recv()
        for j, chip in enumerate(chips):
            copy(4 + j, (*chip, 1 - c), me).wait_recv()
        for cp in first + passed:
            cp.wait_send()
        mine.wait()

    return pl.pallas_call(
        body, name=name,
        out_shape=jax.ShapeDtypeStruct((N_DEV, R, W), shard.dtype),
        in_specs=[pl.BlockSpec(memory_space=pltpu.HBM)], out_specs=pl.BlockSpec(memory_space=pltpu.HBM),
        scratch_shapes=[pltpu.SemaphoreType.DMA((7,)), pltpu.SemaphoreType.DMA((7,)), pltpu.SemaphoreType.DMA],
    )(shard)


def exchange_blocks(blocks, name):
    _, R, W = blocks.shape
    flips = [(fx, fy, fc) for fx in (0, 1) for fy in (0, 1) for fc in (0, 1)][1:]

    def body(x_ref, out_ref, send_sems, recv_sems, local_sem):
        x, y, c = _position()
        me = 4 * x + 2 * y + c
        mine = pltpu.make_async_copy(x_ref.at[me], out_ref.at[me], local_sem)
        mine.start()
        copies = []
        for k, (fx, fy, fc) in enumerate(flips):
            px, py, pc = x ^ fx, y ^ fy, c ^ fc
            peer = 4 * px + 2 * py + pc
            copies.append((
                pltpu.make_async_remote_copy(src_ref=x_ref.at[peer], dst_ref=out_ref.at[me],
                                             send_sem=send_sems.at[k], recv_sem=recv_sems.at[k],
                                             device_id=(px, py, pc), device_id_type=MESH),
                pltpu.make_async_remote_copy(src_ref=x_ref.at[peer], dst_ref=out_ref.at[peer],
                                             send_sem=send_sems.at[k], recv_sem=recv_sems.at[k],
                                             device_id=(px, py, pc), device_id_type=MESH)))
        for send, _ in copies:
            send.start()
        for _, recv in copies:
            recv.wait_recv()
        for send, _ in copies:
            send.wait_send()
        mine.wait()

    return pl.pallas_call(
        body, name=name,
        out_shape=jax.ShapeDtypeStruct(blocks.shape, blocks.dtype),
        in_specs=[pl.BlockSpec(memory_space=pltpu.HBM)], out_specs=pl.BlockSpec(memory_space=pltpu.HBM),
        scratch_shapes=[pltpu.SemaphoreType.DMA((7,)), pltpu.SemaphoreType.DMA((7,)), pltpu.SemaphoreType.DMA],
    )(blocks)


def sum_adamw(parts, w, m, v, name):
    _, R, W = parts.shape
    tr = _pick(R, (512, 256, 128, 64, 32, 16, 8))

    def body(p_ref, w_ref, m_ref, v_ref, g_out, d_out, m_out, v_out):
        g = p_ref[0].astype(F32)
        for j in range(1, N_DEV):
            g = g + p_ref[j].astype(F32)
        m_new = ADAM_B1 * m_ref[...] + (1.0 - ADAM_B1) * g
        v_new = ADAM_B2 * v_ref[...] + (1.0 - ADAM_B2) * (g * g)
        m_hat = m_new / (1.0 - ADAM_B1 ** ADAM_STEP)
        v_hat = v_new / (1.0 - ADAM_B2 ** ADAM_STEP)
        g_out[...] = g
        d_out[...] = -ADAM_LR * (m_hat / (jnp.sqrt(v_hat) + ADAM_EPS) + ADAM_WD * w_ref[...])
        m_out[...] = m_new
        v_out[...] = v_new

    row = pl.BlockSpec((tr, W), lambda i: (i, 0))
    return pl.pallas_call(
        body, name=name, grid=(R // tr,),
        in_specs=[pl.BlockSpec((N_DEV, tr, W), lambda i: (0, i, 0)), row, row, row],
        out_specs=[row, row, row, row],
        out_shape=[jax.ShapeDtypeStruct((R, W), F32)] * 4,
        compiler_params=_params(("parallel",)),
    )(parts, w, m, v)


def _padded(n):
    return -(-n // PACK_UNIT) * PACK_UNIT


def _pack(arrays, dtype):
    segs = []
    for arr in arrays:
        flat = arr.reshape(-1).astype(dtype)
        segs.append(jnp.pad(flat, (0, _padded(flat.size) - flat.size)))
    return jnp.concatenate(segs).reshape(-1, PACK_W)


def _pack_blocks(arrays, dtype):
    segs = []
    for arr in arrays:
        flat = arr.reshape(N_DEV, -1).astype(dtype)
        segs.append(jnp.pad(flat, ((0, 0), (0, _padded(flat.shape[1]) - flat.shape[1]))))
    return jnp.concatenate(segs, axis=1).reshape(N_DEV, -1, PACK_W)


def _unpack(buf, shapes):
    lead = buf.shape[:-2]
    flat = buf.reshape(*lead, -1)
    out, off = [], 0
    for shape in shapes:
        n = int(np.prod(shape))
        out.append(flat[..., off:off + n].reshape(*lead, *shape))
        off += _padded(n)
    return out


def _round_up(n, unit):
    return -(-n // unit) * unit


def _pack_rows(arrays, width, dtype):
    parts = []
    for arr in arrays:
        r, c = arr.shape[-2:]
        pad = [(0, 0)] * (arr.ndim - 2) + [(0, _round_up(r, 16) - r), (0, width - c)]
        parts.append(jnp.pad(arr.astype(dtype), pad))
    return jnp.concatenate(parts, axis=-2)


def _unpack_rows(buf, shapes):
    out, off = [], 0
    for r, c in shapes:
        out.append(buf[..., off:off + r, :c])
        off += _round_up(r, 16)
    return out


def _to_full(blocks, axis):
    moved = jnp.moveaxis(blocks, 0, axis)
    shape = list(moved.shape)
    shape[axis:axis + 2] = [shape[axis] * shape[axis + 1]]
    return moved.reshape(shape)


def _to_blocks(full, axis):
    shape = list(full.shape)
    shape[axis:axis + 1] = [N_DEV, shape[axis] // N_DEV]
    return jnp.moveaxis(full.reshape(shape), axis, 0)


def rms_norm(x, gain):
    return x * lax.rsqrt(jnp.mean(x * x, axis=-1, keepdims=True) + EPS) * gain


def _rope_tables(n_tokens):
    rows = n_tokens // GRID_W
    row = jnp.repeat(jnp.arange(rows, dtype=F32), GRID_W)
    col = jnp.tile(jnp.arange(GRID_W, dtype=F32), rows)
    n_freq = ATT_HEAD_DIM // 4
    inv_freq = ROPE_THETA ** (-jnp.arange(n_freq, dtype=F32) / n_freq)
    ang = jnp.stack([row[:, None] * inv_freq, col[:, None] * inv_freq], axis=1)
    return jnp.cos(ang), jnp.sin(ang)


def _rope(x, cos, sin):
    n_t, nh, hd = x.shape
    xr = x.reshape(n_t, nh, 2, 2, hd // 4)
    x1, x2 = xr[..., 0, :], xr[..., 1, :]
    cs, sn = cos[:, None], sin[:, None]
    return jnp.stack([x1 * cs - x2 * sn, x2 * cs + x1 * sn], axis=-2).reshape(n_t, nh, hd)


W_IN_ORDER = (0, 1, 2, 3, 4, 5, 6, 9, 10, 7, 8)
W_IN_SIZES = tuple(IN_SPLITS[s] for s in W_IN_ORDER)


def _w_in_reorder(w, to_kernel_order):
    if to_kernel_order:
        points = np.cumsum((0,) + IN_SPLITS)
        pieces = [w[..., int(points[s]):int(points[s + 1])] for s in W_IN_ORDER]
    else:
        points = np.cumsum((0,) + W_IN_SIZES)
        where = {s: j for j, s in enumerate(W_IN_ORDER)}
        pieces = [w[..., int(points[where[s]]):int(points[where[s] + 1])] for s in range(len(IN_SPLITS))]
    return jnp.concatenate(pieces, axis=-1)


def _s5_discretize(a_re, a_im, log_dt, b_re, b_im):
    dt = jnp.exp(log_dt)[:, None]
    mag = jnp.exp(a_re * dt)
    ab_re, ab_im = mag * jnp.cos(a_im * dt), mag * jnp.sin(a_im * dt)
    den = a_re * a_re + a_im * a_im
    f_re = ((ab_re - 1.0) * a_re + ab_im * a_im) / den
    f_im = (ab_im * a_re - (ab_re - 1.0) * a_im) / den
    bb_re = f_re[..., None] * b_re - f_im[..., None] * b_im
    bb_im = f_re[..., None] * b_im + f_im[..., None] * b_re
    return ab_re, ab_im, bb_re, bb_im


def _s5_direction(u, lp, d, ctx_len):
    ab_re, ab_im, bb_re, bb_im = _s5_discretize(lp["s5_a_re"][d], lp["s5_a_im"][d], lp["s5_log_dt"][d],
                                                lp["s5_b_re"][d], lp["s5_b_im"][d])
    eye = jnp.eye(S5_GROUPS, dtype=F32)
    b_cat = jnp.concatenate([jnp.einsum("gph,gk->ghkp", bb_re, eye).reshape(S5_WIDTH, S5_FLAT),
                             jnp.einsum("gph,gk->ghkp", bb_im, eye).reshape(S5_WIDTH, S5_FLAT)], axis=1)
    c_cat = jnp.concatenate([jnp.einsum("ghp,gk->gpkh", lp["s5_c_re"][d], eye).reshape(S5_FLAT, S5_WIDTH),
                             -jnp.einsum("ghp,gk->gpkh", lp["s5_c_im"][d], eye).reshape(S5_FLAT, S5_WIDTH)], axis=0)
    bu = matmul(u, b_cat)
    s = s5_scan(bu, ab_re.reshape(-1), ab_im.reshape(-1), d, ctx_len)
    return matmul(s, c_cat)


def _s5_branch(u, lp, ctx_len):
    y = _s5_direction(u, lp, 0, ctx_len) + _s5_direction(u, lp, 1, ctx_len) + lp["s5_d"] * u
    y = jax.nn.gelu(y)
    return y * jax.nn.sigmoid(matmul(y, lp["s5_glu_w"]) + lp["s5_glu_b"])


def _heads(a, nh):
    return a.reshape(a.shape[0], nh, a.shape[1] // nh).transpose(1, 0, 2)


def _token_mixer(h, lp, rope, ctx_len, with_ctx_out):
    n_t = h.shape[0]
    proj = linear(h, lp["w_in"], lp["w_in_slot"])
    aq, ak, av, gq, gk, gv, gr, su, bg, glf, glb = split_cols(proj, W_IN_SIZES)

    aq = rms_norm(aq.reshape(n_t, ATT_HEADS, ATT_HEAD_DIM), lp["q_norm"])
    ak = rms_norm(ak.reshape(n_t, ATT_KV_HEADS, ATT_HEAD_DIM), lp["k_norm"])
    aq = jnp.concatenate([aq[:ctx_len], _rope(aq[ctx_len:], *rope)], axis=0).transpose(1, 0, 2)
    ak = jnp.concatenate([ak[:ctx_len], _rope(ak[ctx_len:], *rope)], axis=0).transpose(1, 0, 2)
    av = av.reshape(n_t, ATT_KV_HEADS, ATT_HEAD_DIM).transpose(1, 0, 2)
    o_att_lat = attention(aq[:, ctx_len:], ak, av)
    if with_ctx_out:
        o_att_ctx = attention(aq[:, :ctx_len], ak[:, :ctx_len], av[:, :ctx_len])
        o_att = jnp.concatenate([o_att_ctx, o_att_lat], axis=1)
    else:
        o_att = o_att_lat
    o_att = o_att.transpose(1, 0, 2).reshape(-1, ATT_WIDTH)

    def log_decay(low, d):
        z = jnp.dot(low, lp["gla_gate_w"][d]) + lp["gla_gate_b"][d]
        return _heads(jax.nn.log_sigmoid(z) / GLA_GATE_NORM, GLA_HEADS)

    q_g, k_g, v_g = _heads(gq, GLA_HEADS) * (GLA_DK ** -0.5), _heads(gk, GLA_HEADS), _heads(gv, GLA_HEADS)
    o_f = gla_scan(q_g, k_g, v_g, log_decay(glf, 0), False, ctx_len)
    o_b = gla_scan(q_g, k_g, v_g, log_decay(glb, 1), True, ctx_len)
    o_gla = rms_norm((o_f + o_b).transpose(1, 0, 2), lp["gla_out_norm"]).reshape(n_t, GLA_V_WIDTH)
    o_gla = o_gla * jax.nn.silu(gr)

    o_s5 = _s5_branch(su, lp, ctx_len)

    if not with_ctx_out:
        o_gla, o_s5, bg = o_gla[ctx_len:], o_s5[ctx_len:], bg[ctx_len:]
    g_att, g_gla, g_s5 = split_cols(jax.nn.sigmoid(bg), (D_MODEL,) * N_BRANCH)
    merged = (g_att * linear(o_att, lp["w_br_att"], lp["w_br_att_slot"])
              + g_gla * linear(o_gla, lp["w_br_gla"], lp["w_br_gla_slot"])
              + g_s5 * linear(o_s5, lp["w_br_s5"], lp["w_br_s5_slot"]))
    return linear(merged, lp["w_out"], lp["w_out_slot"])


HALO = 8


def _ffn_mid_blocking(n_rows, half):
    return _pick(n_rows, (1056, 1024, 256, 128, 64, 32, 16, 8)), _pick(half, (256, 128))


def _ffn_mid_specs(n_rows, half, rb, tc):
    nj, per = half // tc, rb // HALO

    def specs(side):
        return [pl.BlockSpec((rb, tc), lambda j, i: (i, side * nj + j)),
                pl.BlockSpec((HALO, tc), lambda j, i: (jnp.maximum(i * per - 1, 0), side * nj + j)),
                pl.BlockSpec((HALO, tc), lambda j, i: (jnp.minimum((i + 1) * per, n_rows // HALO - 1), side * nj + j))]

    return specs


def _with_halo(main_ref, prev_ref, next_ref):
    return jnp.concatenate([prev_ref[...], main_ref[...], next_ref[...]], axis=0)


def _row_neighbours(ext, first_row, n_rows, starts):
    n = ext.shape[0]
    row = lax.broadcasted_iota(jnp.int32, ext.shape, 0) + first_row
    first = functools.reduce(jnp.logical_or, [row == s for s in starts])
    last = functools.reduce(jnp.logical_or, [row == e - 1 for e in tuple(starts[1:]) + (n_rows,)])
    return jnp.where(first, 0.0, pltpu.roll(ext, 1, 0)), jnp.where(last, 0.0, pltpu.roll(ext, n - 1, 0))


def _ffn_mid_fwd_call(u, taps, starts):
    n_rows, half = u.shape[0], u.shape[1] // 2
    rb, tc = _ffn_mid_blocking(n_rows, half)
    specs = _ffn_mid_specs(n_rows, half, rb, tc)
    nj = half // tc

    def body(am, ap, an, vm, vp, vn, wa, wv, o_ref):
        first_row = pl.program_id(1) * rb - HALO

        def conv(ext, w):
            above, below = _row_neighbours(ext, first_row, n_rows, starts)
            return above * w[0:1, :] + ext * w[1:2, :] + below * w[2:3, :] + w[3:4, :]

        ca = conv(_with_halo(am, ap, an), wa)[HALO:HALO + rb]
        cv = conv(_with_halo(vm, vp, vn), wv)[HALO:HALO + rb]
        o_ref[...] = ca * jax.nn.sigmoid(ca) * cv

    return pl.pallas_call(
        body, name="ffn_mid_fwd", grid=(nj, n_rows // rb),
        in_specs=specs(0) + specs(1) + [pl.BlockSpec((8, tc), lambda j, i: (0, j)),
                                        pl.BlockSpec((8, tc), lambda j, i: (0, nj + j))],
        out_specs=pl.BlockSpec((rb, tc), lambda j, i: (i, j)),
        out_shape=jax.ShapeDtypeStruct((n_rows, half), F32),
        compiler_params=_params(("parallel", "parallel")),
    )(u, u, u, u, u, u, taps, taps)


def _ffn_mid_bwd_call(u, taps, d_act, starts):
    n_rows, half = u.shape[0], u.shape[1] // 2
    rb, tc = _ffn_mid_blocking(n_rows, half)
    specs = _ffn_mid_specs(n_rows, half, rb, tc)
    nj = half // tc
    main = slice(HALO, HALO + rb)

    def body(am, ap, an, vm, vp, vn, wa, wv, dm, dp, dn, dua_ref, duv_ref, dwa_ref, dwv_ref):
        i = pl.program_id(1)
        first_row = i * rb - HALO
        neighbours = functools.partial(_row_neighbours, first_row=first_row, n_rows=n_rows, starts=starts)
        ext_a, ext_v, ext_d = _with_halo(am, ap, an), _with_halo(vm, vp, vn), _with_halo(dm, dp, dn)
        above_a, below_a = neighbours(ext_a)
        above_v, below_v = neighbours(ext_v)
        ca = above_a * wa[0:1, :] + ext_a * wa[1:2, :] + below_a * wa[2:3, :] + wa[3:4, :]
        cv = above_v * wv[0:1, :] + ext_v * wv[1:2, :] + below_v * wv[2:3, :] + wv[3:4, :]
        sig = jax.nn.sigmoid(ca)
        d_cv = ext_d * (ca * sig)
        d_ca = ext_d * cv * (sig * (1.0 + ca * (1.0 - sig)))

        def finish(d_c, above, ext, below, w, du_ref, dw_ref):
            d_above, d_below = neighbours(d_c)
            du_ref[...] = (w[1:2, :] * d_c + w[0:1, :] * d_below + w[2:3, :] * d_above)[main]
            d_main = d_c[main]
            sums = jnp.concatenate([jnp.sum(above[main] * d_main, axis=0, keepdims=True),
                                    jnp.sum(ext[main] * d_main, axis=0, keepdims=True),
                                    jnp.sum(below[main] * d_main, axis=0, keepdims=True),
                                    jnp.sum(d_main, axis=0, keepdims=True), jnp.zeros((4, tc), F32)], axis=0)

            @pl.when(i == 0)
            def _():
                dw_ref[...] = sums

            @pl.when(i > 0)
            def _():
                dw_ref[...] += sums

        finish(d_ca, above_a, ext_a, below_a, wa, dua_ref, dwa_ref)
        finish(d_cv, above_v, ext_v, below_v, wv, duv_ref, dwv_ref)

    d_specs = specs(0)
    block = pl.BlockSpec((rb, tc), lambda j, i: (i, j))
    taps_out = pl.BlockSpec((8, tc), lambda j, i: (0, j))
    return pl.pallas_call(
        body, name="ffn_mid_bwd", grid=(nj, n_rows // rb),
        in_specs=specs(0) + specs(1) + [pl.BlockSpec((8, tc), lambda j, i: (0, j)),
                                        pl.BlockSpec((8, tc), lambda j, i: (0, nj + j))] + d_specs,
        out_specs=[block, block, taps_out, taps_out],
        out_shape=[jax.ShapeDtypeStruct((n_rows, half), F32)] * 2 + [jax.ShapeDtypeStruct((8, half), F32)] * 2,
        compiler_params=_params(("parallel", "arbitrary")),
    )(u, u, u, u, u, u, taps, taps, d_act, d_act, d_act)


def _taps(conv_w, conv_b):
    return jnp.concatenate([conv_w, conv_b[None, :], jnp.zeros((4, conv_w.shape[1]), F32)], axis=0)


@functools.partial(jax.custom_vjp, nondiff_argnums=(3,))
def ffn_mid(u, conv_w, conv_b, starts):
    return _ffn_mid_fwd_call(u, _taps(conv_w, conv_b), starts)


def _ffn_mid_fwd(u, conv_w, conv_b, starts):
    return _ffn_mid_fwd_call(u, _taps(conv_w, conv_b), starts), (u, conv_w, conv_b)


def _ffn_mid_bwd(starts, res, d_act):
    u, conv_w, conv_b = res
    du_a, du_v, dw_a, dw_v = _ffn_mid_bwd_call(u, _taps(conv_w, conv_b), d_act, starts)
    d_taps = jnp.concatenate([dw_a, dw_v], axis=1)
    return jnp.concatenate([du_a, du_v], axis=1), d_taps[0:3], d_taps[3]


ffn_mid.defvjp(_ffn_mid_fwd, _ffn_mid_bwd)


def _conv_ffn(h, lp, starts):
    u = linear(h, lp["ffn_up"], lp["ffn_up_slot"])
    act = ffn_mid(u, lp["ffn_conv_w"], lp["ffn_conv_b"], tuple(starts))
    return linear(act, lp["ffn_down"], lp["ffn_down_slot"])


def _rows(ctx_val, lat_val, ctx_len, n_lat, with_ctx):
    lat = jnp.broadcast_to(lat_val, (n_lat, lat_val.shape[-1]))
    if not with_ctx:
        return lat
    return jnp.concatenate([jnp.broadcast_to(ctx_val, (ctx_len, ctx_val.shape[-1])), lat], axis=0)


def _local_loss(diff, fixed):
    p = {**fixed, **diff}
    x, ctx = p["x"][0], p["ctx"][0]
    n_lat, ctx_len = x.shape[0], ctx.shape[0]
    depth = p["ada_b"].shape[0]
    rope = _rope_tables(n_lat)
    rows = jnp.concatenate([ctx, x], axis=0)
    cond = jnp.zeros((16, D_MODEL), F32).at[0].set(jax.nn.silu(p["c"][0])).at[1].set(jax.nn.silu(p["c_ctx"]))
    layer_names = [n for n in WEIGHTS if n != "c_ctx"]
    for i in range(depth):
        last = i == depth - 1
        lp = {n: p[n][i] for n in layer_names}
        lp.update({n + "_slot": p[n + "_slot"][i] for n in BIG})
        mod = linear(cond, lp["ada_w"], lp["ada_w_slot"]) + lp["ada_b"]
        m_lat, m_ctx = jnp.split(mod[0:1], 6, axis=-1), jnp.split(mod[1:2], 6, axis=-1)
        both = functools.partial(_rows, ctx_len=ctx_len, n_lat=n_lat, with_ctx=True)

        h = rms_norm(rows, lp["norm_mix_pre"]) * (1.0 + both(m_ctx[1], m_lat[1])) + both(m_ctx[0], m_lat[0])
        y = _token_mixer(h, lp, rope, ctx_len, not last)
        if last:
            rows = rows[ctx_len:]
        cur = functools.partial(_rows, ctx_len=ctx_len, n_lat=n_lat, with_ctx=not last)
        starts = [0] if last else [0, ctx_len]
        rows = rows + cur(m_ctx[2], m_lat[2]) * rms_norm(y, lp["norm_mix_post"])
        h = rms_norm(rows, lp["norm_ffn_pre"]) * (1.0 + cur(m_ctx[4], m_lat[4])) + cur(m_ctx[3], m_lat[3])
        rows = rows + cur(m_ctx[5], m_lat[5]) * rms_norm(_conv_ffn(h, lp, starts), lp["norm_ffn_post"])
    err = jnp.square(rows - p["loss_target"][0])
    return 0.5 * jnp.sum(jnp.mean(err, axis=-1))


def kernel(x, c, ctx, c_ctx, ada_w, ada_b, norm_mix_pre, norm_mix_post, norm_ffn_pre, norm_ffn_post, w_in, q_norm, k_norm, gla_gate_w, gla_gate_b, gla_out_norm, s5_a_re, s5_a_im, s5_log_dt, s5_b_re, s5_b_im, s5_c_re, s5_c_im, s5_d, s5_glu_w, s5_glu_b, w_br_att, w_br_gla, w_br_s5, w_out, ffn_up, ffn_conv_w, ffn_conv_b, ffn_down, loss_target, m_c_ctx, m_ada_w, m_ada_b, m_norm_mix_pre, m_norm_mix_post, m_norm_ffn_pre, m_norm_ffn_post, m_w_in, m_q_norm, m_k_norm, m_gla_gate_w, m_gla_gate_b, m_gla_out_norm, m_s5_a_re, m_s5_a_im, m_s5_log_dt, m_s5_b_re, m_s5_b_im, m_s5_c_re, m_s5_c_im, m_s5_d, m_s5_glu_w, m_s5_glu_b, m_w_br_att, m_w_br_gla, m_w_br_s5, m_w_out, m_ffn_up, m_ffn_conv_w, m_ffn_conv_b, m_ffn_down, v_c_ctx, v_ada_w, v_ada_b, v_norm_mix_pre, v_norm_mix_post, v_norm_ffn_pre, v_norm_ffn_post, v_w_in, v_q_norm, v_k_norm, v_gla_gate_w, v_gla_gate_b, v_gla_out_norm, v_s5_a_re, v_s5_a_im, v_s5_log_dt, v_s5_b_re, v_s5_b_im, v_s5_c_re, v_s5_c_im, v_s5_d, v_s5_glu_w, v_s5_glu_b, v_w_br_att, v_w_br_gla, v_w_br_s5, v_w_out, v_ffn_up, v_ffn_conv_w, v_ffn_conv_b, v_ffn_down):
    args = (x, c, ctx, c_ctx, ada_w, ada_b, norm_mix_pre, norm_mix_post, norm_ffn_pre, norm_ffn_post, w_in, q_norm, k_norm, gla_gate_w, gla_gate_b, gla_out_norm, s5_a_re, s5_a_im, s5_log_dt, s5_b_re, s5_b_im, s5_c_re, s5_c_im, s5_d, s5_glu_w, s5_glu_b, w_br_att, w_br_gla, w_br_s5, w_out, ffn_up, ffn_conv_w, ffn_conv_b, ffn_down)
    given = dict(zip(FWD_INPUTS, args))
    given["loss_target"] = loss_target
    m_in = dict(zip(WEIGHTS, (m_c_ctx, m_ada_w, m_ada_b, m_norm_mix_pre, m_norm_mix_post, m_norm_ffn_pre, m_norm_ffn_post, m_w_in, m_q_norm, m_k_norm, m_gla_gate_w, m_gla_gate_b, m_gla_out_norm, m_s5_a_re, m_s5_a_im, m_s5_log_dt, m_s5_b_re, m_s5_b_im, m_s5_c_re, m_s5_c_im, m_s5_d, m_s5_glu_w, m_s5_glu_b, m_w_br_att, m_w_br_gla, m_w_br_s5, m_w_out, m_ffn_up, m_ffn_conv_w, m_ffn_conv_b, m_ffn_down)))
    v_in = dict(zip(WEIGHTS, (v_c_ctx, v_ada_w, v_ada_b, v_norm_mix_pre, v_norm_mix_post, v_norm_ffn_pre, v_norm_ffn_post, v_w_in, v_q_norm, v_k_norm, v_gla_gate_w, v_gla_gate_b, v_gla_out_norm, v_s5_a_re, v_s5_a_im, v_s5_log_dt, v_s5_b_re, v_s5_b_im, v_s5_c_re, v_s5_c_im, v_s5_d, v_s5_glu_w, v_s5_glu_b, v_w_br_att, v_w_br_gla, v_w_br_s5, v_w_out, v_ffn_up, v_ffn_conv_w, v_ffn_conv_b, v_ffn_down)))
    depth = ada_b.shape[0]
    big_names, small_names = list(BIG), list(SMALL_SHARDED)
    sharded_names = big_names + small_names

    full = {}
    groups = {}
    for n in big_names:
        groups.setdefault(_round_up(given[n].shape[2], 128), []).append(n)
    per_layer = {n: [] for n in big_names}
    for i in range(depth):
        for width, names in groups.items():
            gathered = all_gather_blocks(_pack_rows([given[n][i] for n in names], width, BF16),
                                         f"gather_weights_{width}")
            for n, blocks in zip(names, _unpack_rows(gathered, [given[n].shape[1:] for n in names])):
                per_layer[n].append(_to_full(blocks, BIG[n]))
    for n in big_names:
        full[n] = jnp.stack(per_layer[n])
    small_shapes = [given[n].shape for n in small_names]
    gathered = all_gather_blocks(_pack([given[n] for n in small_names], F32), "gather_small")
    for n, blocks in zip(small_names, _unpack(gathered, small_shapes)):
        full[n] = _to_full(blocks, SMALL_SHARDED[n] + 1)

    w_in_full = jnp.pad(_w_in_reorder(full["w_in"], True), ((0, 0), (0, 0), (0, D_IN_PAD - D_IN)))

    diff = {"x": x}
    diff.update({n: given[n] for n in REPLICATED})
    diff.update({n: full[n] for n in small_names})
    fixed = {"c": c, "ctx": ctx, "loss_target": loss_target}
    for n in big_names:
        fixed[n] = w_in_full if n == "w_in" else full[n]
        diff[n + "_slot"] = jnp.zeros(fixed[n].shape, F32)
    loss_local, grads = jax.value_and_grad(_local_loss)(diff, fixed)
    loss = lax.psum(loss_local, ("x", "y", "c"))

    g_full = {n: grads[n] for n in small_names}
    for n in big_names:
        g_full[n] = grads[n + "_slot"]
    g_full["w_in"] = _w_in_reorder(g_full["w_in"], False)

    out_g, out_d, out_m, out_v = {}, {}, {}, {}
    layer_out = {n: ([], [], [], []) for n in big_names}
    for i in range(depth):
        for width, names in groups.items():
            send = _pack_rows([_to_blocks(g_full[n][i], BIG[n]) for n in names], width, BF16)
            parts = exchange_blocks(send, f"exchange_grads_{width}")
            results = sum_adamw(parts, _pack_rows([given[n][i] for n in names], width, F32),
                                _pack_rows([m_in[n][i] for n in names], width, F32),
                                _pack_rows([v_in[n][i] for n in names], width, F32), f"adamw_{width}")
            for k, res in enumerate(results):
                for n, arr in zip(names, _unpack_rows(res, [given[n].shape[1:] for n in names])):
                    layer_out[n][k].append(arr)
    for n in big_names:
        out_g[n], out_d[n], out_m[n], out_v[n] = (jnp.stack(parts_k) for parts_k in layer_out[n])
    send = _pack_blocks([_to_blocks(g_full[n], SMALL_SHARDED[n] + 1) for n in small_names], F32)
    parts = exchange_blocks(send, "exchange_small")
    results = sum_adamw(parts, _pack([given[n] for n in small_names], F32), _pack([m_in[n] for n in small_names], F32),
                        _pack([v_in[n] for n in small_names], F32), "adamw_small")
    for store, res in zip((out_g, out_d, out_m, out_v), results):
        for n, arr in zip(small_names, _unpack(res, small_shapes)):
            store[n] = arr

    rep_shapes = [given[n].shape for n in REPLICATED]
    parts = all_gather_blocks(_pack([grads[n] for n in REPLICATED], F32), "gather_rep_grads")
    results = sum_adamw(parts, _pack([given[n] for n in REPLICATED], F32), _pack([m_in[n] for n in REPLICATED], F32),
                        _pack([v_in[n] for n in REPLICATED], F32), "adamw_replicated")
    for store, res in zip((out_g, out_d, out_m, out_v), results):
        for n, arr in zip(REPLICATED, _unpack(res, rep_shapes)):
            store[n] = arr

    return (loss, grads["x"], *[out_g[n] for n in WEIGHTS], *[out_d[n] for n in WEIGHTS],
            *[out_m[n] for n in WEIGHTS], *[out_v[n] for n in WEIGHTS])
```

```python
import functools
import math

import numpy as np
import jax
import jax.numpy as jnp
from jax import lax
from jax.experimental import pallas as pl
from jax.experimental.pallas import tpu as pltpu

F32 = jnp.float32
BF16 = jnp.bfloat16
MESH = pl.DeviceIdType.MESH
N_DEV = 8

D_MODEL = 1024
GRID_W = 64
ATT_HEADS, ATT_KV_HEADS, ATT_HEAD_DIM = 4, 2, 64
ATT_WIDTH, ATT_KV_WIDTH = 256, 128
ROPE_THETA = 10000.0
GLA_HEADS, GLA_DK, GLA_DV = 4, 64, 128
GLA_K_WIDTH, GLA_V_WIDTH = 256, 512
GLA_GATE_RANK, GLA_GATE_NORM, GLA_CHUNK = 16, 16.0, 64
S5_GROUPS, S5_GROUP_CH, S5_WIDTH, S5_STATE = 16, 16, 256, 64
S5_FLAT = S5_GROUPS * S5_STATE
N_BRANCH = 3
EPS = 1e-6
IN_SPLITS = (ATT_WIDTH, ATT_KV_WIDTH, ATT_KV_WIDTH, GLA_K_WIDTH, GLA_K_WIDTH, GLA_V_WIDTH, GLA_V_WIDTH,
             GLA_GATE_RANK, GLA_GATE_RANK, S5_WIDTH, N_BRANCH * D_MODEL)
D_IN = sum(IN_SPLITS)
D_IN_PAD = 5632

ADAM_LR, ADAM_B1, ADAM_B2, ADAM_EPS, ADAM_WD, ADAM_STEP = 0.001, 0.9, 0.999, 1e-08, 0.01, 10

VMEM_LIMIT = 56 * 1024 * 1024
PACK_W = 512
PACK_UNIT = 16 * PACK_W

WEIGHTS = ['c_ctx', 'ada_w', 'ada_b', 'norm_mix_pre', 'norm_mix_post', 'norm_ffn_pre', 'norm_ffn_post', 'w_in',
           'q_norm', 'k_norm', 'gla_gate_w', 'gla_gate_b', 'gla_out_norm', 's5_a_re', 's5_a_im', 's5_log_dt',
           's5_b_re', 's5_b_im', 's5_c_re', 's5_c_im', 's5_d', 's5_glu_w', 's5_glu_b', 'w_br_att', 'w_br_gla',
           'w_br_s5', 'w_out', 'ffn_up', 'ffn_conv_w', 'ffn_conv_b', 'ffn_down']
FWD_INPUTS = ['x', 'c', 'ctx'] + WEIGHTS
BIG = {'ada_w': 1, 'w_in': 1, 'w_br_att': 1, 'w_br_gla': 1, 'w_br_s5': 1, 'w_out': 0, 'ffn_up': 1, 'ffn_down': 0}
SMALL_SHARDED = {'gla_gate_w': 2, 'gla_gate_b': 1, 's5_glu_w': 0, 'ffn_conv_w': 1}
SHARDED = {**BIG, **SMALL_SHARDED}
MATMUL_WEIGHTS = list(BIG) + ['ffn_up_v']
REPLICATED = [n for n in WEIGHTS if n not in SHARDED]


def _pick(n, cands):
    for cand in cands:
        if n % cand == 0:
            return cand
    return n


def _params(sem):
    return pltpu.CompilerParams(dimension_semantics=sem, vmem_limit_bytes=VMEM_LIMIT)


_DIMS = {"nn": ((1,), (0,)), "nt": ((1,), (1,)), "tn": ((0,), (0,))}


def _mm(a, b, mode, name):
    if mode == "tn":
        K, M = a.shape
    else:
        M, K = a.shape
    N = b.shape[0] if mode == "nt" else b.shape[1]
    tm = _pick(M, (768, 512, 256, 128))
    tn = _pick(N, (512, 256, 128))
    tk = K if K <= 2816 else _pick(K, (1408, 1024, 768, 512, 256, 128))
    nk = K // tk
    dims = (_DIMS[mode], ((), ()))

    def body(a_ref, b_ref, o_ref):
        acc = lax.dot_general(a_ref[...].astype(BF16), b_ref[...].astype(BF16), dims, preferred_element_type=F32)
        if nk == 1:
            o_ref[...] = acc
        else:
            k = pl.program_id(2)

            @pl.when(k == 0)
            def _():
                o_ref[...] = acc

            @pl.when(k > 0)
            def _():
                o_ref[...] += acc

    a_spec = (pl.BlockSpec((tk, tm), lambda i, j, k: (k, i)) if mode == "tn"
              else pl.BlockSpec((tm, tk), lambda i, j, k: (i, k)))
    b_spec = (pl.BlockSpec((tn, tk), lambda i, j, k: (j, k)) if mode == "nt"
              else pl.BlockSpec((tk, tn), lambda i, j, k: (k, j)))
    return pl.pallas_call(
        body, name=name, grid=(M // tm, N // tn, nk),
        in_specs=[a_spec, b_spec], out_specs=pl.BlockSpec((tm, tn), lambda i, j, k: (i, j)),
        out_shape=jax.ShapeDtypeStruct((M, N), F32),
        compiler_params=_params(("parallel", "parallel", "arbitrary")),
    )(a, b)


@jax.custom_vjp
def matmul(a, w):
    return _mm(a.astype(BF16), w.astype(BF16), "nn", "mm_fwd")


def _matmul_fwd(a, w):
    ab, wb = a.astype(BF16), w.astype(BF16)
    return _mm(ab, wb, "nn", "mm_fwd"), (ab, wb)


def _matmul_bwd(res, dy):
    ab, wb = res
    dyb = dy.astype(BF16)
    return _mm(dyb, wb, "nt", "mm_dx"), _mm(ab, dyb, "tn", "mm_dw")


matmul.defvjp(_matmul_fwd, _matmul_bwd)


@jax.custom_vjp
def linear(a, w, w_grad_slot):
    del w_grad_slot
    return _mm(a.astype(BF16), w, "nn", "lin_fwd")


def _linear_fwd(a, w, w_grad_slot):
    del w_grad_slot
    ab = a.astype(BF16)
    return _mm(ab, w, "nn", "lin_fwd"), (ab, w)


def _linear_bwd(res, dy):
    ab, w = res
    dyb = dy.astype(BF16)
    return _mm(dyb, w, "nt", "lin_dx"), jnp.zeros_like(w), _mm(ab, dyb, "tn", "lin_dw")


linear.defvjp(_linear_fwd, _linear_bwd)


def _attn_fwd_call(q, k, v):
    H, Tq, d = q.shape
    KV, Tk, _ = k.shape
    G = H // KV
    tq = _pick(Tq, (256, 128, 64))

    def body(q_ref, k_ref, v_ref, o_ref, lse_ref):
        s = lax.dot_general(q_ref[0], k_ref[0], (_DIMS["nt"], ((), ())), preferred_element_type=F32)
        m = jnp.max(s, axis=1, keepdims=True)
        p = jnp.exp(s - m)
        l = jnp.sum(p, axis=1, keepdims=True)
        o_ref[0] = jnp.dot(p.astype(BF16), v_ref[0], preferred_element_type=F32) * (1.0 / l)
        lse_ref[0] = m + jnp.log(l)

    return pl.pallas_call(
        body, name="attn_fwd", grid=(H, Tq // tq),
        in_specs=[pl.BlockSpec((1, tq, d), lambda h, i: (h, i, 0)),
                  pl.BlockSpec((1, Tk, d), lambda h, i: (h // G, 0, 0)),
                  pl.BlockSpec((1, Tk, d), lambda h, i: (h // G, 0, 0))],
        out_specs=[pl.BlockSpec((1, tq, d), lambda h, i: (h, i, 0)),
                   pl.BlockSpec((1, tq, 1), lambda h, i: (h, i, 0))],
        out_shape=[jax.ShapeDtypeStruct((H, Tq, d), F32), jax.ShapeDtypeStruct((H, Tq, 1), F32)],
        compiler_params=_params(("parallel", "parallel")),
    )(q, k, v)


def _attn_bwd_call(q, k, v, o, lse, do, scale):
    H, Tq, d = q.shape
    KV, Tk, _ = k.shape
    G = H // KV
    tq = _pick(Tq, (256, 128, 64))
    ck = _pick(Tk, (1408, 1024, 512, 256, 128, 64))
    nck = Tk // ck

    def body(q_ref, k_ref, v_ref, o_ref, lse_ref, do_ref, dq_ref, dk_ref, dv_ref):
        @pl.when((pl.program_id(1) == 0) & (pl.program_id(2) == 0))
        def _():
            dk_ref[...] = jnp.zeros_like(dk_ref)
            dv_ref[...] = jnp.zeros_like(dv_ref)

        qb = q_ref[0]
        do = do_ref[0]
        dob = do.astype(BF16)
        delta = jnp.sum(do * o_ref[0], axis=1, keepdims=True)
        lse = lse_ref[0]
        dq = jnp.zeros((tq, d), F32)
        for cidx in range(nck):
            rows = slice(cidx * ck, (cidx + 1) * ck)
            ks = k_ref[0, rows, :]
            vs = v_ref[0, rows, :]
            s = lax.dot_general(qb, ks, (_DIMS["nt"], ((), ())), preferred_element_type=F32)
            p = jnp.exp(s - lse)
            dv_ref[0, rows, :] += lax.dot_general(p.astype(BF16), dob, (_DIMS["tn"], ((), ())),
                                                  preferred_element_type=F32)
            dp = lax.dot_general(dob, vs, (_DIMS["nt"], ((), ())), preferred_element_type=F32)
            dsb = (p * (dp - delta)).astype(BF16)
            dq = dq + jnp.dot(dsb, ks, preferred_element_type=F32)
            dk_ref[0, rows, :] += lax.dot_general(dsb, qb, (_DIMS["tn"], ((), ())), preferred_element_type=F32)
        dq_ref[0] = dq * scale

    q_spec = pl.BlockSpec((1, tq, d), lambda kv, g, i: (kv * G + g, i, 0))
    kv_spec = pl.BlockSpec((1, Tk, d), lambda kv, g, i: (kv, 0, 0))
    return pl.pallas_call(
        body, name="attn_bwd", grid=(KV, G, Tq // tq),
        in_specs=[q_spec, kv_spec, kv_spec, q_spec,
                  pl.BlockSpec((1, tq, 1), lambda kv, g, i: (kv * G + g, i, 0)), q_spec],
        out_specs=[q_spec, kv_spec, kv_spec],
        out_shape=[jax.ShapeDtypeStruct((H, Tq, d), F32), jax.ShapeDtypeStruct((KV, Tk, d), F32),
                   jax.ShapeDtypeStruct((KV, Tk, d), F32)],
        compiler_params=_params(("arbitrary", "arbitrary", "arbitrary")),
    )(q, k, v, o, lse, do)


ATT_SCALE = ATT_HEAD_DIM ** -0.5


@jax.custom_vjp
def attention(q, k, v):
    return _attn_fwd_call((q * ATT_SCALE).astype(BF16), k.astype(BF16), v.astype(BF16))[0]


def _attention_fwd(q, k, v):
    qb, kb, vb = (q * ATT_SCALE).astype(BF16), k.astype(BF16), v.astype(BF16)
    o, lse = _attn_fwd_call(qb, kb, vb)
    return o, (qb, kb, vb, o, lse)


def _attention_bwd(res, do):
    qb, kb, vb, o, lse = res
    return _attn_bwd_call(qb, kb, vb, o, lse, do, ATT_SCALE)


attention.defvjp(_attention_fwd, _attention_bwd)


_ORDER_DOWN = {0: False, 1: True, 2: True, 3: False}
_ORDER_ADJOINT = {0: 2, 1: 3}


def _scan_tables(a_re, a_im, down):
    pw_re, pw_im = [a_re], [a_im]
    for _ in range(7):
        pw_re, pw_im = (pw_re + [pw_re[-1] * a_re - pw_im[-1] * a_im],
                        pw_im + [pw_re[-1] * a_im + pw_im[-1] * a_re])
    carry_rows = list(range(7, -1, -1)) if down else list(range(8))
    rows_re = [pw_re[r] for r in carry_rows] + [pw_re[0], pw_re[1], pw_re[3]]
    rows_im = [pw_im[r] for r in carry_rows] + [pw_im[0], pw_im[1], pw_im[3]]
    tab = jnp.concatenate([jnp.stack(rows_re), jnp.stack(rows_im)], axis=1)
    return jnp.concatenate([tab, jnp.zeros((5, 2 * S5_FLAT), F32)], axis=0)


def _scan_call(bu, a_re, a_im, order, ctx_len):
    T, W2 = bu.shape
    P = W2 // 2
    rb = _pick(math.gcd(ctx_len, T - ctx_len), (256, 128, 64, 32, 16, 8))
    nblk, cb = T // rb, ctx_len // rb
    ntile = rb // 8
    down = _ORDER_DOWN[order]
    tab = _scan_tables(a_re, a_im, down)

    def blk(n):
        return _block_in_order(n, nblk, cb, order)

    def body(bu_ref, tab_ref, s_ref, carry_ref):
        @pl.when(pl.program_id(0) == 0)
        def _():
            carry_ref[...] = jnp.zeros_like(carry_ref)

        row = lax.broadcasted_iota(jnp.int32, (8, P), 0)
        cp_re, cp_im = tab_ref[0:8, 0:P], tab_ref[0:8, P:2 * P]
        steps = []
        for j, sh in enumerate((1, 2, 4)):
            keep = (row < 8 - sh) if down else (row >= sh)
            steps.append((8 - sh if down else sh, keep, tab_ref[8 + j:9 + j, 0:P], tab_ref[8 + j:9 + j, P:2 * P]))

        def tile(j, carry):
            c_re, c_im = carry
            t = (ntile - 1 - j) if down else j
            r0 = pl.multiple_of(t * 8, 8)
            x_re = bu_ref[pl.ds(r0, 8), 0:P]
            x_im = bu_ref[pl.ds(r0, 8), P:2 * P]
            for shift, keep, p_re, p_im in steps:
                y_re = jnp.where(keep, pltpu.roll(x_re, shift, 0), 0.0)
                y_im = jnp.where(keep, pltpu.roll(x_im, shift, 0), 0.0)
                x_re, x_im = x_re + p_re * y_re - p_im * y_im, x_im + p_re * y_im + p_im * y_re
            x_re, x_im = x_re + cp_re * c_re - cp_im * c_im, x_im + cp_re * c_im + cp_im * c_re
            s_ref[pl.ds(r0, 8), 0:P] = x_re
            s_ref[pl.ds(r0, 8), P:2 * P] = x_im
            last = 0 if down else 7
            return x_re[last:last + 1, :], x_im[last:last + 1, :]

        c_re, c_im = lax.fori_loop(0, ntile, tile, (carry_ref[0:1, 0:P], carry_ref[0:1, P:2 * P]))
        carry_ref[0:1, 0:P] = c_re
        carry_ref[0:1, P:2 * P] = c_im

    return pl.pallas_call(
        body, name=f"s5_scan_{order}", grid=(nblk,),
        in_specs=[pl.BlockSpec((rb, W2), lambda n: (blk(n), 0)), pl.BlockSpec((16, W2), lambda n: (0, 0))],
        out_specs=pl.BlockSpec((rb, W2), lambda n: (blk(n), 0)),
        out_shape=jax.ShapeDtypeStruct((T, W2), F32),
        scratch_shapes=[pltpu.VMEM((8, W2), F32)],
        compiler_params=_params(("arbitrary",)),
    )(bu, tab)


def _prev_in_order(s, order, ctx_len):
    zero = jnp.zeros_like(s[:1])
    if order == 0:
        return jnp.concatenate([zero, s[:-1]], axis=0)
    return jnp.concatenate([s[1:ctx_len], zero, s[ctx_len + 1:], s[:1]], axis=0)


@functools.partial(jax.custom_vjp, nondiff_argnums=(3, 4))
def s5_scan(bu, a_re, a_im, order, ctx_len):
    return _scan_call(bu, a_re, a_im, order, ctx_len)


def _s5_scan_fwd(bu, a_re, a_im, order, ctx_len):
    s = _scan_call(bu, a_re, a_im, order, ctx_len)
    return s, (s, a_re, a_im)


def _s5_scan_bwd(order, ctx_len, res, ds):
    s, a_re, a_im = res
    lam = _scan_call(ds, a_re, -a_im, _ORDER_ADJOINT[order], ctx_len)
    P = a_re.shape[0]
    sp = _prev_in_order(s, order, ctx_len)
    l_re, l_im, p_re, p_im = lam[:, :P], lam[:, P:], sp[:, :P], sp[:, P:]
    g_re = jnp.sum(l_re * p_re + l_im * p_im, axis=0)
    g_im = jnp.sum(l_im * p_re - l_re * p_im, axis=0)
    return lam, g_re, g_im


s5_scan.defvjp(_s5_scan_fwd, _s5_scan_bwd)


def _dot(a, b, mode, precision=None):
    return lax.dot_general(a, b, (_DIMS[mode], ((), ())), preferred_element_type=F32, precision=precision)


def _block_in_order(n, nblk, cblk, order):
    if order == 0:
        return n
    if order == 1:
        return jnp.where(n < cblk, cblk - 1 - n, nblk - 1 - (n - cblk))
    if order == 2:
        return nblk - 1 - n
    return jnp.where(n < nblk - cblk, cblk + n, n - (nblk - cblk))


def _gla_chunk_terms(qn, kn, gn, tri, reverse):
    b = _dot(tri, gn, "tn" if reverse else "nn", lax.Precision.HIGHEST)
    edge = 0 if reverse else GLA_CHUNK - 1
    b_end = b[edge:edge + 1, :]
    e_pos, e_neg, e_end = jnp.exp(b), jnp.exp(-b), jnp.exp(b_end - b)
    return b_end, e_pos, e_neg, e_end, qn * e_pos, kn * e_neg, kn * e_end


def _gla_masks():
    L = GLA_CHUNK
    rows, cols = lax.broadcasted_iota(jnp.int32, (L, L), 0), lax.broadcasted_iota(jnp.int32, (L, L), 1)
    return rows >= cols, rows <= cols


def _gla_blocking(n_t, ctx_len):
    n_chunks, ctx_chunks = n_t // GLA_CHUNK, ctx_len // GLA_CHUNK
    per_block = _pick(math.gcd(ctx_chunks, n_chunks - ctx_chunks), (4, 2, 1))
    return n_chunks, per_block, GLA_CHUNK * per_block, n_chunks // per_block, ctx_chunks // per_block


def _gla_fwd_call(q, k, v, g, reverse, ctx_len):
    H, T, dk = q.shape
    dv = v.shape[-1]
    n_chunks, cb, rb, nb, cblk = _gla_blocking(T, ctx_len)
    L = GLA_CHUNK
    order = 1 if reverse else 0

    def body(q_ref, k_ref, v_ref, g_ref, o_ref, sb_ref, s_ref):
        @pl.when(pl.program_id(0) == 0)
        def _():
            s_ref[...] = jnp.zeros_like(s_ref)

        lower, upper = _gla_masks()
        tri = lower.astype(F32)
        seen = upper if reverse else lower
        ones = jnp.ones((L, dv), F32)
        states = [s_ref[h] for h in range(H)]
        for n in (reversed(range(cb)) if reverse else range(cb)):
            rows = slice(n * L, (n + 1) * L)
            for h in range(H):
                qn, kn, vn, gn = q_ref[h, rows, :], k_ref[h, rows, :], v_ref[h, rows, :], g_ref[h, rows, :]
                _, _, _, _, q_in, k_in, k_end = _gla_chunk_terms(qn, kn, gn, tri, reverse)
                att = jnp.where(seen, _dot(q_in.astype(BF16), k_in.astype(BF16), "nt"), 0.0)
                vb = vn.astype(BF16)
                sb_ref[h, n] = states[h]
                o_ref[h, rows, :] = (_dot(att.astype(BF16), vb, "nn")
                                     + _dot(q_in.astype(BF16), states[h].astype(BF16), "nn"))
                decay = jnp.exp(_dot(gn, ones, "tn", lax.Precision.HIGHEST))
                states[h] = decay * states[h] + _dot(k_end.astype(BF16), vb, "tn")
        for h in range(H):
            s_ref[h] = states[h]

    def at(i):
        return 0, _block_in_order(i, nb, cblk, order), 0

    row_k, row_v = pl.BlockSpec((H, rb, dk), at), pl.BlockSpec((H, rb, dv), at)
    return pl.pallas_call(
        body, name="gla_fwd", grid=(nb,),
        in_specs=[row_k, row_k, row_v, row_k],
        out_specs=[row_v, pl.BlockSpec((H, cb, dk, dv), lambda i: (*at(i), 0))],
        out_shape=[jax.ShapeDtypeStruct((H, T, dv), F32), jax.ShapeDtypeStruct((H, n_chunks, dk, dv), F32)],
        scratch_shapes=[pltpu.VMEM((H, dk, dv), F32)],
        compiler_params=_params(("arbitrary",)),
    )(q, k, v, g)


def _gla_bwd_call(q, k, v, g, sb, do, reverse, ctx_len):
    H, T, dk = q.shape
    dv = v.shape[-1]
    n_chunks, cb, rb, nb, cblk = _gla_blocking(T, ctx_len)
    L = GLA_CHUNK
    order = 3 if reverse else 2

    def body(q_ref, k_ref, v_ref, g_ref, sb_ref, do_ref, dq_ref, dk_ref, dv_ref, dg_ref, ds_ref):
        @pl.when(pl.program_id(0) == 0)
        def _():
            ds_ref[...] = jnp.zeros_like(ds_ref)

        lower, upper = _gla_masks()
        tri = lower.astype(F32)
        seen = upper if reverse else lower
        ones = jnp.ones((L, dv), F32)
        ones8 = jnp.ones((8, dv), F32)
        d_states = [ds_ref[h] for h in range(H)]
        for n in (range(cb) if reverse else reversed(range(cb))):
            rows = slice(n * L, (n + 1) * L)
            for h in range(H):
                qn, kn, vn, gn = q_ref[h, rows, :], k_ref[h, rows, :], v_ref[h, rows, :], g_ref[h, rows, :]
                state, d_state = sb_ref[h, n], d_states[h]
                b_end, e_pos, e_neg, e_end, q_in, k_in, k_end = _gla_chunk_terms(qn, kn, gn, tri, reverse)
                q_b, k_b, ke_b, vb = q_in.astype(BF16), k_in.astype(BF16), k_end.astype(BF16), vn.astype(BF16)
                dob = do_ref[h, rows, :].astype(BF16)
                dsb = d_state.astype(BF16)
                att = jnp.where(seen, _dot(q_b, k_b, "nt"), 0.0).astype(BF16)
                d_att = jnp.where(seen, _dot(dob, vb, "nt"), 0.0).astype(BF16)
                d_qin = _dot(d_att, k_b, "nn") + _dot(dob, state.astype(BF16), "nt")
                d_kin = _dot(d_att, q_b, "tn")
                d_kend = _dot(vb, dsb, "nt")
                dv_ref[h, rows, :] = _dot(att, dob, "tn") + _dot(ke_b, dsb, "nn")
                through_decay = _dot(ones8, state * d_state, "nt", lax.Precision.HIGHEST)[0:1, :]
                d_bend = jnp.sum(d_kend * k_end, axis=0, keepdims=True) + jnp.exp(b_end) * through_decay
                d_b = d_qin * q_in - d_kin * k_in - d_kend * k_end
                dg_ref[h, rows, :] = _dot(tri, d_b, "nn" if reverse else "tn", lax.Precision.HIGHEST) + d_bend
                dq_ref[h, rows, :] = d_qin * e_pos
                dk_ref[h, rows, :] = d_kin * e_neg + d_kend * e_end
                decay = jnp.exp(_dot(gn, ones, "tn", lax.Precision.HIGHEST))
                d_states[h] = _dot(q_b, dob, "tn") + decay * d_state
        for h in range(H):
            ds_ref[h] = d_states[h]

    def at(i):
        return 0, _block_in_order(i, nb, cblk, order), 0

    row_k, row_v = pl.BlockSpec((H, rb, dk), at), pl.BlockSpec((H, rb, dv), at)
    return pl.pallas_call(
        body, name="gla_bwd", grid=(nb,),
        in_specs=[row_k, row_k, row_v, row_k, pl.BlockSpec((H, cb, dk, dv), lambda i: (*at(i), 0)), row_v],
        out_specs=[row_k, row_k, row_v, row_k],
        out_shape=[jax.ShapeDtypeStruct((H, T, dk), F32), jax.ShapeDtypeStruct((H, T, dk), F32),
                   jax.ShapeDtypeStruct((H, T, dv), F32), jax.ShapeDtypeStruct((H, T, dk), F32)],
        scratch_shapes=[pltpu.VMEM((H, dk, dv), F32)],
        compiler_params=_params(("arbitrary",)),
    )(q, k, v, g, sb, do)


@functools.partial(jax.custom_vjp, nondiff_argnums=(4, 5))
def gla_scan(q, k, v, g, reverse, ctx_len):
    return _gla_fwd_call(q, k, v, g, reverse, ctx_len)[0]


def _gla_scan_fwd(q, k, v, g, reverse, ctx_len):
    o, sb = _gla_fwd_call(q, k, v, g, reverse, ctx_len)
    return o, (q, k, v, g, sb)


def _gla_scan_bwd(reverse, ctx_len, res, do):
    q, k, v, g, sb = res
    return _gla_bwd_call(q, k, v, g, sb, do, reverse, ctx_len)


gla_scan.defvjp(_gla_scan_fwd, _gla_scan_bwd)


@functools.partial(jax.custom_vjp, nondiff_argnums=(1,))
def split_cols(x, sizes):
    points = np.cumsum((0,) + tuple(sizes))
    return tuple(x[:, int(a):int(b)] for a, b in zip(points[:-1], points[1:]))


def _split_cols_fwd(x, sizes):
    return split_cols(x, sizes), x.shape[1]


def _split_cols_bwd(sizes, width, cts):
    parts = list(cts)
    if width > sum(sizes):
        parts.append(jnp.zeros((cts[0].shape[0], width - sum(sizes)), cts[0].dtype))
    return (jnp.concatenate(parts, axis=1),)


split_cols.defvjp(_split_cols_fwd, _split_cols_bwd)


def _position():
    return lax.axis_index("x"), lax.axis_index("y"), lax.axis_index("c")


def all_gather_blocks(shard, name):
    R, W = shard.shape

    def body(x_ref, out_ref, send_sems, recv_sems, local_sem):
        x, y, c = _position()
        me, sibling = (x, y, c), (x, y, 1 - c)
        chips = [(1 - x, y), (x, 1 - y), (1 - x, 1 - y)]

        def slot(px, py, pc):
            return out_ref.at[4 * px + 2 * py + pc]

        def copy(k, block, to, src=None):
            return pltpu.make_async_remote_copy(
                src_ref=slot(*block) if src is None else src, dst_ref=slot(*block),
                send_sem=send_sems.at[k], recv_sem=recv_sems.at[k], device_id=to, device_id_type=MESH)

        mine = pltpu.make_async_copy(x_ref, slot(*me), local_sem)
        mine.start()
        first = [copy(0, me, sibling, src=x_ref)]
        first += [copy(1 + j, me, (*chip, c), src=x_ref) for j, chip in enumerate(chips)]
        for cp in first:
            cp.start()
        passed = [copy(4 + j, (*chip, c), sibling) for j, chip in enumerate(chips)]
        for j, chip in enumerate(chips):
            copy(1 + j, (*chip, c), me).wait_recv()
            passed[j].start()
        copy(0, sibling, me).wait_recv()
        for j, chip in enumerate(chips):
            copy(4 + j, (*chip, 1 - c), me).wait_recv()
        for cp in first + passed:
            cp.wait_send()
        mine.wait()

    return pl.pallas_call(
        body, name=name,
        out_shape=jax.ShapeDtypeStruct((N_DEV, R, W), shard.dtype),
        in_specs=[pl.BlockSpec(memory_space=pltpu.HBM)], out_specs=pl.BlockSpec(memory_space=pltpu.HBM),
        scratch_shapes=[pltpu.SemaphoreType.DMA((7,)), pltpu.SemaphoreType.DMA((7,)), pltpu.SemaphoreType.DMA],
    )(shard)


def exchange_blocks(blocks, name):
    _, R, W = blocks.shape
    flips = [(fx, fy, fc) for fx in (0, 1) for fy in (0, 1) for fc in (0, 1)][1:]

    def body(x_ref, out_ref, send_sems, recv_sems, local_sem):
        x, y, c = _position()
        me = 4 * x + 2 * y + c
        mine = pltpu.make_async_copy(x_ref.at[me], out_ref.at[me], local_sem)
        mine.start()
        copies = []
        for k, (fx, fy, fc) in enumerate(flips):
            px, py, pc = x ^ fx, y ^ fy, c ^ fc
            peer = 4 * px + 2 * py + pc
            copies.append((
                pltpu.make_async_remote_copy(src_ref=x_ref.at[peer], dst_ref=out_ref.at[me],
                                             send_sem=send_sems.at[k], recv_sem=recv_sems.at[k],
                                             device_id=(px, py, pc), device_id_type=MESH),
                pltpu.make_async_remote_copy(src_ref=x_ref.at[peer], dst_ref=out_ref.at[peer],
                                             send_sem=send_sems.at[k], recv_sem=recv_sems.at[k],
                                             device_id=(px, py, pc), device_id_type=MESH)))
        for send, _ in copies:
            send.start()
        for _, recv in copies:
            recv.wait_recv()
        for send, _ in copies:
            send.wait_send()
        mine.wait()

    return pl.pallas_call(
        body, name=name,
        out_shape=jax.ShapeDtypeStruct(blocks.shape, blocks.dtype),
        in_specs=[pl.BlockSpec(memory_space=pltpu.HBM)], out_specs=pl.BlockSpec(memory_space=pltpu.HBM),
        scratch_shapes=[pltpu.SemaphoreType.DMA((7,)), pltpu.SemaphoreType.DMA((7,)), pltpu.SemaphoreType.DMA],
    )(blocks)


def sum_adamw(parts, w, m, v, name):
    _, R, W = parts.shape
    tr = _pick(R, (512, 256, 128, 64, 32, 16, 8))

    def body(p_ref, w_ref, m_ref, v_ref, g_out, d_out, m_out, v_out):
        g = p_ref[0].astype(F32)
        for j in range(1, N_DEV):
            g = g + p_ref[j].astype(F32)
        m_new = ADAM_B1 * m_ref[...] + (1.0 - ADAM_B1) * g
        v_new = ADAM_B2 * v_ref[...] + (1.0 - ADAM_B2) * (g * g)
        m_hat = m_new / (1.0 - ADAM_B1 ** ADAM_STEP)
        v_hat = v_new / (1.0 - ADAM_B2 ** ADAM_STEP)
        g_out[...] = g
        d_out[...] = -ADAM_LR * (m_hat / (jnp.sqrt(v_hat) + ADAM_EPS) + ADAM_WD * w_ref[...])
        m_out[...] = m_new
        v_out[...] = v_new

    row = pl.BlockSpec((tr, W), lambda i: (i, 0))
    return pl.pallas_call(
        body, name=name, grid=(R // tr,),
        in_specs=[pl.BlockSpec((N_DEV, tr, W), lambda i: (0, i, 0)), row, row, row],
        out_specs=[row, row, row, row],
        out_shape=[jax.ShapeDtypeStruct((R, W), F32)] * 4,
        compiler_params=_params(("parallel",)),
    )(parts, w, m, v)


def _padded(n):
    return -(-n // PACK_UNIT) * PACK_UNIT


def _pack(arrays, dtype):
    segs = []
    for arr in arrays:
        flat = arr.reshape(-1).astype(dtype)
        segs.append(jnp.pad(flat, (0, _padded(flat.size) - flat.size)))
    return jnp.concatenate(segs).reshape(-1, PACK_W)


def _pack_blocks(arrays, dtype):
    segs = []
    for arr in arrays:
        flat = arr.reshape(N_DEV, -1).astype(dtype)
        segs.append(jnp.pad(flat, ((0, 0), (0, _padded(flat.shape[1]) - flat.shape[1]))))
    return jnp.concatenate(segs, axis=1).reshape(N_DEV, -1, PACK_W)


def _unpack(buf, shapes):
    lead = buf.shape[:-2]
    flat = buf.reshape(*lead, -1)
    out, off = [], 0
    for shape in shapes:
        n = int(np.prod(shape))
        out.append(flat[..., off:off + n].reshape(*lead, *shape))
        off += _padded(n)
    return out


def _round_up(n, unit):
    return -(-n // unit) * unit


def _pack_rows(arrays, width, dtype):
    parts = []
    for arr in arrays:
        r, c = arr.shape[-2:]
        pad = [(0, 0)] * (arr.ndim - 2) + [(0, _round_up(r, 16) - r), (0, width - c)]
        parts.append(jnp.pad(arr.astype(dtype), pad))
    return jnp.concatenate(parts, axis=-2)


def _unpack_rows(buf, shapes):
    out, off = [], 0
    for r, c in shapes:
        out.append(buf[..., off:off + r, :c])
        off += _round_up(r, 16)
    return out


def _to_full(blocks, axis):
    moved = jnp.moveaxis(blocks, 0, axis)
    shape = list(moved.shape)
    shape[axis:axis + 2] = [shape[axis] * shape[axis + 1]]
    return moved.reshape(shape)


def _to_blocks(full, axis):
    shape = list(full.shape)
    shape[axis:axis + 1] = [N_DEV, shape[axis] // N_DEV]
    return jnp.moveaxis(full.reshape(shape), axis, 0)


def rms_norm(x, gain):
    return x * lax.rsqrt(jnp.mean(x * x, axis=-1, keepdims=True) + EPS) * gain


def _rope_tables(n_tokens):
    rows = n_tokens // GRID_W
    row = jnp.repeat(jnp.arange(rows, dtype=F32), GRID_W)
    col = jnp.tile(jnp.arange(GRID_W, dtype=F32), rows)
    n_freq = ATT_HEAD_DIM // 4
    inv_freq = ROPE_THETA ** (-jnp.arange(n_freq, dtype=F32) / n_freq)
    ang = jnp.stack([row[:, None] * inv_freq, col[:, None] * inv_freq], axis=1)
    return jnp.cos(ang), jnp.sin(ang)


def _rope(x, cos, sin):
    n_t, nh, hd = x.shape
    xr = x.reshape(n_t, nh, 2, 2, hd // 4)
    x1, x2 = xr[..., 0, :], xr[..., 1, :]
    cs, sn = cos[:, None], sin[:, None]
    return jnp.stack([x1 * cs - x2 * sn, x2 * cs + x1 * sn], axis=-2).reshape(n_t, nh, hd)


W_IN_ORDER = (0, 1, 2, 3, 4, 5, 6, 9, 10, 7, 8)
W_IN_SIZES = tuple(IN_SPLITS[s] for s in W_IN_ORDER)


def _w_in_reorder(w, to_kernel_order):
    if to_kernel_order:
        points = np.cumsum((0,) + IN_SPLITS)
        pieces = [w[..., int(points[s]):int(points[s + 1])] for s in W_IN_ORDER]
    else:
        points = np.cumsum((0,) + W_IN_SIZES)
        where = {s: j for j, s in enumerate(W_IN_ORDER)}
        pieces = [w[..., int(points[where[s]]):int(points[where[s] + 1])] for s in range(len(IN_SPLITS))]
    return jnp.concatenate(pieces, axis=-1)


def _s5_discretize(a_re, a_im, log_dt, b_re, b_im):
    dt = jnp.exp(log_dt)[:, None]
    mag = jnp.exp(a_re * dt)
    ab_re, ab_im = mag * jnp.cos(a_im * dt), mag * jnp.sin(a_im * dt)
    den = a_re * a_re + a_im * a_im
    f_re = ((ab_re - 1.0) * a_re + ab_im * a_im) / den
    f_im = (ab_im * a_re - (ab_re - 1.0) * a_im) / den
    bb_re = f_re[..., None] * b_re - f_im[..., None] * b_im
    bb_im = f_re[..., None] * b_im + f_im[..., None] * b_re
    return ab_re, ab_im, bb_re, bb_im


def _s5_direction(u, lp, d, ctx_len):
    ab_re, ab_im, bb_re, bb_im = _s5_discretize(lp["s5_a_re"][d], lp["s5_a_im"][d], lp["s5_log_dt"][d],
                                                lp["s5_b_re"][d], lp["s5_b_im"][d])
    eye = jnp.eye(S5_GROUPS, dtype=F32)
    b_cat = jnp.concatenate([jnp.einsum("gph,gk->ghkp", bb_re, eye).reshape(S5_WIDTH, S5_FLAT),
                             jnp.einsum("gph,gk->ghkp", bb_im, eye).reshape(S5_WIDTH, S5_FLAT)], axis=1)
    c_cat = jnp.concatenate([jnp.einsum("ghp,gk->gpkh", lp["s5_c_re"][d], eye).reshape(S5_FLAT, S5_WIDTH),
                             -jnp.einsum("ghp,gk->gpkh", lp["s5_c_im"][d], eye).reshape(S5_FLAT, S5_WIDTH)], axis=0)
    bu = matmul(u, b_cat)
    s = s5_scan(bu, ab_re.reshape(-1), ab_im.reshape(-1), d, ctx_len)
    return matmul(s, c_cat)


def _s5_branch(u, lp, ctx_len):
    y = _s5_direction(u, lp, 0, ctx_len) + _s5_direction(u, lp, 1, ctx_len) + lp["s5_d"] * u
    y = jax.nn.gelu(y)
    return y * jax.nn.sigmoid(matmul(y, lp["s5_glu_w"]) + lp["s5_glu_b"])


def _heads(a, nh):
    return a.reshape(a.shape[0], nh, a.shape[1] // nh).transpose(1, 0, 2)


def _token_mixer(h, lp, rope, ctx_len, with_ctx_out):
    n_t = h.shape[0]
    proj = linear(h, lp["w_in"], lp["w_in_slot"])
    aq, ak, av, gq, gk, gv, gr, su, bg, glf, glb = split_cols(proj, W_IN_SIZES)

    aq = rms_norm(aq.reshape(n_t, ATT_HEADS, ATT_HEAD_DIM), lp["q_norm"])
    ak = rms_norm(ak.reshape(n_t, ATT_KV_HEADS, ATT_HEAD_DIM), lp["k_norm"])
    aq = jnp.concatenate([aq[:ctx_len], _rope(aq[ctx_len:], *rope)], axis=0).transpose(1, 0, 2)
    ak = jnp.concatenate([ak[:ctx_len], _rope(ak[ctx_len:], *rope)], axis=0).transpose(1, 0, 2)
    av = av.reshape(n_t, ATT_KV_HEADS, ATT_HEAD_DIM).transpose(1, 0, 2)
    o_att_lat = attention(aq[:, ctx_len:], ak, av)
    if with_ctx_out:
        o_att_ctx = attention(aq[:, :ctx_len], ak[:, :ctx_len], av[:, :ctx_len])
        o_att = jnp.concatenate([o_att_ctx, o_att_lat], axis=1)
    else:
        o_att = o_att_lat
    o_att = o_att.transpose(1, 0, 2).reshape(-1, ATT_WIDTH)

    def log_decay(low, d):
        z = jnp.dot(low, lp["gla_gate_w"][d]) + lp["gla_gate_b"][d]
        return _heads(jax.nn.log_sigmoid(z) / GLA_GATE_NORM, GLA_HEADS)

    q_g, k_g, v_g = _heads(gq, GLA_HEADS) * (GLA_DK ** -0.5), _heads(gk, GLA_HEADS), _heads(gv, GLA_HEADS)
    o_f = gla_scan(q_g, k_g, v_g, log_decay(glf, 0), False, ctx_len)
    o_b = gla_scan(q_g, k_g, v_g, log_decay(glb, 1), True, ctx_len)
    o_gla = rms_norm((o_f + o_b).transpose(1, 0, 2), lp["gla_out_norm"]).reshape(n_t, GLA_V_WIDTH)
    o_gla = o_gla * jax.nn.silu(gr)

    o_s5 = _s5_branch(su, lp, ctx_len)

    if not with_ctx_out:
        o_gla, o_s5, bg = o_gla[ctx_len:], o_s5[ctx_len:], bg[ctx_len:]
    g_att, g_gla, g_s5 = split_cols(jax.nn.sigmoid(bg), (D_MODEL,) * N_BRANCH)
    merged = (g_att * linear(o_att, lp["w_br_att"], lp["w_br_att_slot"])
              + g_gla * linear(o_gla, lp["w_br_gla"], lp["w_br_gla_slot"])
              + g_s5 * linear(o_s5, lp["w_br_s5"], lp["w_br_s5_slot"]))
    return linear(merged, lp["w_out"], lp["w_out_slot"])


HALO = 8


def _ffn_mid_blocking(n_rows, half):
    return _pick(n_rows, (1056, 1024, 256, 128, 64, 32, 16, 8)), _pick(half, (256, 128))


def _ffn_mid_specs(n_rows, rb, tc):
    per = rb // HALO
    return [pl.BlockSpec((rb, tc), lambda j, i: (i, j)),
            pl.BlockSpec((HALO, tc), lambda j, i: (jnp.maximum(i * per - 1, 0), j)),
            pl.BlockSpec((HALO, tc), lambda j, i: (jnp.minimum((i + 1) * per, n_rows // HALO - 1), j))]


def _with_halo(main_ref, prev_ref, next_ref):
    return jnp.concatenate([prev_ref[...], main_ref[...], next_ref[...]], axis=0)


def _row_neighbours(ext, first_row, n_rows, starts):
    n = ext.shape[0]
    row = lax.broadcasted_iota(jnp.int32, ext.shape, 0) + first_row
    first = functools.reduce(jnp.logical_or, [row == s for s in starts])
    last = functools.reduce(jnp.logical_or, [row == e - 1 for e in tuple(starts[1:]) + (n_rows,)])
    return jnp.where(first, 0.0, pltpu.roll(ext, 1, 0)), jnp.where(last, 0.0, pltpu.roll(ext, n - 1, 0))


def _ffn_mid_fwd_call(u_a, u_v, taps, starts):
    n_rows, half = u_a.shape
    rb, tc = _ffn_mid_blocking(n_rows, half)
    specs = _ffn_mid_specs(n_rows, rb, tc)
    nj = half // tc

    def body(am, ap, an, vm, vp, vn, wa, wv, o_ref):
        first_row = pl.program_id(1) * rb - HALO

        def conv(ext, w):
            above, below = _row_neighbours(ext, first_row, n_rows, starts)
            return above * w[0:1, :] + ext * w[1:2, :] + below * w[2:3, :] + w[3:4, :]

        ca = conv(_with_halo(am, ap, an), wa)[HALO:HALO + rb]
        cv = conv(_with_halo(vm, vp, vn), wv)[HALO:HALO + rb]
        o_ref[...] = ca * jax.nn.sigmoid(ca) * cv

    return pl.pallas_call(
        body, name="ffn_mid_fwd", grid=(nj, n_rows // rb),
        in_specs=specs + specs + [pl.BlockSpec((8, tc), lambda j, i: (0, j)),
                                  pl.BlockSpec((8, tc), lambda j, i: (0, nj + j))],
        out_specs=pl.BlockSpec((rb, tc), lambda j, i: (i, j)),
        out_shape=jax.ShapeDtypeStruct((n_rows, half), F32),
        compiler_params=_params(("parallel", "parallel")),
    )(u_a, u_a, u_a, u_v, u_v, u_v, taps, taps)


def _ffn_mid_bwd_call(u_a, u_v, taps, d_act, starts):
    n_rows, half = u_a.shape
    rb, tc = _ffn_mid_blocking(n_rows, half)
    specs = _ffn_mid_specs(n_rows, rb, tc)
    nj = half // tc
    main = slice(HALO, HALO + rb)

    def body(am, ap, an, vm, vp, vn, wa, wv, dm, dp, dn, dua_ref, duv_ref, dwa_ref, dwv_ref):
        i = pl.program_id(1)
        first_row = i * rb - HALO
        neighbours = functools.partial(_row_neighbours, first_row=first_row, n_rows=n_rows, starts=starts)
        ext_a, ext_v, ext_d = _with_halo(am, ap, an), _with_halo(vm, vp, vn), _with_halo(dm, dp, dn)
        above_a, below_a = neighbours(ext_a)
        above_v, below_v = neighbours(ext_v)
        ca = above_a * wa[0:1, :] + ext_a * wa[1:2, :] + below_a * wa[2:3, :] + wa[3:4, :]
        cv = above_v * wv[0:1, :] + ext_v * wv[1:2, :] + below_v * wv[2:3, :] + wv[3:4, :]
        sig = jax.nn.sigmoid(ca)
        d_cv = ext_d * (ca * sig)
        d_ca = ext_d * cv * (sig * (1.0 + ca * (1.0 - sig)))

        def finish(d_c, above, ext, below, w, du_ref, dw_ref):
            d_above, d_below = neighbours(d_c)
            du_ref[...] = (w[1:2, :] * d_c + w[0:1, :] * d_below + w[2:3, :] * d_above)[main]
            d_main = d_c[main]
            sums = jnp.concatenate([jnp.sum(above[main] * d_main, axis=0, keepdims=True),
                                    jnp.sum(ext[main] * d_main, axis=0, keepdims=True),
                                    jnp.sum(below[main] * d_main, axis=0, keepdims=True),
                                    jnp.sum(d_main, axis=0, keepdims=True), jnp.zeros((4, tc), F32)], axis=0)

            @pl.when(i == 0)
            def _():
                dw_ref[...] = sums

            @pl.when(i > 0)
            def _():
                dw_ref[...] += sums

        finish(d_ca, above_a, ext_a, below_a, wa, dua_ref, dwa_ref)
        finish(d_cv, above_v, ext_v, below_v, wv, duv_ref, dwv_ref)

    block = pl.BlockSpec((rb, tc), lambda j, i: (i, j))
    taps_out = pl.BlockSpec((8, tc), lambda j, i: (0, j))
    return pl.pallas_call(
        body, name="ffn_mid_bwd", grid=(nj, n_rows // rb),
        in_specs=specs + specs + [pl.BlockSpec((8, tc), lambda j, i: (0, j)),
                                  pl.BlockSpec((8, tc), lambda j, i: (0, nj + j))] + specs,
        out_specs=[block, block, taps_out, taps_out],
        out_shape=[jax.ShapeDtypeStruct((n_rows, half), F32)] * 2 + [jax.ShapeDtypeStruct((8, half), F32)] * 2,
        compiler_params=_params(("parallel", "arbitrary")),
    )(u_a, u_a, u_a, u_v, u_v, u_v, taps, taps, d_act, d_act, d_act)


def _taps(conv_w, conv_b):
    return jnp.concatenate([conv_w, conv_b[None, :], jnp.zeros((4, conv_w.shape[1]), F32)], axis=0)


@functools.partial(jax.custom_vjp, nondiff_argnums=(4,))
def ffn_mid(u_a, u_v, conv_w, conv_b, starts):
    return _ffn_mid_fwd_call(u_a, u_v, _taps(conv_w, conv_b), starts)


def _ffn_mid_fwd(u_a, u_v, conv_w, conv_b, starts):
    return _ffn_mid_fwd_call(u_a, u_v, _taps(conv_w, conv_b), starts), (u_a, u_v, conv_w, conv_b)


def _ffn_mid_bwd(starts, res, d_act):
    u_a, u_v, conv_w, conv_b = res
    du_a, du_v, dw_a, dw_v = _ffn_mid_bwd_call(u_a, u_v, _taps(conv_w, conv_b), d_act, starts)
    d_taps = jnp.concatenate([dw_a, dw_v], axis=1)
    return du_a, du_v, d_taps[0:3], d_taps[3]


ffn_mid.defvjp(_ffn_mid_fwd, _ffn_mid_bwd)


def _conv_ffn(h, lp, starts):
    u_a = linear(h, lp["ffn_up"], lp["ffn_up_slot"])
    u_v = linear(h, lp["ffn_up_v"], lp["ffn_up_v_slot"])
    act = ffn_mid(u_a, u_v, lp["ffn_conv_w"], lp["ffn_conv_b"], tuple(starts))
    return linear(act, lp["ffn_down"], lp["ffn_down_slot"])


def _rows(ctx_val, lat_val, ctx_len, n_lat, with_ctx):
    lat = jnp.broadcast_to(lat_val, (n_lat, lat_val.shape[-1]))
    if not with_ctx:
        return lat
    return jnp.concatenate([jnp.broadcast_to(ctx_val, (ctx_len, ctx_val.shape[-1])), lat], axis=0)


def _local_loss(diff, fixed):
    p = {**fixed, **diff}
    x, ctx = p["x"][0], p["ctx"][0]
    n_lat, ctx_len = x.shape[0], ctx.shape[0]
    depth = p["ada_b"].shape[0]
    rope = _rope_tables(n_lat)
    rows = jnp.concatenate([ctx, x], axis=0)
    cond = jnp.zeros((16, D_MODEL), F32).at[0].set(jax.nn.silu(p["c"][0])).at[1].set(jax.nn.silu(p["c_ctx"]))
    layer_names = [n for n in WEIGHTS if n != "c_ctx"]
    for i in range(depth):
        last = i == depth - 1
        lp = {n: p[n][i] for n in layer_names}
        lp["ffn_up_v"] = p["ffn_up_v"][i]
        lp.update({n + "_slot": p[n + "_slot"][i] for n in MATMUL_WEIGHTS})
        mod = linear(cond, lp["ada_w"], lp["ada_w_slot"]) + lp["ada_b"]
        m_lat, m_ctx = jnp.split(mod[0:1], 6, axis=-1), jnp.split(mod[1:2], 6, axis=-1)
        both = functools.partial(_rows, ctx_len=ctx_len, n_lat=n_lat, with_ctx=True)

        h = rms_norm(rows, lp["norm_mix_pre"]) * (1.0 + both(m_ctx[1], m_lat[1])) + both(m_ctx[0], m_lat[0])
        y = _token_mixer(h, lp, rope, ctx_len, not last)
        if last:
            rows = rows[ctx_len:]
        cur = functools.partial(_rows, ctx_len=ctx_len, n_lat=n_lat, with_ctx=not last)
        starts = [0] if last else [0, ctx_len]
        rows = rows + cur(m_ctx[2], m_lat[2]) * rms_norm(y, lp["norm_mix_post"])
        h = rms_norm(rows, lp["norm_ffn_pre"]) * (1.0 + cur(m_ctx[4], m_lat[4])) + cur(m_ctx[3], m_lat[3])
        rows = rows + cur(m_ctx[5], m_lat[5]) * rms_norm(_conv_ffn(h, lp, starts), lp["norm_ffn_post"])
    err = jnp.square(rows - p["loss_target"][0])
    return 0.5 * jnp.sum(jnp.mean(err, axis=-1))


def kernel(x, c, ctx, c_ctx, ada_w, ada_b, norm_mix_pre, norm_mix_post, norm_ffn_pre, norm_ffn_post, w_in, q_norm, k_norm, gla_gate_w, gla_gate_b, gla_out_norm, s5_a_re, s5_a_im, s5_log_dt, s5_b_re, s5_b_im, s5_c_re, s5_c_im, s5_d, s5_glu_w, s5_glu_b, w_br_att, w_br_gla, w_br_s5, w_out, ffn_up, ffn_conv_w, ffn_conv_b, ffn_down, loss_target, m_c_ctx, m_ada_w, m_ada_b, m_norm_mix_pre, m_norm_mix_post, m_norm_ffn_pre, m_norm_ffn_post, m_w_in, m_q_norm, m_k_norm, m_gla_gate_w, m_gla_gate_b, m_gla_out_norm, m_s5_a_re, m_s5_a_im, m_s5_log_dt, m_s5_b_re, m_s5_b_im, m_s5_c_re, m_s5_c_im, m_s5_d, m_s5_glu_w, m_s5_glu_b, m_w_br_att, m_w_br_gla, m_w_br_s5, m_w_out, m_ffn_up, m_ffn_conv_w, m_ffn_conv_b, m_ffn_down, v_c_ctx, v_ada_w, v_ada_b, v_norm_mix_pre, v_norm_mix_post, v_norm_ffn_pre, v_norm_ffn_post, v_w_in, v_q_norm, v_k_norm, v_gla_gate_w, v_gla_gate_b, v_gla_out_norm, v_s5_a_re, v_s5_a_im, v_s5_log_dt, v_s5_b_re, v_s5_b_im, v_s5_c_re, v_s5_c_im, v_s5_d, v_s5_glu_w, v_s5_glu_b, v_w_br_att, v_w_br_gla, v_w_br_s5, v_w_out, v_ffn_up, v_ffn_conv_w, v_ffn_conv_b, v_ffn_down):
    args = (x, c, ctx, c_ctx, ada_w, ada_b, norm_mix_pre, norm_mix_post, norm_ffn_pre, norm_ffn_post, w_in, q_norm, k_norm, gla_gate_w, gla_gate_b, gla_out_norm, s5_a_re, s5_a_im, s5_log_dt, s5_b_re, s5_b_im, s5_c_re, s5_c_im, s5_d, s5_glu_w, s5_glu_b, w_br_att, w_br_gla, w_br_s5, w_out, ffn_up, ffn_conv_w, ffn_conv_b, ffn_down)
    given = dict(zip(FWD_INPUTS, args))
    given["loss_target"] = loss_target
    m_in = dict(zip(WEIGHTS, (m_c_ctx, m_ada_w, m_ada_b, m_norm_mix_pre, m_norm_mix_post, m_norm_ffn_pre, m_norm_ffn_post, m_w_in, m_q_norm, m_k_norm, m_gla_gate_w, m_gla_gate_b, m_gla_out_norm, m_s5_a_re, m_s5_a_im, m_s5_log_dt, m_s5_b_re, m_s5_b_im, m_s5_c_re, m_s5_c_im, m_s5_d, m_s5_glu_w, m_s5_glu_b, m_w_br_att, m_w_br_gla, m_w_br_s5, m_w_out, m_ffn_up, m_ffn_conv_w, m_ffn_conv_b, m_ffn_down)))
    v_in = dict(zip(WEIGHTS, (v_c_ctx, v_ada_w, v_ada_b, v_norm_mix_pre, v_norm_mix_post, v_norm_ffn_pre, v_norm_ffn_post, v_w_in, v_q_norm, v_k_norm, v_gla_gate_w, v_gla_gate_b, v_gla_out_norm, v_s5_a_re, v_s5_a_im, v_s5_log_dt, v_s5_b_re, v_s5_b_im, v_s5_c_re, v_s5_c_im, v_s5_d, v_s5_glu_w, v_s5_glu_b, v_w_br_att, v_w_br_gla, v_w_br_s5, v_w_out, v_ffn_up, v_ffn_conv_w, v_ffn_conv_b, v_ffn_down)))
    depth = ada_b.shape[0]
    big_names, small_names = list(BIG), list(SMALL_SHARDED)
    sharded_names = big_names + small_names

    full = {}
    groups = {}
    for n in big_names:
        groups.setdefault(_round_up(given[n].shape[2], 128), []).append(n)
    per_layer = {n: [] for n in big_names}
    for i in range(depth):
        for width, names in groups.items():
            gathered = all_gather_blocks(_pack_rows([given[n][i] for n in names], width, BF16),
                                         f"gather_weights_{width}")
            for n, blocks in zip(names, _unpack_rows(gathered, [given[n].shape[1:] for n in names])):
                per_layer[n].append(_to_full(blocks, BIG[n]))
    for n in big_names:
        full[n] = jnp.stack(per_layer[n])
    small_shapes = [given[n].shape for n in small_names]
    gathered = all_gather_blocks(_pack([given[n] for n in small_names], F32), "gather_small")
    for n, blocks in zip(small_names, _unpack(gathered, small_shapes)):
        full[n] = _to_full(blocks, SMALL_SHARDED[n] + 1)

    w_in_full = jnp.pad(_w_in_reorder(full["w_in"], True), ((0, 0), (0, 0), (0, D_IN_PAD - D_IN)))

    diff = {"x": x}
    diff.update({n: given[n] for n in REPLICATED})
    diff.update({n: full[n] for n in small_names})
    fixed = {"c": c, "ctx": ctx, "loss_target": loss_target}
    d_ff = full["ffn_up"].shape[2] // 2
    for n in big_names:
        fixed[n] = w_in_full if n == "w_in" else full[n]
    fixed["ffn_up"], fixed["ffn_up_v"] = full["ffn_up"][:, :, :d_ff], full["ffn_up"][:, :, d_ff:]
    for n in MATMUL_WEIGHTS:
        diff[n + "_slot"] = jnp.zeros(fixed[n].shape, F32)
    loss_local, grads = jax.value_and_grad(_local_loss)(diff, fixed)
    loss = lax.psum(loss_local, ("x", "y", "c"))

    g_full = {n: grads[n] for n in small_names}
    for n in big_names:
        g_full[n] = grads[n + "_slot"]
    g_full["ffn_up"] = jnp.concatenate([grads["ffn_up_slot"], grads["ffn_up_v_slot"]], axis=2)
    g_full["w_in"] = _w_in_reorder(g_full["w_in"], False)

    out_g, out_d, out_m, out_v = {}, {}, {}, {}
    layer_out = {n: ([], [], [], []) for n in big_names}
    for i in range(depth):
        for width, names in groups.items():
            send = _pack_rows([_to_blocks(g_full[n][i], BIG[n]) for n in names], width, BF16)
            parts = exchange_blocks(send, f"exchange_grads_{width}")
            results = sum_adamw(parts, _pack_rows([given[n][i] for n in names], width, F32),
                                _pack_rows([m_in[n][i] for n in names], width, F32),
                                _pack_rows([v_in[n][i] for n in names], width, F32), f"adamw_{width}")
            for k, res in enumerate(results):
                for n, arr in zip(names, _unpack_rows(res, [given[n].shape[1:] for n in names])):
                    layer_out[n][k].append(arr)
    for n in big_names:
        out_g[n], out_d[n], out_m[n], out_v[n] = (jnp.stack(parts_k) for parts_k in layer_out[n])
    send = _pack_blocks([_to_blocks(g_full[n], SMALL_SHARDED[n] + 1) for n in small_names], F32)
    parts = exchange_blocks(send, "exchange_small")
    results = sum_adamw(parts, _pack([given[n] for n in small_names], F32), _pack([m_in[n] for n in small_names], F32),
                        _pack([v_in[n] for n in small_names], F32), "adamw_small")
    for store, res in zip((out_g, out_d, out_m, out_v), results):
        for n, arr in zip(small_names, _unpack(res, small_shapes)):
            store[n] = arr

    rep_shapes = [given[n].shape for n in REPLICATED]
    parts = all_gather_blocks(_pack([grads[n] for n in REPLICATED], F32), "gather_rep_grads")
    results = sum_adamw(parts, _pack([given[n] for n in REPLICATED], F32), _pack([m_in[n] for n in REPLICATED], F32),
                        _pack([v_in[n] for n in REPLICATED], F32), "adamw_replicated")
    for store, res in zip((out_g, out_d, out_m, out_v), results):
        for n, arr in zip(REPLICATED, _unpack(res, rep_shapes)):
            store[n] = arr

    return (loss, grads["x"], *[out_g[n] for n in WEIGHTS], *[out_d[n] for n in WEIGHTS],
            *[out_m[n] for n in WEIGHTS], *[out_v[n] for n in WEIGHTS])
```

```python
import functools
import math

import numpy as np
import jax
import jax.numpy as jnp
from jax import lax
from jax.experimental import pallas as pl
from jax.experimental.pallas import tpu as pltpu

F32 = jnp.float32
BF16 = jnp.bfloat16
MESH = pl.DeviceIdType.MESH
N_DEV = 8

D_MODEL = 1024
GRID_W = 64
ATT_HEADS, ATT_KV_HEADS, ATT_HEAD_DIM = 4, 2, 64
ATT_WIDTH, ATT_KV_WIDTH = 256, 128
ROPE_THETA = 10000.0
GLA_HEADS, GLA_DK, GLA_DV = 4, 64, 128
GLA_K_WIDTH, GLA_V_WIDTH = 256, 512
GLA_GATE_RANK, GLA_GATE_NORM, GLA_CHUNK = 16, 16.0, 64
S5_GROUPS, S5_GROUP_CH, S5_WIDTH, S5_STATE = 16, 16, 256, 64
S5_FLAT = S5_GROUPS * S5_STATE
N_BRANCH = 3
EPS = 1e-6
IN_SPLITS = (ATT_WIDTH, ATT_KV_WIDTH, ATT_KV_WIDTH, GLA_K_WIDTH, GLA_K_WIDTH, GLA_V_WIDTH, GLA_V_WIDTH,
             GLA_GATE_RANK, GLA_GATE_RANK, S5_WIDTH, N_BRANCH * D_MODEL)
D_IN = sum(IN_SPLITS)
D_IN_PAD = 5632

ADAM_LR, ADAM_B1, ADAM_B2, ADAM_EPS, ADAM_WD, ADAM_STEP = 0.001, 0.9, 0.999, 1e-08, 0.01, 10

VMEM_LIMIT = 56 * 1024 * 1024
MM_VMEM_BUDGET = 40 * 1024 * 1024
PACK_W = 512
PACK_UNIT = 16 * PACK_W

WEIGHTS = ['c_ctx', 'ada_w', 'ada_b', 'norm_mix_pre', 'norm_mix_post', 'norm_ffn_pre', 'norm_ffn_post', 'w_in',
           'q_norm', 'k_norm', 'gla_gate_w', 'gla_gate_b', 'gla_out_norm', 's5_a_re', 's5_a_im', 's5_log_dt',
           's5_b_re', 's5_b_im', 's5_c_re', 's5_c_im', 's5_d', 's5_glu_w', 's5_glu_b', 'w_br_att', 'w_br_gla',
           'w_br_s5', 'w_out', 'ffn_up', 'ffn_conv_w', 'ffn_conv_b', 'ffn_down']
FWD_INPUTS = ['x', 'c', 'ctx'] + WEIGHTS
BIG = {'ada_w': 1, 'w_in': 1, 'w_br_att': 1, 'w_br_gla': 1, 'w_br_s5': 1, 'w_out': 0, 'ffn_up': 1, 'ffn_down': 0}
SMALL_SHARDED = {'gla_gate_w': 2, 'gla_gate_b': 1, 's5_glu_w': 0, 'ffn_conv_w': 1}
SHARDED = {**BIG, **SMALL_SHARDED}
MATMUL_WEIGHTS = list(BIG) + ['ffn_up_v']
REPLICATED = [n for n in WEIGHTS if n not in SHARDED]


def _pick(n, cands):
    for cand in cands:
        if n % cand == 0:
            return cand
    return n


def _params(sem):
    return pltpu.CompilerParams(dimension_semantics=sem, vmem_limit_bytes=VMEM_LIMIT)


_DIMS = {"nn": ((1,), (0,)), "nt": ((1,), (1,)), "tn": ((0,), (0,))}


def _mm(a, b, mode, name):
    if mode == "tn":
        K, M = a.shape
    else:
        M, K = a.shape
    N = b.shape[0] if mode == "nt" else b.shape[1]
    tm = _pick(M, (1408, 1024, 768, 512, 256, 128))
    tn = _pick(N, (1408, 1024, 512, 256, 128))
    tk_options = [K] + [t for t in (2816, 1408, 1024, 768, 512, 256, 128) if t < K and K % t == 0]
    for tk in tk_options:
        blocks = tm * tk * a.dtype.itemsize + tk * tn * b.dtype.itemsize + tm * tn * 4
        if 2 * blocks <= MM_VMEM_BUDGET:
            break
    nk = K // tk
    dims = (_DIMS[mode], ((), ()))

    def body(a_ref, b_ref, o_ref):
        acc = lax.dot_general(a_ref[...].astype(BF16), b_ref[...].astype(BF16), dims, preferred_element_type=F32)
        if nk == 1:
            o_ref[...] = acc
        else:
            k = pl.program_id(2)

            @pl.when(k == 0)
            def _():
                o_ref[...] = acc

            @pl.when(k > 0)
            def _():
                o_ref[...] += acc

    a_spec = (pl.BlockSpec((tk, tm), lambda i, j, k: (k, i)) if mode == "tn"
              else pl.BlockSpec((tm, tk), lambda i, j, k: (i, k)))
    b_spec = (pl.BlockSpec((tn, tk), lambda i, j, k: (j, k)) if mode == "nt"
              else pl.BlockSpec((tk, tn), lambda i, j, k: (k, j)))
    return pl.pallas_call(
        body, name=name, grid=(M // tm, N // tn, nk),
        in_specs=[a_spec, b_spec], out_specs=pl.BlockSpec((tm, tn), lambda i, j, k: (i, j)),
        out_shape=jax.ShapeDtypeStruct((M, N), F32),
        compiler_params=_params(("parallel", "parallel", "arbitrary")),
    )(a, b)


FROM_XLA = (False, False)


def _rounded(x, from_kernel):
    return x if from_kernel else x.astype(BF16)


@functools.partial(jax.custom_vjp, nondiff_argnums=(2,))
def matmul(a, w, from_kernel=FROM_XLA):
    return _mm(_rounded(a, from_kernel[0]), w.astype(BF16), "nn", "mm_fwd")


def _matmul_fwd(a, w, from_kernel):
    ab, wb = _rounded(a, from_kernel[0]), w.astype(BF16)
    return _mm(ab, wb, "nn", "mm_fwd"), (ab, wb)


def _matmul_bwd(from_kernel, res, dy):
    ab, wb = res
    dyb = _rounded(dy, from_kernel[1])
    return _mm(dyb, wb, "nt", "mm_dx"), _mm(ab, dyb, "tn", "mm_dw")


matmul.defvjp(_matmul_fwd, _matmul_bwd)


@functools.partial(jax.custom_vjp, nondiff_argnums=(3,))
def linear(a, w, w_grad_slot, from_kernel=FROM_XLA):
    del w_grad_slot
    return _mm(_rounded(a, from_kernel[0]), w, "nn", "lin_fwd")


def _linear_fwd(a, w, w_grad_slot, from_kernel):
    del w_grad_slot
    ab = _rounded(a, from_kernel[0])
    return _mm(ab, w, "nn", "lin_fwd"), (ab, w)


def _linear_bwd(from_kernel, res, dy):
    ab, w = res
    dyb = _rounded(dy, from_kernel[1])
    return _mm(dyb, w, "nt", "lin_dx"), jnp.zeros_like(w), _mm(ab, dyb, "tn", "lin_dw")


linear.defvjp(_linear_fwd, _linear_bwd)


def _attn_fwd_call(q, k, v):
    H, Tq, d = q.shape
    KV, Tk, _ = k.shape
    G = H // KV
    tq = _pick(Tq, (256, 128, 64))

    def body(q_ref, k_ref, v_ref, o_ref, lse_ref):
        s = lax.dot_general(q_ref[0], k_ref[0], (_DIMS["nt"], ((), ())), preferred_element_type=F32)
        m = jnp.max(s, axis=1, keepdims=True)
        p = jnp.exp(s - m)
        l = jnp.sum(p, axis=1, keepdims=True)
        o_ref[0] = jnp.dot(p.astype(BF16), v_ref[0], preferred_element_type=F32) * (1.0 / l)
        lse_ref[0] = m + jnp.log(l)

    return pl.pallas_call(
        body, name="attn_fwd", grid=(H, Tq // tq),
        in_specs=[pl.BlockSpec((1, tq, d), lambda h, i: (h, i, 0)),
                  pl.BlockSpec((1, Tk, d), lambda h, i: (h // G, 0, 0)),
                  pl.BlockSpec((1, Tk, d), lambda h, i: (h // G, 0, 0))],
        out_specs=[pl.BlockSpec((1, tq, d), lambda h, i: (h, i, 0)),
                   pl.BlockSpec((1, tq, 1), lambda h, i: (h, i, 0))],
        out_shape=[jax.ShapeDtypeStruct((H, Tq, d), F32), jax.ShapeDtypeStruct((H, Tq, 1), F32)],
        compiler_params=_params(("parallel", "parallel")),
    )(q, k, v)


def _attn_bwd_call(q, k, v, o, lse, do, scale):
    H, Tq, d = q.shape
    KV, Tk, _ = k.shape
    G = H // KV
    tq = _pick(Tq, (256, 128, 64))
    ck = _pick(Tk, (1408, 1024, 512, 256, 128, 64))
    nck = Tk // ck

    def body(q_ref, k_ref, v_ref, o_ref, lse_ref, do_ref, dq_ref, dk_ref, dv_ref):
        @pl.when((pl.program_id(1) == 0) & (pl.program_id(2) == 0))
        def _():
            dk_ref[...] = jnp.zeros_like(dk_ref)
            dv_ref[...] = jnp.zeros_like(dv_ref)

        qb = q_ref[0]
        do = do_ref[0]
        dob = do.astype(BF16)
        delta = jnp.sum(do * o_ref[0], axis=1, keepdims=True)
        lse = lse_ref[0]
        dq = jnp.zeros((tq, d), F32)
        for cidx in range(nck):
            rows = slice(cidx * ck, (cidx + 1) * ck)
            ks = k_ref[0, rows, :]
            vs = v_ref[0, rows, :]
            s = lax.dot_general(qb, ks, (_DIMS["nt"], ((), ())), preferred_element_type=F32)
            p = jnp.exp(s - lse)
            dv_ref[0, rows, :] += lax.dot_general(p.astype(BF16), dob, (_DIMS["tn"], ((), ())),
                                                  preferred_element_type=F32)
            dp = lax.dot_general(dob, vs, (_DIMS["nt"], ((), ())), preferred_element_type=F32)
            dsb = (p * (dp - delta)).astype(BF16)
            dq = dq + jnp.dot(dsb, ks, preferred_element_type=F32)
            dk_ref[0, rows, :] += lax.dot_general(dsb, qb, (_DIMS["tn"], ((), ())), preferred_element_type=F32)
        dq_ref[0] = dq * scale

    q_spec = pl.BlockSpec((1, tq, d), lambda kv, g, i: (kv * G + g, i, 0))
    kv_spec = pl.BlockSpec((1, Tk, d), lambda kv, g, i: (kv, 0, 0))
    return pl.pallas_call(
        body, name="attn_bwd", grid=(KV, G, Tq // tq),
        in_specs=[q_spec, kv_spec, kv_spec, q_spec,
                  pl.BlockSpec((1, tq, 1), lambda kv, g, i: (kv * G + g, i, 0)), q_spec],
        out_specs=[q_spec, kv_spec, kv_spec],
        out_shape=[jax.ShapeDtypeStruct((H, Tq, d), F32), jax.ShapeDtypeStruct((KV, Tk, d), F32),
                   jax.ShapeDtypeStruct((KV, Tk, d), F32)],
        compiler_params=_params(("arbitrary", "arbitrary", "arbitrary")),
    )(q, k, v, o, lse, do)


ATT_SCALE = ATT_HEAD_DIM ** -0.5


@jax.custom_vjp
def attention(q, k, v):
    return _attn_fwd_call((q * ATT_SCALE).astype(BF16), k.astype(BF16), v.astype(BF16))[0]


def _attention_fwd(q, k, v):
    qb, kb, vb = (q * ATT_SCALE).astype(BF16), k.astype(BF16), v.astype(BF16)
    o, lse = _attn_fwd_call(qb, kb, vb)
    return o, (qb, kb, vb, o, lse)


def _attention_bwd(res, do):
    qb, kb, vb, o, lse = res
    return _attn_bwd_call(qb, kb, vb, o, lse, do, ATT_SCALE)


attention.defvjp(_attention_fwd, _attention_bwd)


_ORDER_DOWN = {0: False, 1: True, 2: True, 3: False}
_ORDER_ADJOINT = {0: 2, 1: 3}


def _scan_tables(a_re, a_im, down):
    pw_re, pw_im = [a_re], [a_im]
    for _ in range(7):
        pw_re, pw_im = (pw_re + [pw_re[-1] * a_re - pw_im[-1] * a_im],
                        pw_im + [pw_re[-1] * a_im + pw_im[-1] * a_re])
    carry_rows = list(range(7, -1, -1)) if down else list(range(8))
    rows_re = [pw_re[r] for r in carry_rows] + [pw_re[0], pw_re[1], pw_re[3]]
    rows_im = [pw_im[r] for r in carry_rows] + [pw_im[0], pw_im[1], pw_im[3]]
    tab = jnp.concatenate([jnp.stack(rows_re), jnp.stack(rows_im)], axis=1)
    return jnp.concatenate([tab, jnp.zeros((5, 2 * S5_FLAT), F32)], axis=0)


def _scan_call(bu, a_re, a_im, order, ctx_len):
    T, W2 = bu.shape
    P = W2 // 2
    rb = _pick(math.gcd(ctx_len, T - ctx_len), (256, 128, 64, 32, 16, 8))
    nblk, cb = T // rb, ctx_len // rb
    ntile = rb // 8
    down = _ORDER_DOWN[order]
    tab = _scan_tables(a_re, a_im, down)

    def blk(n):
        return _block_in_order(n, nblk, cb, order)

    def body(bu_ref, tab_ref, s_ref, carry_ref):
        @pl.when(pl.program_id(0) == 0)
        def _():
            carry_ref[...] = jnp.zeros_like(carry_ref)

        row = lax.broadcasted_iota(jnp.int32, (8, P), 0)
        cp_re, cp_im = tab_ref[0:8, 0:P], tab_ref[0:8, P:2 * P]
        steps = []
        for j, sh in enumerate((1, 2, 4)):
            keep = (row < 8 - sh) if down else (row >= sh)
            steps.append((8 - sh if down else sh, keep, tab_ref[8 + j:9 + j, 0:P], tab_ref[8 + j:9 + j, P:2 * P]))

        def tile(j, carry):
            c_re, c_im = carry
            t = (ntile - 1 - j) if down else j
            r0 = pl.multiple_of(t * 8, 8)
            x_re = bu_ref[pl.ds(r0, 8), 0:P]
            x_im = bu_ref[pl.ds(r0, 8), P:2 * P]
            for shift, keep, p_re, p_im in steps:
                y_re = jnp.where(keep, pltpu.roll(x_re, shift, 0), 0.0)
                y_im = jnp.where(keep, pltpu.roll(x_im, shift, 0), 0.0)
                x_re, x_im = x_re + p_re * y_re - p_im * y_im, x_im + p_re * y_im + p_im * y_re
            x_re, x_im = x_re + cp_re * c_re - cp_im * c_im, x_im + cp_re * c_im + cp_im * c_re
            s_ref[pl.ds(r0, 8), 0:P] = x_re
            s_ref[pl.ds(r0, 8), P:2 * P] = x_im
            last = 0 if down else 7
            return x_re[last:last + 1, :], x_im[last:last + 1, :]

        c_re, c_im = lax.fori_loop(0, ntile, tile, (carry_ref[0:1, 0:P], carry_ref[0:1, P:2 * P]))
        carry_ref[0:1, 0:P] = c_re
        carry_ref[0:1, P:2 * P] = c_im

    return pl.pallas_call(
        body, name=f"s5_scan_{order}", grid=(nblk,),
        in_specs=[pl.BlockSpec((rb, W2), lambda n: (blk(n), 0)), pl.BlockSpec((16, W2), lambda n: (0, 0))],
        out_specs=pl.BlockSpec((rb, W2), lambda n: (blk(n), 0)),
        out_shape=jax.ShapeDtypeStruct((T, W2), F32),
        scratch_shapes=[pltpu.VMEM((8, W2), F32)],
        compiler_params=_params(("arbitrary",)),
    )(bu, tab)


def _prev_in_order(s, order, ctx_len):
    zero = jnp.zeros_like(s[:1])
    if order == 0:
        return jnp.concatenate([zero, s[:-1]], axis=0)
    return jnp.concatenate([s[1:ctx_len], zero, s[ctx_len + 1:], s[:1]], axis=0)


@functools.partial(jax.custom_vjp, nondiff_argnums=(3, 4))
def s5_scan(bu, a_re, a_im, order, ctx_len):
    return _scan_call(bu, a_re, a_im, order, ctx_len)


def _s5_scan_fwd(bu, a_re, a_im, order, ctx_len):
    s = _scan_call(bu, a_re, a_im, order, ctx_len)
    return s, (s, a_re, a_im)


def _s5_scan_bwd(order, ctx_len, res, ds):
    s, a_re, a_im = res
    lam = _scan_call(ds, a_re, -a_im, _ORDER_ADJOINT[order], ctx_len)
    P = a_re.shape[0]
    sp = _prev_in_order(s, order, ctx_len)
    l_re, l_im, p_re, p_im = lam[:, :P], lam[:, P:], sp[:, :P], sp[:, P:]
    g_re = jnp.sum(l_re * p_re + l_im * p_im, axis=0)
    g_im = jnp.sum(l_im * p_re - l_re * p_im, axis=0)
    return lam, g_re, g_im


s5_scan.defvjp(_s5_scan_fwd, _s5_scan_bwd)


def _dot(a, b, mode, precision=None):
    return lax.dot_general(a, b, (_DIMS[mode], ((), ())), preferred_element_type=F32, precision=precision)


def _dot_with_mask(a, b, mode, mask_first):
    x = b if mask_first else a
    hi = x.astype(BF16)
    lo = (x - hi.astype(F32)).astype(BF16)
    if mask_first:
        mask = a.astype(BF16)
        return _dot(mask, hi, mode) + _dot(mask, lo, mode)
    mask = b.astype(BF16)
    return _dot(hi, mask, mode) + _dot(lo, mask, mode)


def _block_in_order(n, nblk, cblk, order):
    if order == 0:
        return n
    if order == 1:
        return jnp.where(n < cblk, cblk - 1 - n, nblk - 1 - (n - cblk))
    if order == 2:
        return nblk - 1 - n
    return jnp.where(n < nblk - cblk, cblk + n, n - (nblk - cblk))


def _gla_chunk_terms(qn, kn, gn, tri, reverse):
    b = _dot_with_mask(tri, gn, "tn" if reverse else "nn", True)
    edge = 0 if reverse else GLA_CHUNK - 1
    b_end = b[edge:edge + 1, :]
    e_pos, e_neg, e_end = jnp.exp(b), jnp.exp(-b), jnp.exp(b_end - b)
    return b_end, e_pos, e_neg, e_end, qn * e_pos, kn * e_neg, kn * e_end


def _gla_masks():
    L = GLA_CHUNK
    rows, cols = lax.broadcasted_iota(jnp.int32, (L, L), 0), lax.broadcasted_iota(jnp.int32, (L, L), 1)
    return rows >= cols, rows <= cols


def _gla_blocking(n_t, ctx_len):
    n_chunks, ctx_chunks = n_t // GLA_CHUNK, ctx_len // GLA_CHUNK
    per_block = _pick(math.gcd(ctx_chunks, n_chunks - ctx_chunks), (4, 2, 1))
    return n_chunks, per_block, GLA_CHUNK * per_block, n_chunks // per_block, ctx_chunks // per_block


def _gla_fwd_call(q, k, v, g, reverse, ctx_len):
    H, T, dk = q.shape
    dv = v.shape[-1]
    n_chunks, cb, rb, nb, cblk = _gla_blocking(T, ctx_len)
    L = GLA_CHUNK
    order = 1 if reverse else 0

    def body(q_ref, k_ref, v_ref, g_ref, o_ref, sb_ref, s_ref):
        @pl.when(pl.program_id(0) == 0)
        def _():
            s_ref[...] = jnp.zeros_like(s_ref)

        lower, upper = _gla_masks()
        tri = lower.astype(F32)
        seen = upper if reverse else lower
        ones = jnp.ones((L, dv), F32)
        states = [s_ref[h] for h in range(H)]
        for n in (reversed(range(cb)) if reverse else range(cb)):
            rows = slice(n * L, (n + 1) * L)
            for h in range(H):
                qn, kn, vn, gn = q_ref[h, rows, :], k_ref[h, rows, :], v_ref[h, rows, :], g_ref[h, rows, :]
                _, _, _, _, q_in, k_in, k_end = _gla_chunk_terms(qn, kn, gn, tri, reverse)
                att = jnp.where(seen, _dot(q_in.astype(BF16), k_in.astype(BF16), "nt"), 0.0)
                vb = vn.astype(BF16)
                sb_ref[h, n] = states[h]
                o_ref[h, rows, :] = (_dot(att.astype(BF16), vb, "nn")
                                     + _dot(q_in.astype(BF16), states[h].astype(BF16), "nn"))
                decay = jnp.exp(_dot_with_mask(gn, ones, "tn", False))
                states[h] = decay * states[h] + _dot(k_end.astype(BF16), vb, "tn")
        for h in range(H):
            s_ref[h] = states[h]

    def at(i):
        return 0, _block_in_order(i, nb, cblk, order), 0

    row_k, row_v = pl.BlockSpec((H, rb, dk), at), pl.BlockSpec((H, rb, dv), at)
    return pl.pallas_call(
        body, name="gla_fwd", grid=(nb,),
        in_specs=[row_k, row_k, row_v, row_k],
        out_specs=[row_v, pl.BlockSpec((H, cb, dk, dv), lambda i: (*at(i), 0))],
        out_shape=[jax.ShapeDtypeStruct((H, T, dv), F32), jax.ShapeDtypeStruct((H, n_chunks, dk, dv), F32)],
        scratch_shapes=[pltpu.VMEM((H, dk, dv), F32)],
        compiler_params=_params(("arbitrary",)),
    )(q, k, v, g)


def _gla_bwd_call(q, k, v, g, sb, do, reverse, ctx_len):
    H, T, dk = q.shape
    dv = v.shape[-1]
    n_chunks, cb, rb, nb, cblk = _gla_blocking(T, ctx_len)
    L = GLA_CHUNK
    order = 3 if reverse else 2

    def body(q_ref, k_ref, v_ref, g_ref, sb_ref, do_ref, dq_ref, dk_ref, dv_ref, dg_ref, ds_ref):
        @pl.when(pl.program_id(0) == 0)
        def _():
            ds_ref[...] = jnp.zeros_like(ds_ref)

        lower, upper = _gla_masks()
        tri = lower.astype(F32)
        seen = upper if reverse else lower
        ones = jnp.ones((L, dv), F32)
        ones8 = jnp.ones((8, dv), F32)
        d_states = [ds_ref[h] for h in range(H)]
        for n in (range(cb) if reverse else reversed(range(cb))):
            rows = slice(n * L, (n + 1) * L)
            for h in range(H):
                qn, kn, vn, gn = q_ref[h, rows, :], k_ref[h, rows, :], v_ref[h, rows, :], g_ref[h, rows, :]
                state, d_state = sb_ref[h, n], d_states[h]
                b_end, e_pos, e_neg, e_end, q_in, k_in, k_end = _gla_chunk_terms(qn, kn, gn, tri, reverse)
                q_b, k_b, ke_b, vb = q_in.astype(BF16), k_in.astype(BF16), k_end.astype(BF16), vn.astype(BF16)
                dob = do_ref[h, rows, :].astype(BF16)
                dsb = d_state.astype(BF16)
                att = jnp.where(seen, _dot(q_b, k_b, "nt"), 0.0).astype(BF16)
                d_att = jnp.where(seen, _dot(dob, vb, "nt"), 0.0).astype(BF16)
                d_qin = _dot(d_att, k_b, "nn") + _dot(dob, state.astype(BF16), "nt")
                d_kin = _dot(d_att, q_b, "tn")
                d_kend = _dot(vb, dsb, "nt")
                dv_ref[h, rows, :] = _dot(att, dob, "tn") + _dot(ke_b, dsb, "nn")
                through_decay = _dot_with_mask(ones8, state * d_state, "nt", True)[0:1, :]
                d_bend = jnp.sum(d_kend * k_end, axis=0, keepdims=True) + jnp.exp(b_end) * through_decay
                d_b = d_qin * q_in - d_kin * k_in - d_kend * k_end
                dg_ref[h, rows, :] = _dot_with_mask(tri, d_b, "nn" if reverse else "tn", True) + d_bend
                dq_ref[h, rows, :] = d_qin * e_pos
                dk_ref[h, rows, :] = d_kin * e_neg + d_kend * e_end
                decay = jnp.exp(_dot_with_mask(gn, ones, "tn", False))
                d_states[h] = _dot(q_b, dob, "tn") + decay * d_state
        for h in range(H):
            ds_ref[h] = d_states[h]

    def at(i):
        return 0, _block_in_order(i, nb, cblk, order), 0

    row_k, row_v = pl.BlockSpec((H, rb, dk), at), pl.BlockSpec((H, rb, dv), at)
    return pl.pallas_call(
        body, name="gla_bwd", grid=(nb,),
        in_specs=[row_k, row_k, row_v, row_k, pl.BlockSpec((H, cb, dk, dv), lambda i: (*at(i), 0)), row_v],
        out_specs=[row_k, row_k, row_v, row_k],
        out_shape=[jax.ShapeDtypeStruct((H, T, dk), F32), jax.ShapeDtypeStruct((H, T, dk), F32),
                   jax.ShapeDtypeStruct((H, T, dv), F32), jax.ShapeDtypeStruct((H, T, dk), F32)],
        scratch_shapes=[pltpu.VMEM((H, dk, dv), F32)],
        compiler_params=_params(("arbitrary",)),
    )(q, k, v, g, sb, do)


@functools.partial(jax.custom_vjp, nondiff_argnums=(4, 5))
def gla_scan(q, k, v, g, reverse, ctx_len):
    return _gla_fwd_call(q, k, v, g, reverse, ctx_len)[0]


def _gla_scan_fwd(q, k, v, g, reverse, ctx_len):
    o, sb = _gla_fwd_call(q, k, v, g, reverse, ctx_len)
    return o, (q, k, v, g, sb)


def _gla_scan_bwd(reverse, ctx_len, res, do):
    q, k, v, g, sb = res
    return _gla_bwd_call(q, k, v, g, sb, do, reverse, ctx_len)


gla_scan.defvjp(_gla_scan_fwd, _gla_scan_bwd)


@functools.partial(jax.custom_vjp, nondiff_argnums=(1,))
def split_cols(x, sizes):
    points = np.cumsum((0,) + tuple(sizes))
    return tuple(x[:, int(a):int(b)] for a, b in zip(points[:-1], points[1:]))


def _split_cols_fwd(x, sizes):
    return split_cols(x, sizes), x.shape[1]


def _split_cols_bwd(sizes, width, cts):
    parts = list(cts)
    if width > sum(sizes):
        parts.append(jnp.zeros((cts[0].shape[0], width - sum(sizes)), cts[0].dtype))
    return (jnp.concatenate(parts, axis=1),)


split_cols.defvjp(_split_cols_fwd, _split_cols_bwd)


def _position():
    return lax.axis_index("x"), lax.axis_index("y"), lax.axis_index("c")


def all_gather_blocks(shard, name):
    R, W = shard.shape

    def body(x_ref, out_ref, send_sems, recv_sems, local_sem):
        x, y, c = _position()
        me, sibling = (x, y, c), (x, y, 1 - c)
        chips = [(1 - x, y), (x, 1 - y), (1 - x, 1 - y)]

        def slot(px, py, pc):
            return out_ref.at[4 * px + 2 * py + pc]

        def copy(k, block, to, src=None):
            return pltpu.make_async_remote_copy(
                src_ref=slot(*block) if src is None else src, dst_ref=slot(*block),
                send_sem=send_sems.at[k], recv_sem=recv_sems.at[k], device_id=to, device_id_type=MESH)

        mine = pltpu.make_async_copy(x_ref, slot(*me), local_sem)
        mine.start()
        first = [copy(0, me, sibling, src=x_ref)]
        first += [copy(1 + j, me, (*chip, c), src=x_ref) for j, chip in enumerate(chips)]
        for cp in first:
            cp.start()
        passed = [copy(4 + j, (*chip, c), sibling) for j, chip in enumerate(chips)]
        for j, chip in enumerate(chips):
            copy(1 + j, (*chip, c), me).wait_recv()
            passed[j].start()
        copy(0, sibling, me).wait_recv()
        for j, chip in enumerate(chips):
            copy(4 + j, (*chip, 1 - c), me).wait_recv()
        for cp in first + passed:
            cp.wait_send()
        mine.wait()

    return pl.pallas_call(
        body, name=name,
        out_shape=jax.ShapeDtypeStruct((N_DEV, R, W), shard.dtype),
        in_specs=[pl.BlockSpec(memory_space=pltpu.HBM)], out_specs=pl.BlockSpec(memory_space=pltpu.HBM),
        scratch_shapes=[pltpu.SemaphoreType.DMA((7,)), pltpu.SemaphoreType.DMA((7,)), pltpu.SemaphoreType.DMA],
    )(shard)


def exchange_blocks(blocks, name):
    _, R, W = blocks.shape
    flips = [(fx, fy, fc) for fx in (0, 1) for fy in (0, 1) for fc in (0, 1)][1:]

    def body(x_ref, out_ref, send_sems, recv_sems, local_sem):
        x, y, c = _position()
        me = 4 * x + 2 * y + c
        mine = pltpu.make_async_copy(x_ref.at[me], out_ref.at[me], local_sem)
        mine.start()
        copies = []
        for k, (fx, fy, fc) in enumerate(flips):
            px, py, pc = x ^ fx, y ^ fy, c ^ fc
            peer = 4 * px + 2 * py + pc
            copies.append((
                pltpu.make_async_remote_copy(src_ref=x_ref.at[peer], dst_ref=out_ref.at[me],
                                             send_sem=send_sems.at[k], recv_sem=recv_sems.at[k],
                                             device_id=(px, py, pc), device_id_type=MESH),
                pltpu.make_async_remote_copy(src_ref=x_ref.at[peer], dst_ref=out_ref.at[peer],
                                             send_sem=send_sems.at[k], recv_sem=recv_sems.at[k],
                                             device_id=(px, py, pc), device_id_type=MESH)))
        for send, _ in copies:
            send.start()
        for _, recv in copies:
            recv.wait_recv()
        for send, _ in copies:
            send.wait_send()
        mine.wait()

    return pl.pallas_call(
        body, name=name,
        out_shape=jax.ShapeDtypeStruct(blocks.shape, blocks.dtype),
        in_specs=[pl.BlockSpec(memory_space=pltpu.HBM)], out_specs=pl.BlockSpec(memory_space=pltpu.HBM),
        scratch_shapes=[pltpu.SemaphoreType.DMA((7,)), pltpu.SemaphoreType.DMA((7,)), pltpu.SemaphoreType.DMA],
    )(blocks)


def sum_adamw(parts, w, m, v, name):
    _, R, W = parts.shape
    tr = _pick(R, (512, 256, 128, 64, 32, 16, 8))

    def body(p_ref, w_ref, m_ref, v_ref, g_out, d_out, m_out, v_out):
        g = p_ref[0].astype(F32)
        for j in range(1, N_DEV):
            g = g + p_ref[j].astype(F32)
        m_new = ADAM_B1 * m_ref[...] + (1.0 - ADAM_B1) * g
        v_new = ADAM_B2 * v_ref[...] + (1.0 - ADAM_B2) * (g * g)
        m_hat = m_new / (1.0 - ADAM_B1 ** ADAM_STEP)
        v_hat = v_new / (1.0 - ADAM_B2 ** ADAM_STEP)
        g_out[...] = g
        d_out[...] = -ADAM_LR * (m_hat / (jnp.sqrt(v_hat) + ADAM_EPS) + ADAM_WD * w_ref[...])
        m_out[...] = m_new
        v_out[...] = v_new

    row = pl.BlockSpec((tr, W), lambda i: (i, 0))
    return pl.pallas_call(
        body, name=name, grid=(R // tr,),
        in_specs=[pl.BlockSpec((N_DEV, tr, W), lambda i: (0, i, 0)), row, row, row],
        out_specs=[row, row, row, row],
        out_shape=[jax.ShapeDtypeStruct((R, W), F32)] * 4,
        compiler_params=_params(("parallel",)),
    )(parts, w, m, v)


def _padded(n):
    return -(-n // PACK_UNIT) * PACK_UNIT


def _pack(arrays, dtype):
    segs = []
    for arr in arrays:
        flat = arr.reshape(-1).astype(dtype)
        segs.append(jnp.pad(flat, (0, _padded(flat.size) - flat.size)))
    return jnp.concatenate(segs).reshape(-1, PACK_W)


def _pack_blocks(arrays, dtype):
    segs = []
    for arr in arrays:
        flat = arr.reshape(N_DEV, -1).astype(dtype)
        segs.append(jnp.pad(flat, ((0, 0), (0, _padded(flat.shape[1]) - flat.shape[1]))))
    return jnp.concatenate(segs, axis=1).reshape(N_DEV, -1, PACK_W)


def _unpack(buf, shapes):
    lead = buf.shape[:-2]
    flat = buf.reshape(*lead, -1)
    out, off = [], 0
    for shape in shapes:
        n = int(np.prod(shape))
        out.append(flat[..., off:off + n].reshape(*lead, *shape))
        off += _padded(n)
    return out


def _round_up(n, unit):
    return -(-n // unit) * unit


def _pack_rows(arrays, width, dtype):
    parts = []
    for arr in arrays:
        r, c = arr.shape[-2:]
        pad = [(0, 0)] * (arr.ndim - 2) + [(0, _round_up(r, 16) - r), (0, width - c)]
        parts.append(jnp.pad(arr.astype(dtype), pad))
    return jnp.concatenate(parts, axis=-2)


def _unpack_rows(buf, shapes):
    out, off = [], 0
    for r, c in shapes:
        out.append(buf[..., off:off + r, :c])
        off += _round_up(r, 16)
    return out


def _to_full(blocks, axis):
    moved = jnp.moveaxis(blocks, 0, axis)
    shape = list(moved.shape)
    shape[axis:axis + 2] = [shape[axis] * shape[axis + 1]]
    return moved.reshape(shape)


def _to_blocks(full, axis):
    shape = list(full.shape)
    shape[axis:axis + 1] = [N_DEV, shape[axis] // N_DEV]
    return jnp.moveaxis(full.reshape(shape), axis, 0)


def rms_norm(x, gain):
    return x * lax.rsqrt(jnp.mean(x * x, axis=-1, keepdims=True) + EPS) * gain


def _rope_tables(n_tokens):
    rows = n_tokens // GRID_W
    row = jnp.repeat(jnp.arange(rows, dtype=F32), GRID_W)
    col = jnp.tile(jnp.arange(GRID_W, dtype=F32), rows)
    n_freq = ATT_HEAD_DIM // 4
    inv_freq = ROPE_THETA ** (-jnp.arange(n_freq, dtype=F32) / n_freq)
    ang = jnp.stack([row[:, None] * inv_freq, col[:, None] * inv_freq], axis=1)
    return jnp.cos(ang), jnp.sin(ang)


def _rope(x, cos, sin):
    n_t, nh, hd = x.shape
    xr = x.reshape(n_t, nh, 2, 2, hd // 4)
    x1, x2 = xr[..., 0, :], xr[..., 1, :]
    cs, sn = cos[:, None], sin[:, None]
    return jnp.stack([x1 * cs - x2 * sn, x2 * cs + x1 * sn], axis=-2).reshape(n_t, nh, hd)


W_IN_ORDER = (0, 1, 2, 3, 4, 5, 6, 9, 10, 7, 8)
W_IN_SIZES = tuple(IN_SPLITS[s] for s in W_IN_ORDER)


def _w_in_reorder(w, to_kernel_order):
    if to_kernel_order:
        points = np.cumsum((0,) + IN_SPLITS)
        pieces = [w[..., int(points[s]):int(points[s + 1])] for s in W_IN_ORDER]
    else:
        points = np.cumsum((0,) + W_IN_SIZES)
        where = {s: j for j, s in enumerate(W_IN_ORDER)}
        pieces = [w[..., int(points[where[s]]):int(points[where[s] + 1])] for s in range(len(IN_SPLITS))]
    return jnp.concatenate(pieces, axis=-1)


def _s5_discretize(a_re, a_im, log_dt, b_re, b_im):
    dt = jnp.exp(log_dt)[:, None]
    mag = jnp.exp(a_re * dt)
    ab_re, ab_im = mag * jnp.cos(a_im * dt), mag * jnp.sin(a_im * dt)
    den = a_re * a_re + a_im * a_im
    f_re = ((ab_re - 1.0) * a_re + ab_im * a_im) / den
    f_im = (ab_im * a_re - (ab_re - 1.0) * a_im) / den
    bb_re = f_re[..., None] * b_re - f_im[..., None] * b_im
    bb_im = f_re[..., None] * b_im + f_im[..., None] * b_re
    return ab_re, ab_im, bb_re, bb_im


def _s5_direction(u, lp, d, ctx_len):
    ab_re, ab_im, bb_re, bb_im = _s5_discretize(lp["s5_a_re"][d], lp["s5_a_im"][d], lp["s5_log_dt"][d],
                                                lp["s5_b_re"][d], lp["s5_b_im"][d])
    eye = jnp.eye(S5_GROUPS, dtype=F32)
    b_cat = jnp.concatenate([jnp.einsum("gph,gk->ghkp", bb_re, eye).reshape(S5_WIDTH, S5_FLAT),
                             jnp.einsum("gph,gk->ghkp", bb_im, eye).reshape(S5_WIDTH, S5_FLAT)], axis=1)
    c_cat = jnp.concatenate([jnp.einsum("ghp,gk->gpkh", lp["s5_c_re"][d], eye).reshape(S5_FLAT, S5_WIDTH),
                             -jnp.einsum("ghp,gk->gpkh", lp["s5_c_im"][d], eye).reshape(S5_FLAT, S5_WIDTH)], axis=0)
    bu = matmul(u, b_cat, (False, True))
    s = s5_scan(bu, ab_re.reshape(-1), ab_im.reshape(-1), d, ctx_len)
    return matmul(s, c_cat, (True, False))


def _s5_branch(u, lp, ctx_len):
    y = _s5_direction(u, lp, 0, ctx_len) + _s5_direction(u, lp, 1, ctx_len) + lp["s5_d"] * u
    y = jax.nn.gelu(y)
    return y * jax.nn.sigmoid(matmul(y, lp["s5_glu_w"], FROM_XLA) + lp["s5_glu_b"])


def _heads(a, nh):
    return a.reshape(a.shape[0], nh, a.shape[1] // nh).transpose(1, 0, 2)


def _token_mixer(h, lp, rope, ctx_len, with_ctx_out):
    n_t = h.shape[0]
    proj = linear(h, lp["w_in"], lp["w_in_slot"], FROM_XLA)
    aq, ak, av, gq, gk, gv, gr, su, bg, glf, glb = split_cols(proj, W_IN_SIZES)

    aq = rms_norm(aq.reshape(n_t, ATT_HEADS, ATT_HEAD_DIM), lp["q_norm"])
    ak = rms_norm(ak.reshape(n_t, ATT_KV_HEADS, ATT_HEAD_DIM), lp["k_norm"])
    aq = jnp.concatenate([aq[:ctx_len], _rope(aq[ctx_len:], *rope)], axis=0).transpose(1, 0, 2)
    ak = jnp.concatenate([ak[:ctx_len], _rope(ak[ctx_len:], *rope)], axis=0).transpose(1, 0, 2)
    av = av.reshape(n_t, ATT_KV_HEADS, ATT_HEAD_DIM).transpose(1, 0, 2)
    o_att_lat = attention(aq[:, ctx_len:], ak, av)
    if with_ctx_out:
        o_att_ctx = attention(aq[:, :ctx_len], ak[:, :ctx_len], av[:, :ctx_len])
        o_att = jnp.concatenate([o_att_ctx, o_att_lat], axis=1)
    else:
        o_att = o_att_lat
    o_att = o_att.transpose(1, 0, 2).reshape(-1, ATT_WIDTH)

    def log_decay(low, d):
        z = jnp.dot(low, lp["gla_gate_w"][d]) + lp["gla_gate_b"][d]
        return _heads(jax.nn.log_sigmoid(z) / GLA_GATE_NORM, GLA_HEADS)

    q_g, k_g, v_g = _heads(gq, GLA_HEADS) * (GLA_DK ** -0.5), _heads(gk, GLA_HEADS), _heads(gv, GLA_HEADS)
    o_f = gla_scan(q_g, k_g, v_g, log_decay(glf, 0), False, ctx_len)
    o_b = gla_scan(q_g, k_g, v_g, log_decay(glb, 1), True, ctx_len)
    o_gla = rms_norm((o_f + o_b).transpose(1, 0, 2), lp["gla_out_norm"]).reshape(n_t, GLA_V_WIDTH)
    o_gla = o_gla * jax.nn.silu(gr)

    o_s5 = _s5_branch(su, lp, ctx_len)

    if not with_ctx_out:
        o_gla, o_s5, bg = o_gla[ctx_len:], o_s5[ctx_len:], bg[ctx_len:]
    g_att, g_gla, g_s5 = split_cols(jax.nn.sigmoid(bg), (D_MODEL,) * N_BRANCH)
    merged = (g_att * linear(o_att, lp["w_br_att"], lp["w_br_att_slot"], FROM_XLA)
              + g_gla * linear(o_gla, lp["w_br_gla"], lp["w_br_gla_slot"], FROM_XLA)
              + g_s5 * linear(o_s5, lp["w_br_s5"], lp["w_br_s5_slot"], FROM_XLA))
    return linear(merged, lp["w_out"], lp["w_out_slot"], FROM_XLA)


HALO = 8


def _ffn_mid_blocking(n_rows, half):
    return _pick(n_rows, (1056, 1024, 256, 128, 64, 32, 16, 8)), _pick(half, (256, 128))


def _ffn_mid_specs(n_rows, rb, tc):
    per = rb // HALO
    return [pl.BlockSpec((rb, tc), lambda j, i: (i, j)),
            pl.BlockSpec((HALO, tc), lambda j, i: (jnp.maximum(i * per - 1, 0), j)),
            pl.BlockSpec((HALO, tc), lambda j, i: (jnp.minimum((i + 1) * per, n_rows // HALO - 1), j))]


def _with_halo(main_ref, prev_ref, next_ref):
    return jnp.concatenate([prev_ref[...], main_ref[...], next_ref[...]], axis=0)


def _row_neighbours(ext, first_row, n_rows, starts):
    n = ext.shape[0]
    row = lax.broadcasted_iota(jnp.int32, ext.shape, 0) + first_row
    first = functools.reduce(jnp.logical_or, [row == s for s in starts])
    last = functools.reduce(jnp.logical_or, [row == e - 1 for e in tuple(starts[1:]) + (n_rows,)])
    return jnp.where(first, 0.0, pltpu.roll(ext, 1, 0)), jnp.where(last, 0.0, pltpu.roll(ext, n - 1, 0))


def _ffn_mid_fwd_call(u_a, u_v, taps, starts):
    n_rows, half = u_a.shape
    rb, tc = _ffn_mid_blocking(n_rows, half)
    specs = _ffn_mid_specs(n_rows, rb, tc)
    nj = half // tc

    def body(am, ap, an, vm, vp, vn, wa, wv, o_ref):
        first_row = pl.program_id(1) * rb - HALO

        def conv(ext, w):
            above, below = _row_neighbours(ext, first_row, n_rows, starts)
            return above * w[0:1, :] + ext * w[1:2, :] + below * w[2:3, :] + w[3:4, :]

        ca = conv(_with_halo(am, ap, an), wa)[HALO:HALO + rb]
        cv = conv(_with_halo(vm, vp, vn), wv)[HALO:HALO + rb]
        o_ref[...] = ca * jax.nn.sigmoid(ca) * cv

    return pl.pallas_call(
        body, name="ffn_mid_fwd", grid=(nj, n_rows // rb),
        in_specs=specs + specs + [pl.BlockSpec((8, tc), lambda j, i: (0, j)),
                                  pl.BlockSpec((8, tc), lambda j, i: (0, nj + j))],
        out_specs=pl.BlockSpec((rb, tc), lambda j, i: (i, j)),
        out_shape=jax.ShapeDtypeStruct((n_rows, half), F32),
        compiler_params=_params(("parallel", "parallel")),
    )(u_a, u_a, u_a, u_v, u_v, u_v, taps, taps)


def _ffn_mid_bwd_call(u_a, u_v, taps, d_act, starts):
    n_rows, half = u_a.shape
    rb, tc = _ffn_mid_blocking(n_rows, half)
    specs = _ffn_mid_specs(n_rows, rb, tc)
    nj = half // tc
    main = slice(HALO, HALO + rb)

    def body(am, ap, an, vm, vp, vn, wa, wv, dm, dp, dn, dua_ref, duv_ref, dwa_ref, dwv_ref):
        i = pl.program_id(1)
        first_row = i * rb - HALO
        neighbours = functools.partial(_row_neighbours, first_row=first_row, n_rows=n_rows, starts=starts)
        ext_a, ext_v, ext_d = _with_halo(am, ap, an), _with_halo(vm, vp, vn), _with_halo(dm, dp, dn)
        above_a, below_a = neighbours(ext_a)
        above_v, below_v = neighbours(ext_v)
        ca = above_a * wa[0:1, :] + ext_a * wa[1:2, :] + below_a * wa[2:3, :] + wa[3:4, :]
        cv = above_v * wv[0:1, :] + ext_v * wv[1:2, :] + below_v * wv[2:3, :] + wv[3:4, :]
        sig = jax.nn.sigmoid(ca)
        d_cv = ext_d * (ca * sig)
        d_ca = ext_d * cv * (sig * (1.0 + ca * (1.0 - sig)))

        def finish(d_c, above, ext, below, w, du_ref, dw_ref):
            d_above, d_below = neighbours(d_c)
            du_ref[...] = (w[1:2, :] * d_c + w[0:1, :] * d_below + w[2:3, :] * d_above)[main]
            d_main = d_c[main]
            sums = jnp.concatenate([jnp.sum(above[main] * d_main, axis=0, keepdims=True),
                                    jnp.sum(ext[main] * d_main, axis=0, keepdims=True),
                                    jnp.sum(below[main] * d_main, axis=0, keepdims=True),
                                    jnp.sum(d_main, axis=0, keepdims=True), jnp.zeros((4, tc), F32)], axis=0)

            @pl.when(i == 0)
            def _():
                dw_ref[...] = sums

            @pl.when(i > 0)
            def _():
                dw_ref[...] += sums

        finish(d_ca, above_a, ext_a, below_a, wa, dua_ref, dwa_ref)
        finish(d_cv, above_v, ext_v, below_v, wv, duv_ref, dwv_ref)

    block = pl.BlockSpec((rb, tc), lambda j, i: (i, j))
    taps_out = pl.BlockSpec((8, tc), lambda j, i: (0, j))
    return pl.pallas_call(
        body, name="ffn_mid_bwd", grid=(nj, n_rows // rb),
        in_specs=specs + specs + [pl.BlockSpec((8, tc), lambda j, i: (0, j)),
                                  pl.BlockSpec((8, tc), lambda j, i: (0, nj + j))] + specs,
        out_specs=[block, block, taps_out, taps_out],
        out_shape=[jax.ShapeDtypeStruct((n_rows, half), F32)] * 2 + [jax.ShapeDtypeStruct((8, half), F32)] * 2,
        compiler_params=_params(("parallel", "arbitrary")),
    )(u_a, u_a, u_a, u_v, u_v, u_v, taps, taps, d_act, d_act, d_act)


def _taps(conv_w, conv_b):
    return jnp.concatenate([conv_w, conv_b[None, :], jnp.zeros((4, conv_w.shape[1]), F32)], axis=0)


@functools.partial(jax.custom_vjp, nondiff_argnums=(4,))
def ffn_mid(u_a, u_v, conv_w, conv_b, starts):
    return _ffn_mid_fwd_call(u_a, u_v, _taps(conv_w, conv_b), starts)


def _ffn_mid_fwd(u_a, u_v, conv_w, conv_b, starts):
    return _ffn_mid_fwd_call(u_a, u_v, _taps(conv_w, conv_b), starts), (u_a, u_v, conv_w, conv_b)


def _ffn_mid_bwd(starts, res, d_act):
    u_a, u_v, conv_w, conv_b = res
    du_a, du_v, dw_a, dw_v = _ffn_mid_bwd_call(u_a, u_v, _taps(conv_w, conv_b), d_act, starts)
    d_taps = jnp.concatenate([dw_a, dw_v], axis=1)
    return du_a, du_v, d_taps[0:3], d_taps[3]


ffn_mid.defvjp(_ffn_mid_fwd, _ffn_mid_bwd)


def _conv_ffn(h, lp, starts):
    u_a = linear(h, lp["ffn_up"], lp["ffn_up_slot"], (False, True))
    u_v = linear(h, lp["ffn_up_v"], lp["ffn_up_v_slot"], (False, True))
    act = ffn_mid(u_a, u_v, lp["ffn_conv_w"], lp["ffn_conv_b"], tuple(starts))
    return linear(act, lp["ffn_down"], lp["ffn_down_slot"], (True, False))


def _rows(ctx_val, lat_val, ctx_len, n_lat, with_ctx):
    lat = jnp.broadcast_to(lat_val, (n_lat, lat_val.shape[-1]))
    if not with_ctx:
        return lat
    return jnp.concatenate([jnp.broadcast_to(ctx_val, (ctx_len, ctx_val.shape[-1])), lat], axis=0)


def _local_loss(diff, fixed):
    p = {**fixed, **diff}
    x, ctx = p["x"][0], p["ctx"][0]
    n_lat, ctx_len = x.shape[0], ctx.shape[0]
    depth = p["ada_b"].shape[0]
    rope = _rope_tables(n_lat)
    rows = jnp.concatenate([ctx, x], axis=0)
    cond = jnp.zeros((16, D_MODEL), F32).at[0].set(jax.nn.silu(p["c"][0])).at[1].set(jax.nn.silu(p["c_ctx"]))
    layer_names = [n for n in WEIGHTS if n != "c_ctx"]
    for i in range(depth):
        last = i == depth - 1
        lp = {n: p[n][i] for n in layer_names}
        lp["ffn_up_v"] = p["ffn_up_v"][i]
        lp.update({n + "_slot": p[n + "_slot"][i] for n in MATMUL_WEIGHTS})
        mod = linear(cond, lp["ada_w"], lp["ada_w_slot"], FROM_XLA) + lp["ada_b"]
        m_lat, m_ctx = jnp.split(mod[0:1], 6, axis=-1), jnp.split(mod[1:2], 6, axis=-1)
        both = functools.partial(_rows, ctx_len=ctx_len, n_lat=n_lat, with_ctx=True)

        h = rms_norm(rows, lp["norm_mix_pre"]) * (1.0 + both(m_ctx[1], m_lat[1])) + both(m_ctx[0], m_lat[0])
        y = _token_mixer(h, lp, rope, ctx_len, not last)
        if last:
            rows = rows[ctx_len:]
        cur = functools.partial(_rows, ctx_len=ctx_len, n_lat=n_lat, with_ctx=not last)
        starts = [0] if last else [0, ctx_len]
        rows = rows + cur(m_ctx[2], m_lat[2]) * rms_norm(y, lp["norm_mix_post"])
        h = rms_norm(rows, lp["norm_ffn_pre"]) * (1.0 + cur(m_ctx[4], m_lat[4])) + cur(m_ctx[3], m_lat[3])
        rows = rows + cur(m_ctx[5], m_lat[5]) * rms_norm(_conv_ffn(h, lp, starts), lp["norm_ffn_post"])
    err = jnp.square(rows - p["loss_target"][0])
    return 0.5 * jnp.sum(jnp.mean(err, axis=-1))


def kernel(x, c, ctx, c_ctx, ada_w, ada_b, norm_mix_pre, norm_mix_post, norm_ffn_pre, norm_ffn_post, w_in, q_norm, k_norm, gla_gate_w, gla_gate_b, gla_out_norm, s5_a_re, s5_a_im, s5_log_dt, s5_b_re, s5_b_im, s5_c_re, s5_c_im, s5_d, s5_glu_w, s5_glu_b, w_br_att, w_br_gla, w_br_s5, w_out, ffn_up, ffn_conv_w, ffn_conv_b, ffn_down, loss_target, m_c_ctx, m_ada_w, m_ada_b, m_norm_mix_pre, m_norm_mix_post, m_norm_ffn_pre, m_norm_ffn_post, m_w_in, m_q_norm, m_k_norm, m_gla_gate_w, m_gla_gate_b, m_gla_out_norm, m_s5_a_re, m_s5_a_im, m_s5_log_dt, m_s5_b_re, m_s5_b_im, m_s5_c_re, m_s5_c_im, m_s5_d, m_s5_glu_w, m_s5_glu_b, m_w_br_att, m_w_br_gla, m_w_br_s5, m_w_out, m_ffn_up, m_ffn_conv_w, m_ffn_conv_b, m_ffn_down, v_c_ctx, v_ada_w, v_ada_b, v_norm_mix_pre, v_norm_mix_post, v_norm_ffn_pre, v_norm_ffn_post, v_w_in, v_q_norm, v_k_norm, v_gla_gate_w, v_gla_gate_b, v_gla_out_norm, v_s5_a_re, v_s5_a_im, v_s5_log_dt, v_s5_b_re, v_s5_b_im, v_s5_c_re, v_s5_c_im, v_s5_d, v_s5_glu_w, v_s5_glu_b, v_w_br_att, v_w_br_gla, v_w_br_s5, v_w_out, v_ffn_up, v_ffn_conv_w, v_ffn_conv_b, v_ffn_down):
    args = (x, c, ctx, c_ctx, ada_w, ada_b, norm_mix_pre, norm_mix_post, norm_ffn_pre, norm_ffn_post, w_in, q_norm, k_norm, gla_gate_w, gla_gate_b, gla_out_norm, s5_a_re, s5_a_im, s5_log_dt, s5_b_re, s5_b_im, s5_c_re, s5_c_im, s5_d, s5_glu_w, s5_glu_b, w_br_att, w_br_gla, w_br_s5, w_out, ffn_up, ffn_conv_w, ffn_conv_b, ffn_down)
    given = dict(zip(FWD_INPUTS, args))
    given["loss_target"] = loss_target
    m_in = dict(zip(WEIGHTS, (m_c_ctx, m_ada_w, m_ada_b, m_norm_mix_pre, m_norm_mix_post, m_norm_ffn_pre, m_norm_ffn_post, m_w_in, m_q_norm, m_k_norm, m_gla_gate_w, m_gla_gate_b, m_gla_out_norm, m_s5_a_re, m_s5_a_im, m_s5_log_dt, m_s5_b_re, m_s5_b_im, m_s5_c_re, m_s5_c_im, m_s5_d, m_s5_glu_w, m_s5_glu_b, m_w_br_att, m_w_br_gla, m_w_br_s5, m_w_out, m_ffn_up, m_ffn_conv_w, m_ffn_conv_b, m_ffn_down)))
    v_in = dict(zip(WEIGHTS, (v_c_ctx, v_ada_w, v_ada_b, v_norm_mix_pre, v_norm_mix_post, v_norm_ffn_pre, v_norm_ffn_post, v_w_in, v_q_norm, v_k_norm, v_gla_gate_w, v_gla_gate_b, v_gla_out_norm, v_s5_a_re, v_s5_a_im, v_s5_log_dt, v_s5_b_re, v_s5_b_im, v_s5_c_re, v_s5_c_im, v_s5_d, v_s5_glu_w, v_s5_glu_b, v_w_br_att, v_w_br_gla, v_w_br_s5, v_w_out, v_ffn_up, v_ffn_conv_w, v_ffn_conv_b, v_ffn_down)))
    depth = ada_b.shape[0]
    big_names, small_names = list(BIG), list(SMALL_SHARDED)
    sharded_names = big_names + small_names

    full = {}
    groups = {}
    for n in big_names:
        groups.setdefault(_round_up(given[n].shape[2], 128), []).append(n)
    per_layer = {n: [] for n in big_names}
    for i in range(depth):
        for width, names in groups.items():
            gathered = all_gather_blocks(_pack_rows([given[n][i] for n in names], width, BF16),
                                         f"gather_weights_{width}")
            for n, blocks in zip(names, _unpack_rows(gathered, [given[n].shape[1:] for n in names])):
                per_layer[n].append(_to_full(blocks, BIG[n]))
    for n in big_names:
        full[n] = jnp.stack(per_layer[n])
    small_shapes = [given[n].shape for n in small_names]
    gathered = all_gather_blocks(_pack([given[n] for n in small_names], F32), "gather_small")
    for n, blocks in zip(small_names, _unpack(gathered, small_shapes)):
        full[n] = _to_full(blocks, SMALL_SHARDED[n] + 1)

    w_in_full = jnp.pad(_w_in_reorder(full["w_in"], True), ((0, 0), (0, 0), (0, D_IN_PAD - D_IN)))

    diff = {"x": x}
    diff.update({n: given[n] for n in REPLICATED})
    diff.update({n: full[n] for n in small_names})
    fixed = {"c": c, "ctx": ctx, "loss_target": loss_target}
    d_ff = full["ffn_up"].shape[2] // 2
    for n in big_names:
        fixed[n] = w_in_full if n == "w_in" else full[n]
    fixed["ffn_up"], fixed["ffn_up_v"] = full["ffn_up"][:, :, :d_ff], full["ffn_up"][:, :, d_ff:]
    for n in MATMUL_WEIGHTS:
        diff[n + "_slot"] = jnp.zeros(fixed[n].shape, F32)
    loss_local, grads = jax.value_and_grad(_local_loss)(diff, fixed)
    loss = lax.psum(loss_local, ("x", "y", "c"))

    g_full = {n: grads[n] for n in small_names}
    for n in big_names:
        g_full[n] = grads[n + "_slot"]
    g_full["ffn_up"] = jnp.concatenate([grads["ffn_up_slot"], grads["ffn_up_v_slot"]], axis=2)
    g_full["w_in"] = _w_in_reorder(g_full["w_in"], False)

    out_g, out_d, out_m, out_v = {}, {}, {}, {}
    layer_out = {n: ([], [], [], []) for n in big_names}
    for i in range(depth):
        for width, names in groups.items():
            send = _pack_rows([_to_blocks(g_full[n][i], BIG[n]) for n in names], width, BF16)
            parts = exchange_blocks(send, f"exchange_grads_{width}")
            results = sum_adamw(parts, _pack_rows([given[n][i] for n in names], width, F32),
                                _pack_rows([m_in[n][i] for n in names], width, F32),
                                _pack_rows([v_in[n][i] for n in names], width, F32), f"adamw_{width}")
            for k, res in enumerate(results):
                for n, arr in zip(names, _unpack_rows(res, [given[n].shape[1:] for n in names])):
                    layer_out[n][k].append(arr)
    for n in big_names:
        out_g[n], out_d[n], out_m[n], out_v[n] = (jnp.stack(parts_k) for parts_k in layer_out[n])
    send = _pack_blocks([_to_blocks(g_full[n], SMALL_SHARDED[n] + 1) for n in small_names], F32)
    parts = exchange_blocks(send, "exchange_small")
    results = sum_adamw(parts, _pack([given[n] for n in small_names], F32), _pack([m_in[n] for n in small_names], F32),
                        _pack([v_in[n] for n in small_names], F32), "adamw_small")
    for store, res in zip((out_g, out_d, out_m, out_v), results):
        for n, arr in zip(small_names, _unpack(res, small_shapes)):
            store[n] = arr

    rep_shapes = [given[n].shape for n in REPLICATED]
    parts = all_gather_blocks(_pack([grads[n] for n in REPLICATED], F32), "gather_rep_grads")
    results = sum_adamw(parts, _pack([given[n] for n in REPLICATED], F32), _pack([m_in[n] for n in REPLICATED], F32),
                        _pack([v_in[n] for n in REPLICATED], F32), "adamw_replicated")
    for store, res in zip((out_g, out_d, out_m, out_v), results):
        for n, arr in zip(REPLICATED, _unpack(res, rep_shapes)):
            store[n] = arr

    return (loss, grads["x"], *[out_g[n] for n in WEIGHTS], *[out_d[n] for n in WEIGHTS],
            *[out_m[n] for n in WEIGHTS], *[out_v[n] for n in WEIGHTS])
```

```python
import functools
import math

import numpy as np
import jax
import jax.numpy as jnp
from jax import lax
from jax.experimental import pallas as pl
from jax.experimental.pallas import tpu as pltpu

F32 = jnp.float32
BF16 = jnp.bfloat16
MESH = pl.DeviceIdType.MESH
N_DEV = 8

D_MODEL = 1024
GRID_W = 64
ATT_HEADS, ATT_KV_HEADS, ATT_HEAD_DIM = 4, 2, 64
ATT_WIDTH, ATT_KV_WIDTH = 256, 128
ROPE_THETA = 10000.0
GLA_HEADS, GLA_DK, GLA_DV = 4, 64, 128
GLA_K_WIDTH, GLA_V_WIDTH = 256, 512
GLA_GATE_RANK, GLA_GATE_NORM, GLA_CHUNK = 16, 16.0, 64
S5_GROUPS, S5_GROUP_CH, S5_WIDTH, S5_STATE = 16, 16, 256, 64
S5_FLAT = S5_GROUPS * S5_STATE
N_BRANCH = 3
EPS = 1e-6
IN_SPLITS = (ATT_WIDTH, ATT_KV_WIDTH, ATT_KV_WIDTH, GLA_K_WIDTH, GLA_K_WIDTH, GLA_V_WIDTH, GLA_V_WIDTH,
             GLA_GATE_RANK, GLA_GATE_RANK, S5_WIDTH, N_BRANCH * D_MODEL)
D_IN = sum(IN_SPLITS)
D_IN_PAD = 5632

ADAM_LR, ADAM_B1, ADAM_B2, ADAM_EPS, ADAM_WD, ADAM_STEP = 0.001, 0.9, 0.999, 1e-08, 0.01, 10

VMEM_LIMIT = 56 * 1024 * 1024
MM_VMEM_BUDGET = 40 * 1024 * 1024
PACK_W = 512
PACK_UNIT = 16 * PACK_W

WEIGHTS = ['c_ctx', 'ada_w', 'ada_b', 'norm_mix_pre', 'norm_mix_post', 'norm_ffn_pre', 'norm_ffn_post', 'w_in',
           'q_norm', 'k_norm', 'gla_gate_w', 'gla_gate_b', 'gla_out_norm', 's5_a_re', 's5_a_im', 's5_log_dt',
           's5_b_re', 's5_b_im', 's5_c_re', 's5_c_im', 's5_d', 's5_glu_w', 's5_glu_b', 'w_br_att', 'w_br_gla',
           'w_br_s5', 'w_out', 'ffn_up', 'ffn_conv_w', 'ffn_conv_b', 'ffn_down']
FWD_INPUTS = ['x', 'c', 'ctx'] + WEIGHTS
BIG = {'ada_w': 1, 'w_in': 1, 'w_br_att': 1, 'w_br_gla': 1, 'w_br_s5': 1, 'w_out': 0, 'ffn_up': 1, 'ffn_down': 0}
SMALL_SHARDED = {'gla_gate_w': 2, 'gla_gate_b': 1, 's5_glu_w': 0, 'ffn_conv_w': 1}
SHARDED = {**BIG, **SMALL_SHARDED}
MATMUL_WEIGHTS = list(BIG) + ['ffn_up_v']
REPLICATED = [n for n in WEIGHTS if n not in SHARDED]


def _pick(n, cands):
    for cand in cands:
        if n % cand == 0:
            return cand
    return n


def _params(sem):
    return pltpu.CompilerParams(dimension_semantics=sem, vmem_limit_bytes=VMEM_LIMIT)


_DIMS = {"nn": ((1,), (0,)), "nt": ((1,), (1,)), "tn": ((0,), (0,))}


def _mm(a, b, mode, name):
    if mode == "tn":
        K, M = a.shape
    else:
        M, K = a.shape
    N = b.shape[0] if mode == "nt" else b.shape[1]
    tm = _pick(M, (1408, 1024, 768, 512, 256, 128))
    tn = _pick(N, (1408, 1024, 512, 256, 128))
    tk_options = [K] + [t for t in (2816, 1408, 1024, 768, 512, 256, 128) if t < K and K % t == 0]
    for tk in tk_options:
        blocks = tm * tk * a.dtype.itemsize + tk * tn * b.dtype.itemsize + tm * tn * 4
        if 2 * blocks <= MM_VMEM_BUDGET:
            break
    nk = K // tk
    dims = (_DIMS[mode], ((), ()))

    def body(a_ref, b_ref, o_ref):
        acc = lax.dot_general(a_ref[...].astype(BF16), b_ref[...].astype(BF16), dims, preferred_element_type=F32)
        if nk == 1:
            o_ref[...] = acc
        else:
            k = pl.program_id(2)

            @pl.when(k == 0)
            def _():
                o_ref[...] = acc

            @pl.when(k > 0)
            def _():
                o_ref[...] += acc

    a_spec = (pl.BlockSpec((tk, tm), lambda i, j, k: (k, i)) if mode == "tn"
              else pl.BlockSpec((tm, tk), lambda i, j, k: (i, k)))
    b_spec = (pl.BlockSpec((tn, tk), lambda i, j, k: (j, k)) if mode == "nt"
              else pl.BlockSpec((tk, tn), lambda i, j, k: (k, j)))
    return pl.pallas_call(
        body, name=name, grid=(M // tm, N // tn, nk),
        in_specs=[a_spec, b_spec], out_specs=pl.BlockSpec((tm, tn), lambda i, j, k: (i, j)),
        out_shape=jax.ShapeDtypeStruct((M, N), F32),
        compiler_params=_params(("parallel", "parallel", "arbitrary")),
    )(a, b)


FROM_XLA = (False, False)


def _rounded(x, from_kernel):
    return x if from_kernel else x.astype(BF16)


@functools.partial(jax.custom_vjp, nondiff_argnums=(2,))
def matmul(a, w, from_kernel=FROM_XLA):
    return _mm(_rounded(a, from_kernel[0]), w.astype(BF16), "nn", "mm_fwd")


def _matmul_fwd(a, w, from_kernel):
    ab, wb = _rounded(a, from_kernel[0]), w.astype(BF16)
    return _mm(ab, wb, "nn", "mm_fwd"), (ab, wb)


def _matmul_bwd(from_kernel, res, dy):
    ab, wb = res
    dyb = _rounded(dy, from_kernel[1])
    return _mm(dyb, wb, "nt", "mm_dx"), _mm(ab, dyb, "tn", "mm_dw")


matmul.defvjp(_matmul_fwd, _matmul_bwd)


@functools.partial(jax.custom_vjp, nondiff_argnums=(3,))
def linear(a, w, w_grad_slot, from_kernel=FROM_XLA):
    del w_grad_slot
    return _mm(_rounded(a, from_kernel[0]), w, "nn", "lin_fwd")


def _linear_fwd(a, w, w_grad_slot, from_kernel):
    del w_grad_slot
    ab = _rounded(a, from_kernel[0])
    return _mm(ab, w, "nn", "lin_fwd"), (ab, w)


def _linear_bwd(from_kernel, res, dy):
    ab, w = res
    dyb = _rounded(dy, from_kernel[1])
    return _mm(dyb, w, "nt", "lin_dx"), jnp.zeros_like(w), _mm(ab, dyb, "tn", "lin_dw")


linear.defvjp(_linear_fwd, _linear_bwd)


def _attn_fwd_call(q, k, v):
    H, Tq, d = q.shape
    KV, Tk, _ = k.shape
    G = H // KV
    tq = _pick(Tq, (256, 128, 64))

    def body(q_ref, k_ref, v_ref, o_ref, lse_ref):
        s = lax.dot_general(q_ref[0], k_ref[0], (_DIMS["nt"], ((), ())), preferred_element_type=F32)
        m = jnp.max(s, axis=1, keepdims=True)
        p = jnp.exp(s - m)
        l = jnp.sum(p, axis=1, keepdims=True)
        o_ref[0] = jnp.dot(p.astype(BF16), v_ref[0], preferred_element_type=F32) * (1.0 / l)
        lse_ref[0] = m + jnp.log(l)

    return pl.pallas_call(
        body, name="attn_fwd", grid=(H, Tq // tq),
        in_specs=[pl.BlockSpec((1, tq, d), lambda h, i: (h, i, 0)),
                  pl.BlockSpec((1, Tk, d), lambda h, i: (h // G, 0, 0)),
                  pl.BlockSpec((1, Tk, d), lambda h, i: (h // G, 0, 0))],
        out_specs=[pl.BlockSpec((1, tq, d), lambda h, i: (h, i, 0)),
                   pl.BlockSpec((1, tq, 1), lambda h, i: (h, i, 0))],
        out_shape=[jax.ShapeDtypeStruct((H, Tq, d), F32), jax.ShapeDtypeStruct((H, Tq, 1), F32)],
        compiler_params=_params(("parallel", "parallel")),
    )(q, k, v)


def _attn_bwd_call(q, k, v, o, lse, do, scale):
    H, Tq, d = q.shape
    KV, Tk, _ = k.shape
    G = H // KV
    tq = _pick(Tq, (256, 128, 64))
    ck = _pick(Tk, (1408, 1024, 512, 256, 128, 64))
    nck = Tk // ck

    def body(q_ref, k_ref, v_ref, o_ref, lse_ref, do_ref, dq_ref, dk_ref, dv_ref):
        @pl.when((pl.program_id(1) == 0) & (pl.program_id(2) == 0))
        def _():
            dk_ref[...] = jnp.zeros_like(dk_ref)
            dv_ref[...] = jnp.zeros_like(dv_ref)

        qb = q_ref[0]
        do = do_ref[0]
        dob = do.astype(BF16)
        delta = jnp.sum(do * o_ref[0], axis=1, keepdims=True)
        lse = lse_ref[0]
        dq = jnp.zeros((tq, d), F32)
        for cidx in range(nck):
            rows = slice(cidx * ck, (cidx + 1) * ck)
            ks = k_ref[0, rows, :]
            vs = v_ref[0, rows, :]
            s = lax.dot_general(qb, ks, (_DIMS["nt"], ((), ())), preferred_element_type=F32)
            p = jnp.exp(s - lse)
            dv_ref[0, rows, :] += lax.dot_general(p.astype(BF16), dob, (_DIMS["tn"], ((), ())),
                                                  preferred_element_type=F32)
            dp = lax.dot_general(dob, vs, (_DIMS["nt"], ((), ())), preferred_element_type=F32)
            dsb = (p * (dp - delta)).astype(BF16)
            dq = dq + jnp.dot(dsb, ks, preferred_element_type=F32)
            dk_ref[0, rows, :] += lax.dot_general(dsb, qb, (_DIMS["tn"], ((), ())), preferred_element_type=F32)
        dq_ref[0] = dq * scale

    q_spec = pl.BlockSpec((1, tq, d), lambda kv, g, i: (kv * G + g, i, 0))
    kv_spec = pl.BlockSpec((1, Tk, d), lambda kv, g, i: (kv, 0, 0))
    return pl.pallas_call(
        body, name="attn_bwd", grid=(KV, G, Tq // tq),
        in_specs=[q_spec, kv_spec, kv_spec, q_spec,
                  pl.BlockSpec((1, tq, 1), lambda kv, g, i: (kv * G + g, i, 0)), q_spec],
        out_specs=[q_spec, kv_spec, kv_spec],
        out_shape=[jax.ShapeDtypeStruct((H, Tq, d), F32), jax.ShapeDtypeStruct((KV, Tk, d), F32),
                   jax.ShapeDtypeStruct((KV, Tk, d), F32)],
        compiler_params=_params(("arbitrary", "arbitrary", "arbitrary")),
    )(q, k, v, o, lse, do)


ATT_SCALE = ATT_HEAD_DIM ** -0.5


@jax.custom_vjp
def attention(q, k, v):
    return _attn_fwd_call((q * ATT_SCALE).astype(BF16), k.astype(BF16), v.astype(BF16))[0]


def _attention_fwd(q, k, v):
    qb, kb, vb = (q * ATT_SCALE).astype(BF16), k.astype(BF16), v.astype(BF16)
    o, lse = _attn_fwd_call(qb, kb, vb)
    return o, (qb, kb, vb, o, lse)


def _attention_bwd(res, do):
    qb, kb, vb, o, lse = res
    return _attn_bwd_call(qb, kb, vb, o, lse, do, ATT_SCALE)


attention.defvjp(_attention_fwd, _attention_bwd)


_ORDER_DOWN = {0: False, 1: True, 2: True, 3: False}
_ORDER_ADJOINT = {0: 2, 1: 3}


def _scan_tables(a_re, a_im, down):
    pw_re, pw_im = [a_re], [a_im]
    for _ in range(7):
        pw_re, pw_im = (pw_re + [pw_re[-1] * a_re - pw_im[-1] * a_im],
                        pw_im + [pw_re[-1] * a_im + pw_im[-1] * a_re])
    carry_rows = list(range(7, -1, -1)) if down else list(range(8))
    rows_re = [pw_re[r] for r in carry_rows] + [pw_re[0], pw_re[1], pw_re[3]]
    rows_im = [pw_im[r] for r in carry_rows] + [pw_im[0], pw_im[1], pw_im[3]]
    tab = jnp.concatenate([jnp.stack(rows_re), jnp.stack(rows_im)], axis=1)
    return jnp.concatenate([tab, jnp.zeros((5, 2 * S5_FLAT), F32)], axis=0)


def _scan_call(bu, a_re, a_im, order, ctx_len):
    T, W2 = bu.shape
    P = W2 // 2
    rb = _pick(math.gcd(ctx_len, T - ctx_len), (256, 128, 64, 32, 16, 8))
    nblk, cb = T // rb, ctx_len // rb
    ntile = rb // 8
    down = _ORDER_DOWN[order]
    tab = _scan_tables(a_re, a_im, down)

    def blk(n):
        return _block_in_order(n, nblk, cb, order)

    def body(bu_ref, tab_ref, s_ref, carry_ref):
        @pl.when(pl.program_id(0) == 0)
        def _():
            carry_ref[...] = jnp.zeros_like(carry_ref)

        row = lax.broadcasted_iota(jnp.int32, (8, P), 0)
        cp_re, cp_im = tab_ref[0:8, 0:P], tab_ref[0:8, P:2 * P]
        steps = []
        for j, sh in enumerate((1, 2, 4)):
            keep = (row < 8 - sh) if down else (row >= sh)
            steps.append((8 - sh if down else sh, keep, tab_ref[8 + j:9 + j, 0:P], tab_ref[8 + j:9 + j, P:2 * P]))

        def tile(j, carry):
            c_re, c_im = carry
            t = (ntile - 1 - j) if down else j
            r0 = pl.multiple_of(t * 8, 8)
            x_re = bu_ref[pl.ds(r0, 8), 0:P]
            x_im = bu_ref[pl.ds(r0, 8), P:2 * P]
            for shift, keep, p_re, p_im in steps:
                y_re = jnp.where(keep, pltpu.roll(x_re, shift, 0), 0.0)
                y_im = jnp.where(keep, pltpu.roll(x_im, shift, 0), 0.0)
                x_re, x_im = x_re + p_re * y_re - p_im * y_im, x_im + p_re * y_im + p_im * y_re
            x_re, x_im = x_re + cp_re * c_re - cp_im * c_im, x_im + cp_re * c_im + cp_im * c_re
            s_ref[pl.ds(r0, 8), 0:P] = x_re
            s_ref[pl.ds(r0, 8), P:2 * P] = x_im
            last = 0 if down else 7
            return x_re[last:last + 1, :], x_im[last:last + 1, :]

        c_re, c_im = lax.fori_loop(0, ntile, tile, (carry_ref[0:1, 0:P], carry_ref[0:1, P:2 * P]))
        carry_ref[0:1, 0:P] = c_re
        carry_ref[0:1, P:2 * P] = c_im

    return pl.pallas_call(
        body, name=f"s5_scan_{order}", grid=(nblk,),
        in_specs=[pl.BlockSpec((rb, W2), lambda n: (blk(n), 0)), pl.BlockSpec((16, W2), lambda n: (0, 0))],
        out_specs=pl.BlockSpec((rb, W2), lambda n: (blk(n), 0)),
        out_shape=jax.ShapeDtypeStruct((T, W2), F32),
        scratch_shapes=[pltpu.VMEM((8, W2), F32)],
        compiler_params=_params(("arbitrary",)),
    )(bu, tab)


def _prev_in_order(s, order, ctx_len):
    zero = jnp.zeros_like(s[:1])
    if order == 0:
        return jnp.concatenate([zero, s[:-1]], axis=0)
    return jnp.concatenate([s[1:ctx_len], zero, s[ctx_len + 1:], s[:1]], axis=0)


@functools.partial(jax.custom_vjp, nondiff_argnums=(3, 4))
def s5_scan(bu, a_re, a_im, order, ctx_len):
    return _scan_call(bu, a_re, a_im, order, ctx_len)


def _s5_scan_fwd(bu, a_re, a_im, order, ctx_len):
    s = _scan_call(bu, a_re, a_im, order, ctx_len)
    return s, (s, a_re, a_im)


def _s5_scan_bwd(order, ctx_len, res, ds):
    s, a_re, a_im = res
    lam = _scan_call(ds, a_re, -a_im, _ORDER_ADJOINT[order], ctx_len)
    P = a_re.shape[0]
    sp = _prev_in_order(s, order, ctx_len)
    l_re, l_im, p_re, p_im = lam[:, :P], lam[:, P:], sp[:, :P], sp[:, P:]
    g_re = jnp.sum(l_re * p_re + l_im * p_im, axis=0)
    g_im = jnp.sum(l_im * p_re - l_re * p_im, axis=0)
    return lam, g_re, g_im


s5_scan.defvjp(_s5_scan_fwd, _s5_scan_bwd)


def _dot(a, b, mode, precision=None):
    return lax.dot_general(a, b, (_DIMS[mode], ((), ())), preferred_element_type=F32, precision=precision)


def _dot_with_mask(a, b, mode, mask_first):
    x = b if mask_first else a
    hi = x.astype(BF16)
    lo = (x - hi.astype(F32)).astype(BF16)
    if mask_first:
        mask = a.astype(BF16)
        return _dot(mask, hi, mode) + _dot(mask, lo, mode)
    mask = b.astype(BF16)
    return _dot(hi, mask, mode) + _dot(lo, mask, mode)


def _block_in_order(n, nblk, cblk, order):
    if order == 0:
        return n
    if order == 1:
        return jnp.where(n < cblk, cblk - 1 - n, nblk - 1 - (n - cblk))
    if order == 2:
        return nblk - 1 - n
    return jnp.where(n < nblk - cblk, cblk + n, n - (nblk - cblk))


def _gla_chunk_terms(qn, kn, gn, tri, reverse):
    b = _dot_with_mask(tri, gn, "tn" if reverse else "nn", True)
    edge = 0 if reverse else GLA_CHUNK - 1
    b_end = b[edge:edge + 1, :]
    e_pos, e_neg, e_end = jnp.exp(b), jnp.exp(-b), jnp.exp(b_end - b)
    return b_end, e_pos, e_neg, e_end, qn * e_pos, kn * e_neg, kn * e_end


def _gla_masks():
    L = GLA_CHUNK
    rows, cols = lax.broadcasted_iota(jnp.int32, (L, L), 0), lax.broadcasted_iota(jnp.int32, (L, L), 1)
    return rows >= cols, rows <= cols


def _gla_blocking(n_t, ctx_len):
    n_chunks, ctx_chunks = n_t // GLA_CHUNK, ctx_len // GLA_CHUNK
    per_block = _pick(math.gcd(ctx_chunks, n_chunks - ctx_chunks), (4, 2, 1))
    return n_chunks, per_block, GLA_CHUNK * per_block, n_chunks // per_block, ctx_chunks // per_block


def _gla_fwd_call(q, k, v, g, reverse, ctx_len):
    H, T, dk = q.shape
    dv = v.shape[-1]
    n_chunks, cb, rb, nb, cblk = _gla_blocking(T, ctx_len)
    L = GLA_CHUNK
    order = 1 if reverse else 0

    def body(q_ref, k_ref, v_ref, g_ref, o_ref, sb_ref, s_ref):
        @pl.when(pl.program_id(0) == 0)
        def _():
            s_ref[...] = jnp.zeros_like(s_ref)

        lower, upper = _gla_masks()
        tri = lower.astype(F32)
        seen = upper if reverse else lower
        ones = jnp.ones((L, dv), F32)
        states = [s_ref[h] for h in range(H)]
        for n in (reversed(range(cb)) if reverse else range(cb)):
            rows = slice(n * L, (n + 1) * L)
            for h in range(H):
                qn, kn, vn, gn = q_ref[h, rows, :], k_ref[h, rows, :], v_ref[h, rows, :], g_ref[h, rows, :]
                _, _, _, _, q_in, k_in, k_end = _gla_chunk_terms(qn, kn, gn, tri, reverse)
                att = jnp.where(seen, _dot(q_in.astype(BF16), k_in.astype(BF16), "nt"), 0.0)
                vb = vn.astype(BF16)
                sb_ref[h, n] = states[h]
                o_ref[h, rows, :] = (_dot(att.astype(BF16), vb, "nn")
                                     + _dot(q_in.astype(BF16), states[h].astype(BF16), "nn"))
                decay = jnp.exp(_dot_with_mask(gn, ones, "tn", False))
                states[h] = decay * states[h] + _dot(k_end.astype(BF16), vb, "tn")
        for h in range(H):
            s_ref[h] = states[h]

    def at(i):
        return 0, _block_in_order(i, nb, cblk, order), 0

    row_k, row_v = pl.BlockSpec((H, rb, dk), at), pl.BlockSpec((H, rb, dv), at)
    return pl.pallas_call(
        body, name="gla_fwd", grid=(nb,),
        in_specs=[row_k, row_k, row_v, row_k],
        out_specs=[row_v, pl.BlockSpec((H, cb, dk, dv), lambda i: (*at(i), 0))],
        out_shape=[jax.ShapeDtypeStruct((H, T, dv), F32), jax.ShapeDtypeStruct((H, n_chunks, dk, dv), F32)],
        scratch_shapes=[pltpu.VMEM((H, dk, dv), F32)],
        compiler_params=_params(("arbitrary",)),
    )(q, k, v, g)


def _gla_bwd_call(q, k, v, g, sb, do, reverse, ctx_len):
    H, T, dk = q.shape
    dv = v.shape[-1]
    n_chunks, cb, rb, nb, cblk = _gla_blocking(T, ctx_len)
    L = GLA_CHUNK
    order = 3 if reverse else 2

    def body(q_ref, k_ref, v_ref, g_ref, sb_ref, do_ref, dq_ref, dk_ref, dv_ref, dg_ref, ds_ref):
        @pl.when(pl.program_id(0) == 0)
        def _():
            ds_ref[...] = jnp.zeros_like(ds_ref)

        lower, upper = _gla_masks()
        tri = lower.astype(F32)
        seen = upper if reverse else lower
        ones = jnp.ones((L, dv), F32)
        ones8 = jnp.ones((8, dv), F32)
        d_states = [ds_ref[h] for h in range(H)]
        for n in (range(cb) if reverse else reversed(range(cb))):
            rows = slice(n * L, (n + 1) * L)
            for h in range(H):
                qn, kn, vn, gn = q_ref[h, rows, :], k_ref[h, rows, :], v_ref[h, rows, :], g_ref[h, rows, :]
                state, d_state = sb_ref[h, n], d_states[h]
                b_end, e_pos, e_neg, e_end, q_in, k_in, k_end = _gla_chunk_terms(qn, kn, gn, tri, reverse)
                q_b, k_b, ke_b, vb = q_in.astype(BF16), k_in.astype(BF16), k_end.astype(BF16), vn.astype(BF16)
                dob = do_ref[h, rows, :].astype(BF16)
                dsb = d_state.astype(BF16)
                att = jnp.where(seen, _dot(q_b, k_b, "nt"), 0.0).astype(BF16)
                d_att = jnp.where(seen, _dot(dob, vb, "nt"), 0.0).astype(BF16)
                d_qin = _dot(d_att, k_b, "nn") + _dot(dob, state.astype(BF16), "nt")
                d_kin = _dot(d_att, q_b, "tn")
                d_kend = _dot(vb, dsb, "nt")
                dv_ref[h, rows, :] = _dot(att, dob, "tn") + _dot(ke_b, dsb, "nn")
                through_decay = _dot_with_mask(ones8, state * d_state, "nt", True)[0:1, :]
                d_bend = jnp.sum(d_kend * k_end, axis=0, keepdims=True) + jnp.exp(b_end) * through_decay
                d_b = d_qin * q_in - d_kin * k_in - d_kend * k_end
                dg_ref[h, rows, :] = _dot_with_mask(tri, d_b, "nn" if reverse else "tn", True) + d_bend
                dq_ref[h, rows, :] = d_qin * e_pos
                dk_ref[h, rows, :] = d_kin * e_neg + d_kend * e_end
                decay = jnp.exp(_dot_with_mask(gn, ones, "tn", False))
                d_states[h] = _dot(q_b, dob, "tn") + decay * d_state
        for h in range(H):
            ds_ref[h] = d_states[h]

    def at(i):
        return 0, _block_in_order(i, nb, cblk, order), 0

    row_k, row_v = pl.BlockSpec((H, rb, dk), at), pl.BlockSpec((H, rb, dv), at)
    return pl.pallas_call(
        body, name="gla_bwd", grid=(nb,),
        in_specs=[row_k, row_k, row_v, row_k, pl.BlockSpec((H, cb, dk, dv), lambda i: (*at(i), 0)), row_v],
        out_specs=[row_k, row_k, row_v, row_k],
        out_shape=[jax.ShapeDtypeStruct((H, T, dk), F32), jax.ShapeDtypeStruct((H, T, dk), F32),
                   jax.ShapeDtypeStruct((H, T, dv), F32), jax.ShapeDtypeStruct((H, T, dk), F32)],
        scratch_shapes=[pltpu.VMEM((H, dk, dv), F32)],
        compiler_params=_params(("arbitrary",)),
    )(q, k, v, g, sb, do)


@functools.partial(jax.custom_vjp, nondiff_argnums=(4, 5))
def gla_scan(q, k, v, g, reverse, ctx_len):
    return _gla_fwd_call(q, k, v, g, reverse, ctx_len)[0]


def _gla_scan_fwd(q, k, v, g, reverse, ctx_len):
    o, sb = _gla_fwd_call(q, k, v, g, reverse, ctx_len)
    return o, (q, k, v, g, sb)


def _gla_scan_bwd(reverse, ctx_len, res, do):
    q, k, v, g, sb = res
    return _gla_bwd_call(q, k, v, g, sb, do, reverse, ctx_len)


gla_scan.defvjp(_gla_scan_fwd, _gla_scan_bwd)


@functools.partial(jax.custom_vjp, nondiff_argnums=(1,))
def split_cols(x, sizes):
    points = np.cumsum((0,) + tuple(sizes))
    return tuple(x[:, int(a):int(b)] for a, b in zip(points[:-1], points[1:]))


def _split_cols_fwd(x, sizes):
    return split_cols(x, sizes), x.shape[1]


def _split_cols_bwd(sizes, width, cts):
    parts = list(cts)
    if width > sum(sizes):
        parts.append(jnp.zeros((cts[0].shape[0], width - sum(sizes)), cts[0].dtype))
    return (jnp.concatenate(parts, axis=1),)


split_cols.defvjp(_split_cols_fwd, _split_cols_bwd)


def _position():
    return lax.axis_index("x"), lax.axis_index("y"), lax.axis_index("c")


def all_gather_blocks(shard, name):
    R, W = shard.shape

    def body(x_ref, out_ref, send_sems, recv_sems, local_sem):
        x, y, c = _position()
        me, sibling = (x, y, c), (x, y, 1 - c)
        chips = [(1 - x, y), (x, 1 - y), (1 - x, 1 - y)]

        def slot(px, py, pc):
            return out_ref.at[4 * px + 2 * py + pc]

        def copy(k, block, to, src=None):
            return pltpu.make_async_remote_copy(
                src_ref=slot(*block) if src is None else src, dst_ref=slot(*block),
                send_sem=send_sems.at[k], recv_sem=recv_sems.at[k], device_id=to, device_id_type=MESH)

        mine = pltpu.make_async_copy(x_ref, slot(*me), local_sem)
        mine.start()
        first = [copy(0, me, sibling, src=x_ref)]
        first += [copy(1 + j, me, (*chip, c), src=x_ref) for j, chip in enumerate(chips)]
        for cp in first:
            cp.start()
        passed = [copy(4 + j, (*chip, c), sibling) for j, chip in enumerate(chips)]
        for j, chip in enumerate(chips):
            copy(1 + j, (*chip, c), me).wait_recv()
            passed[j].start()
        copy(0, sibling, me).wait_recv()
        for j, chip in enumerate(chips):
            copy(4 + j, (*chip, 1 - c), me).wait_recv()
        for cp in first + passed:
            cp.wait_send()
        mine.wait()

    return pl.pallas_call(
        body, name=name,
        out_shape=jax.ShapeDtypeStruct((N_DEV, R, W), shard.dtype),
        in_specs=[pl.BlockSpec(memory_space=pltpu.HBM)], out_specs=pl.BlockSpec(memory_space=pltpu.HBM),
        scratch_shapes=[pltpu.SemaphoreType.DMA((7,)), pltpu.SemaphoreType.DMA((7,)), pltpu.SemaphoreType.DMA],
    )(shard)


def exchange_blocks(blocks, name):
    _, R, W = blocks.shape
    flips = [(fx, fy, fc) for fx in (0, 1) for fy in (0, 1) for fc in (0, 1)][1:]

    def body(x_ref, out_ref, send_sems, recv_sems, local_sem):
        x, y, c = _position()
        me = 4 * x + 2 * y + c
        mine = pltpu.make_async_copy(x_ref.at[me], out_ref.at[me], local_sem)
        mine.start()
        copies = []
        for k, (fx, fy, fc) in enumerate(flips):
            px, py, pc = x ^ fx, y ^ fy, c ^ fc
            peer = 4 * px + 2 * py + pc
            copies.append((
                pltpu.make_async_remote_copy(src_ref=x_ref.at[peer], dst_ref=out_ref.at[me],
                                             send_sem=send_sems.at[k], recv_sem=recv_sems.at[k],
                                             device_id=(px, py, pc), device_id_type=MESH),
                pltpu.make_async_remote_copy(src_ref=x_ref.at[peer], dst_ref=out_ref.at[peer],
                                             send_sem=send_sems.at[k], recv_sem=recv_sems.at[k],
                                             device_id=(px, py, pc), device_id_type=MESH)))
        for send, _ in copies:
            send.start()
        for _, recv in copies:
            recv.wait_recv()
        for send, _ in copies:
            send.wait_send()
        mine.wait()

    return pl.pallas_call(
        body, name=name,
        out_shape=jax.ShapeDtypeStruct(blocks.shape, blocks.dtype),
        in_specs=[pl.BlockSpec(memory_space=pltpu.HBM)], out_specs=pl.BlockSpec(memory_space=pltpu.HBM),
        scratch_shapes=[pltpu.SemaphoreType.DMA((7,)), pltpu.SemaphoreType.DMA((7,)), pltpu.SemaphoreType.DMA],
    )(blocks)


def sum_adamw(parts, w, m, v, name):
    _, R, W = parts.shape
    tr = _pick(R, (512, 256, 128, 64, 32, 16, 8))

    def body(p_ref, w_ref, m_ref, v_ref, g_out, d_out, m_out, v_out):
        g = p_ref[0].astype(F32)
        for j in range(1, N_DEV):
            g = g + p_ref[j].astype(F32)
        m_new = ADAM_B1 * m_ref[...] + (1.0 - ADAM_B1) * g
        v_new = ADAM_B2 * v_ref[...] + (1.0 - ADAM_B2) * (g * g)
        m_hat = m_new / (1.0 - ADAM_B1 ** ADAM_STEP)
        v_hat = v_new / (1.0 - ADAM_B2 ** ADAM_STEP)
        g_out[...] = g
        d_out[...] = -ADAM_LR * (m_hat / (jnp.sqrt(v_hat) + ADAM_EPS) + ADAM_WD * w_ref[...])
        m_out[...] = m_new
        v_out[...] = v_new

    row = pl.BlockSpec((tr, W), lambda i: (i, 0))
    return pl.pallas_call(
        body, name=name, grid=(R // tr,),
        in_specs=[pl.BlockSpec((N_DEV, tr, W), lambda i: (0, i, 0)), row, row, row],
        out_specs=[row, row, row, row],
        out_shape=[jax.ShapeDtypeStruct((R, W), F32)] * 4,
        compiler_params=_params(("parallel",)),
    )(parts, w, m, v)


def _padded(n):
    return -(-n // PACK_UNIT) * PACK_UNIT


def _pack(arrays, dtype):
    segs = []
    for arr in arrays:
        flat = arr.reshape(-1).astype(dtype)
        segs.append(jnp.pad(flat, (0, _padded(flat.size) - flat.size)))
    return jnp.concatenate(segs).reshape(-1, PACK_W)


def _pack_blocks(arrays, dtype):
    segs = []
    for arr in arrays:
        flat = arr.reshape(N_DEV, -1).astype(dtype)
        segs.append(jnp.pad(flat, ((0, 0), (0, _padded(flat.shape[1]) - flat.shape[1]))))
    return jnp.concatenate(segs, axis=1).reshape(N_DEV, -1, PACK_W)


def _unpack(buf, shapes):
    lead = buf.shape[:-2]
    flat = buf.reshape(*lead, -1)
    out, off = [], 0
    for shape in shapes:
        n = int(np.prod(shape))
        out.append(flat[..., off:off + n].reshape(*lead, *shape))
        off += _padded(n)
    return out


def _round_up(n, unit):
    return -(-n // unit) * unit


def _pack_rows(arrays, width, dtype):
    parts = []
    for arr in arrays:
        r, c = arr.shape[-2:]
        pad = [(0, 0)] * (arr.ndim - 2) + [(0, _round_up(r, 16) - r), (0, width - c)]
        parts.append(jnp.pad(arr.astype(dtype), pad))
    return jnp.concatenate(parts, axis=-2)


def _unpack_rows(buf, shapes):
    out, off = [], 0
    for r, c in shapes:
        out.append(buf[..., off:off + r, :c])
        off += _round_up(r, 16)
    return out


def _to_full(blocks, axis):
    moved = jnp.moveaxis(blocks, 0, axis)
    shape = list(moved.shape)
    shape[axis:axis + 2] = [shape[axis] * shape[axis + 1]]
    return moved.reshape(shape)


def _to_blocks(full, axis):
    shape = list(full.shape)
    shape[axis:axis + 1] = [N_DEV, shape[axis] // N_DEV]
    return jnp.moveaxis(full.reshape(shape), axis, 0)


def rms_norm(x, gain):
    return x * lax.rsqrt(jnp.mean(x * x, axis=-1, keepdims=True) + EPS) * gain


def _rope_tables(n_tokens):
    rows = n_tokens // GRID_W
    row = jnp.repeat(jnp.arange(rows, dtype=F32), GRID_W)
    col = jnp.tile(jnp.arange(GRID_W, dtype=F32), rows)
    n_freq = ATT_HEAD_DIM // 4
    inv_freq = ROPE_THETA ** (-jnp.arange(n_freq, dtype=F32) / n_freq)
    ang = jnp.stack([row[:, None] * inv_freq, col[:, None] * inv_freq], axis=1)
    return jnp.cos(ang), jnp.sin(ang)


def _rope(x, cos, sin):
    n_t, nh, hd = x.shape
    xr = x.reshape(n_t, nh, 2, 2, hd // 4)
    x1, x2 = xr[..., 0, :], xr[..., 1, :]
    cs, sn = cos[:, None], sin[:, None]
    return jnp.stack([x1 * cs - x2 * sn, x2 * cs + x1 * sn], axis=-2).reshape(n_t, nh, hd)


W_IN_ORDER = (0, 1, 2, 3, 4, 5, 6, 9, 10, 7, 8)
W_IN_SIZES = tuple(IN_SPLITS[s] for s in W_IN_ORDER)


def _w_in_reorder(w, to_kernel_order):
    if to_kernel_order:
        points = np.cumsum((0,) + IN_SPLITS)
        pieces = [w[..., int(points[s]):int(points[s + 1])] for s in W_IN_ORDER]
    else:
        points = np.cumsum((0,) + W_IN_SIZES)
        where = {s: j for j, s in enumerate(W_IN_ORDER)}
        pieces = [w[..., int(points[where[s]]):int(points[where[s] + 1])] for s in range(len(IN_SPLITS))]
    return jnp.concatenate(pieces, axis=-1)


def _s5_discretize(a_re, a_im, log_dt, b_re, b_im):
    dt = jnp.exp(log_dt)[:, None]
    mag = jnp.exp(a_re * dt)
    ab_re, ab_im = mag * jnp.cos(a_im * dt), mag * jnp.sin(a_im * dt)
    den = a_re * a_re + a_im * a_im
    f_re = ((ab_re - 1.0) * a_re + ab_im * a_im) / den
    f_im = (ab_im * a_re - (ab_re - 1.0) * a_im) / den
    bb_re = f_re[..., None] * b_re - f_im[..., None] * b_im
    bb_im = f_re[..., None] * b_im + f_im[..., None] * b_re
    return ab_re, ab_im, bb_re, bb_im


def _s5_direction(u, lp, d, ctx_len):
    ab_re, ab_im, bb_re, bb_im = _s5_discretize(lp["s5_a_re"][d], lp["s5_a_im"][d], lp["s5_log_dt"][d],
                                                lp["s5_b_re"][d], lp["s5_b_im"][d])
    eye = jnp.eye(S5_GROUPS, dtype=F32)
    b_cat = jnp.concatenate([jnp.einsum("gph,gk->ghkp", bb_re, eye).reshape(S5_WIDTH, S5_FLAT),
                             jnp.einsum("gph,gk->ghkp", bb_im, eye).reshape(S5_WIDTH, S5_FLAT)], axis=1)
    c_cat = jnp.concatenate([jnp.einsum("ghp,gk->gpkh", lp["s5_c_re"][d], eye).reshape(S5_FLAT, S5_WIDTH),
                             -jnp.einsum("ghp,gk->gpkh", lp["s5_c_im"][d], eye).reshape(S5_FLAT, S5_WIDTH)], axis=0)
    bu = matmul(u, b_cat, (False, True))
    s = s5_scan(bu, ab_re.reshape(-1), ab_im.reshape(-1), d, ctx_len)
    return matmul(s, c_cat, (True, False))


def _s5_branch(u, lp, ctx_len):
    y = _s5_direction(u, lp, 0, ctx_len) + _s5_direction(u, lp, 1, ctx_len) + lp["s5_d"] * u
    y = jax.nn.gelu(y)
    return y * jax.nn.sigmoid(matmul(y, lp["s5_glu_w"], FROM_XLA) + lp["s5_glu_b"])


def _heads(a, nh):
    return a.reshape(a.shape[0], nh, a.shape[1] // nh).transpose(1, 0, 2)


def _token_mixer(h, lp, rope, ctx_len, with_ctx_out):
    n_t = h.shape[0]
    proj = linear(h, lp["w_in"], lp["w_in_slot"], (True, False))
    aq, ak, av, gq, gk, gv, gr, su, bg, glf, glb = split_cols(proj, W_IN_SIZES)

    aq = rms_norm(aq.reshape(n_t, ATT_HEADS, ATT_HEAD_DIM), lp["q_norm"])
    ak = rms_norm(ak.reshape(n_t, ATT_KV_HEADS, ATT_HEAD_DIM), lp["k_norm"])
    aq = jnp.concatenate([aq[:ctx_len], _rope(aq[ctx_len:], *rope)], axis=0).transpose(1, 0, 2)
    ak = jnp.concatenate([ak[:ctx_len], _rope(ak[ctx_len:], *rope)], axis=0).transpose(1, 0, 2)
    av = av.reshape(n_t, ATT_KV_HEADS, ATT_HEAD_DIM).transpose(1, 0, 2)
    o_att_lat = attention(aq[:, ctx_len:], ak, av)
    if with_ctx_out:
        o_att_ctx = attention(aq[:, :ctx_len], ak[:, :ctx_len], av[:, :ctx_len])
        o_att = jnp.concatenate([o_att_ctx, o_att_lat], axis=1)
    else:
        o_att = o_att_lat
    o_att = o_att.transpose(1, 0, 2).reshape(-1, ATT_WIDTH)

    def log_decay(low, d):
        z = jnp.dot(low, lp["gla_gate_w"][d]) + lp["gla_gate_b"][d]
        return _heads(jax.nn.log_sigmoid(z) / GLA_GATE_NORM, GLA_HEADS)

    q_g, k_g, v_g = _heads(gq, GLA_HEADS) * (GLA_DK ** -0.5), _heads(gk, GLA_HEADS), _heads(gv, GLA_HEADS)
    o_f = gla_scan(q_g, k_g, v_g, log_decay(glf, 0), False, ctx_len)
    o_b = gla_scan(q_g, k_g, v_g, log_decay(glb, 1), True, ctx_len)
    o_gla = rms_norm((o_f + o_b).transpose(1, 0, 2), lp["gla_out_norm"]).reshape(n_t, GLA_V_WIDTH)
    o_gla = o_gla * jax.nn.silu(gr)

    o_s5 = _s5_branch(su, lp, ctx_len)

    if not with_ctx_out:
        o_gla, o_s5, bg = o_gla[ctx_len:], o_s5[ctx_len:], bg[ctx_len:]
    g_att, g_gla, g_s5 = split_cols(jax.nn.sigmoid(bg), (D_MODEL,) * N_BRANCH)
    merged = (g_att * linear(o_att, lp["w_br_att"], lp["w_br_att_slot"], FROM_XLA)
              + g_gla * linear(o_gla, lp["w_br_gla"], lp["w_br_gla_slot"], FROM_XLA)
              + g_s5 * linear(o_s5, lp["w_br_s5"], lp["w_br_s5_slot"], FROM_XLA))
    return linear(merged, lp["w_out"], lp["w_out_slot"], FROM_XLA)


def _norm_blocking(n_rows, ctx_rows):
    rb = _pick(math.gcd(ctx_rows, n_rows) if ctx_rows else n_rows, (512, 256, 128, 64, 32, 16, 8))
    return rb, n_rows // rb, ctx_rows // rb


def _segment_rows(per_segment):
    return jnp.broadcast_to(per_segment[:, None, :], (2, 8, per_segment.shape[1]))


def _segment_spec(width, cblk):
    return pl.BlockSpec((1, 8, width), lambda i: (jnp.where(i < cblk, 0, 1), 0, 0))


def _segment_sums(parts, row, cblk):
    return jnp.stack([jnp.sum(parts[:cblk, row], axis=0), jnp.sum(parts[cblk:, row], axis=0)])


def _norm_mod_fwd_call(x, gain, scale2, shift2, ctx_rows):
    n_rows, width = x.shape
    rb, nblk, cblk = _norm_blocking(n_rows, ctx_rows)

    def body(x_ref, g_ref, sc_ref, sh_ref, o_ref):
        xv = x_ref[...]
        inv = lax.rsqrt(jnp.mean(xv * xv, axis=1, keepdims=True) + EPS)
        o_ref[...] = xv * inv * (g_ref[0:1, :] * (1.0 + sc_ref[0, 0:1, :])) + sh_ref[0, 0:1, :]

    rows = pl.BlockSpec((rb, width), lambda i: (i, 0))
    return pl.pallas_call(
        body, name="norm_mod_fwd", grid=(nblk,),
        in_specs=[rows, pl.BlockSpec((8, width), lambda i: (0, 0)), _segment_spec(width, cblk), _segment_spec(width, cblk)],
        out_specs=rows, out_shape=jax.ShapeDtypeStruct((n_rows, width), F32),
        compiler_params=_params(("parallel",)),
    )(x, jnp.broadcast_to(gain[None, :], (8, width)), _segment_rows(scale2), _segment_rows(shift2))


def _norm_mod_bwd_call(x, gain, scale2, dh, ctx_rows):
    n_rows, width = x.shape
    rb, nblk, cblk = _norm_blocking(n_rows, ctx_rows)

    def body(x_ref, g_ref, sc_ref, dh_ref, dx_ref, part_ref):
        xv, dh = x_ref[...], dh_ref[...]
        gain_row, one_plus = g_ref[0:1, :], 1.0 + sc_ref[0, 0:1, :]
        inv = lax.rsqrt(jnp.mean(xv * xv, axis=1, keepdims=True) + EPS)
        xhat = xv * inv
        d_xhat = dh * (gain_row * one_plus)
        dx_ref[...] = inv * (d_xhat - xhat * jnp.mean(d_xhat * xhat, axis=1, keepdims=True))
        dh_xhat = dh * xhat
        part_ref[0] = jnp.concatenate([jnp.sum(dh, axis=0, keepdims=True),
                                       jnp.sum(dh_xhat * gain_row, axis=0, keepdims=True),
                                       jnp.sum(dh_xhat * one_plus, axis=0, keepdims=True),
                                       jnp.zeros((5, width), F32)], axis=0)

    rows = pl.BlockSpec((rb, width), lambda i: (i, 0))
    return pl.pallas_call(
        body, name="norm_mod_bwd", grid=(nblk,),
        in_specs=[rows, pl.BlockSpec((8, width), lambda i: (0, 0)), _segment_spec(width, cblk), rows],
        out_specs=[rows, pl.BlockSpec((1, 8, width), lambda i: (i, 0, 0))],
        out_shape=[jax.ShapeDtypeStruct((n_rows, width), F32), jax.ShapeDtypeStruct((nblk, 8, width), F32)],
        compiler_params=_params(("parallel",)),
    )(x, jnp.broadcast_to(gain[None, :], (8, width)), _segment_rows(scale2), dh)


@functools.partial(jax.custom_vjp, nondiff_argnums=(4,))
def norm_mod(x, gain, scale2, shift2, ctx_rows):
    return _norm_mod_fwd_call(x, gain, scale2, shift2, ctx_rows)


def _norm_mod_fwd(x, gain, scale2, shift2, ctx_rows):
    return _norm_mod_fwd_call(x, gain, scale2, shift2, ctx_rows), (x, gain, scale2)


def _norm_mod_bwd(ctx_rows, res, dh):
    x, gain, scale2 = res
    cblk = _norm_blocking(x.shape[0], ctx_rows)[2]
    dx, parts = _norm_mod_bwd_call(x, gain, scale2, dh, ctx_rows)
    return dx, jnp.sum(parts[:, 2], axis=0), _segment_sums(parts, 1, cblk), _segment_sums(parts, 0, cblk)


norm_mod.defvjp(_norm_mod_fwd, _norm_mod_bwd)


def _resid_norm_fwd_call(x, y, gain, gate2, ctx_rows):
    n_rows, width = x.shape
    rb, nblk, cblk = _norm_blocking(n_rows, ctx_rows)

    def body(x_ref, y_ref, g_ref, gate_ref, o_ref):
        yv = y_ref[...]
        inv = lax.rsqrt(jnp.mean(yv * yv, axis=1, keepdims=True) + EPS)
        o_ref[...] = x_ref[...] + gate_ref[0, 0:1, :] * (yv * inv * g_ref[0:1, :])

    rows = pl.BlockSpec((rb, width), lambda i: (i, 0))
    return pl.pallas_call(
        body, name="resid_norm_fwd", grid=(nblk,),
        in_specs=[rows, rows, pl.BlockSpec((8, width), lambda i: (0, 0)), _segment_spec(width, cblk)],
        out_specs=rows, out_shape=jax.ShapeDtypeStruct((n_rows, width), F32),
        compiler_params=_params(("parallel",)),
    )(x, y, jnp.broadcast_to(gain[None, :], (8, width)), _segment_rows(gate2))


def _resid_norm_bwd_call(y, gain, gate2, d_out, ctx_rows):
    n_rows, width = y.shape
    rb, nblk, cblk = _norm_blocking(n_rows, ctx_rows)

    def body(y_ref, g_ref, gate_ref, do_ref, dy_ref, part_ref):
        yv, do = y_ref[...], do_ref[...]
        gain_row, gate_row = g_ref[0:1, :], gate_ref[0, 0:1, :]
        inv = lax.rsqrt(jnp.mean(yv * yv, axis=1, keepdims=True) + EPS)
        yhat = yv * inv
        d_yhat = do * (gate_row * gain_row)
        dy_ref[...] = inv * (d_yhat - yhat * jnp.mean(d_yhat * yhat, axis=1, keepdims=True))
        do_yhat = do * yhat
        part_ref[0] = jnp.concatenate([jnp.sum(do_yhat * gain_row, axis=0, keepdims=True),
                                       jnp.sum(do_yhat * gate_row, axis=0, keepdims=True),
                                       jnp.zeros((6, width), F32)], axis=0)

    rows = pl.BlockSpec((rb, width), lambda i: (i, 0))
    return pl.pallas_call(
        body, name="resid_norm_bwd", grid=(nblk,),
        in_specs=[rows, pl.BlockSpec((8, width), lambda i: (0, 0)), _segment_spec(width, cblk), rows],
        out_specs=[rows, pl.BlockSpec((1, 8, width), lambda i: (i, 0, 0))],
        out_shape=[jax.ShapeDtypeStruct((n_rows, width), F32), jax.ShapeDtypeStruct((nblk, 8, width), F32)],
        compiler_params=_params(("parallel",)),
    )(y, jnp.broadcast_to(gain[None, :], (8, width)), _segment_rows(gate2), d_out)


@functools.partial(jax.custom_vjp, nondiff_argnums=(4,))
def resid_norm(x, y, gain, gate2, ctx_rows):
    return _resid_norm_fwd_call(x, y, gain, gate2, ctx_rows)


def _resid_norm_fwd(x, y, gain, gate2, ctx_rows):
    return _resid_norm_fwd_call(x, y, gain, gate2, ctx_rows), (y, gain, gate2)


def _resid_norm_bwd(ctx_rows, res, d_out):
    y, gain, gate2 = res
    cblk = _norm_blocking(y.shape[0], ctx_rows)[2]
    dy, parts = _resid_norm_bwd_call(y, gain, gate2, d_out, ctx_rows)
    return d_out, dy, jnp.sum(parts[:, 1], axis=0), _segment_sums(parts, 0, cblk)


resid_norm.defvjp(_resid_norm_fwd, _resid_norm_bwd)


HALO = 8


def _ffn_mid_blocking(n_rows, half):
    return _pick(n_rows, (1056, 1024, 256, 128, 64, 32, 16, 8)), _pick(half, (256, 128))


def _ffn_mid_specs(n_rows, rb, tc):
    per = rb // HALO
    return [pl.BlockSpec((rb, tc), lambda j, i: (i, j)),
            pl.BlockSpec((HALO, tc), lambda j, i: (jnp.maximum(i * per - 1, 0), j)),
            pl.BlockSpec((HALO, tc), lambda j, i: (jnp.minimum((i + 1) * per, n_rows // HALO - 1), j))]


def _with_halo(main_ref, prev_ref, next_ref):
    return jnp.concatenate([prev_ref[...], main_ref[...], next_ref[...]], axis=0)


def _row_neighbours(ext, first_row, n_rows, starts):
    n = ext.shape[0]
    row = lax.broadcasted_iota(jnp.int32, ext.shape, 0) + first_row
    first = functools.reduce(jnp.logical_or, [row == s for s in starts])
    last = functools.reduce(jnp.logical_or, [row == e - 1 for e in tuple(starts[1:]) + (n_rows,)])
    return jnp.where(first, 0.0, pltpu.roll(ext, 1, 0)), jnp.where(last, 0.0, pltpu.roll(ext, n - 1, 0))


def _ffn_mid_fwd_call(u_a, u_v, taps, starts):
    n_rows, half = u_a.shape
    rb, tc = _ffn_mid_blocking(n_rows, half)
    specs = _ffn_mid_specs(n_rows, rb, tc)
    nj = half // tc

    def body(am, ap, an, vm, vp, vn, wa, wv, o_ref):
        first_row = pl.program_id(1) * rb - HALO

        def conv(ext, w):
            above, below = _row_neighbours(ext, first_row, n_rows, starts)
            return above * w[0:1, :] + ext * w[1:2, :] + below * w[2:3, :] + w[3:4, :]

        ca = conv(_with_halo(am, ap, an), wa)[HALO:HALO + rb]
        cv = conv(_with_halo(vm, vp, vn), wv)[HALO:HALO + rb]
        o_ref[...] = ca * jax.nn.sigmoid(ca) * cv

    return pl.pallas_call(
        body, name="ffn_mid_fwd", grid=(nj, n_rows // rb),
        in_specs=specs + specs + [pl.BlockSpec((8, tc), lambda j, i: (0, j)),
                                  pl.BlockSpec((8, tc), lambda j, i: (0, nj + j))],
        out_specs=pl.BlockSpec((rb, tc), lambda j, i: (i, j)),
        out_shape=jax.ShapeDtypeStruct((n_rows, half), F32),
        compiler_params=_params(("parallel", "parallel")),
    )(u_a, u_a, u_a, u_v, u_v, u_v, taps, taps)


def _ffn_mid_bwd_call(u_a, u_v, taps, d_act, starts):
    n_rows, half = u_a.shape
    rb, tc = _ffn_mid_blocking(n_rows, half)
    specs = _ffn_mid_specs(n_rows, rb, tc)
    nj = half // tc
    main = slice(HALO, HALO + rb)

    def body(am, ap, an, vm, vp, vn, wa, wv, dm, dp, dn, dua_ref, duv_ref, dwa_ref, dwv_ref):
        i = pl.program_id(1)
        first_row = i * rb - HALO
        neighbours = functools.partial(_row_neighbours, first_row=first_row, n_rows=n_rows, starts=starts)
        ext_a, ext_v, ext_d = _with_halo(am, ap, an), _with_halo(vm, vp, vn), _with_halo(dm, dp, dn)
        above_a, below_a = neighbours(ext_a)
        above_v, below_v = neighbours(ext_v)
        ca = above_a * wa[0:1, :] + ext_a * wa[1:2, :] + below_a * wa[2:3, :] + wa[3:4, :]
        cv = above_v * wv[0:1, :] + ext_v * wv[1:2, :] + below_v * wv[2:3, :] + wv[3:4, :]
        sig = jax.nn.sigmoid(ca)
        d_cv = ext_d * (ca * sig)
        d_ca = ext_d * cv * (sig * (1.0 + ca * (1.0 - sig)))

        def finish(d_c, above, ext, below, w, du_ref, dw_ref):
            d_above, d_below = neighbours(d_c)
            du_ref[...] = (w[1:2, :] * d_c + w[0:1, :] * d_below + w[2:3, :] * d_above)[main]
            d_main = d_c[main]
            sums = jnp.concatenate([jnp.sum(above[main] * d_main, axis=0, keepdims=True),
                                    jnp.sum(ext[main] * d_main, axis=0, keepdims=True),
                                    jnp.sum(below[main] * d_main, axis=0, keepdims=True),
                                    jnp.sum(d_main, axis=0, keepdims=True), jnp.zeros((4, tc), F32)], axis=0)

            @pl.when(i == 0)
            def _():
                dw_ref[...] = sums

            @pl.when(i > 0)
            def _():
                dw_ref[...] += sums

        finish(d_ca, above_a, ext_a, below_a, wa, dua_ref, dwa_ref)
        finish(d_cv, above_v, ext_v, below_v, wv, duv_ref, dwv_ref)

    block = pl.BlockSpec((rb, tc), lambda j, i: (i, j))
    taps_out = pl.BlockSpec((8, tc), lambda j, i: (0, j))
    return pl.pallas_call(
        body, name="ffn_mid_bwd", grid=(nj, n_rows // rb),
        in_specs=specs + specs + [pl.BlockSpec((8, tc), lambda j, i: (0, j)),
                                  pl.BlockSpec((8, tc), lambda j, i: (0, nj + j))] + specs,
        out_specs=[block, block, taps_out, taps_out],
        out_shape=[jax.ShapeDtypeStruct((n_rows, half), F32)] * 2 + [jax.ShapeDtypeStruct((8, half), F32)] * 2,
        compiler_params=_params(("parallel", "arbitrary")),
    )(u_a, u_a, u_a, u_v, u_v, u_v, taps, taps, d_act, d_act, d_act)


def _taps(conv_w, conv_b):
    return jnp.concatenate([conv_w, conv_b[None, :], jnp.zeros((4, conv_w.shape[1]), F32)], axis=0)


@functools.partial(jax.custom_vjp, nondiff_argnums=(4,))
def ffn_mid(u_a, u_v, conv_w, conv_b, starts):
    return _ffn_mid_fwd_call(u_a, u_v, _taps(conv_w, conv_b), starts)


def _ffn_mid_fwd(u_a, u_v, conv_w, conv_b, starts):
    return _ffn_mid_fwd_call(u_a, u_v, _taps(conv_w, conv_b), starts), (u_a, u_v, conv_w, conv_b)


def _ffn_mid_bwd(starts, res, d_act):
    u_a, u_v, conv_w, conv_b = res
    du_a, du_v, dw_a, dw_v = _ffn_mid_bwd_call(u_a, u_v, _taps(conv_w, conv_b), d_act, starts)
    d_taps = jnp.concatenate([dw_a, dw_v], axis=1)
    return du_a, du_v, d_taps[0:3], d_taps[3]


ffn_mid.defvjp(_ffn_mid_fwd, _ffn_mid_bwd)


def _conv_ffn(h, lp, starts):
    u_a = linear(h, lp["ffn_up"], lp["ffn_up_slot"], (True, True))
    u_v = linear(h, lp["ffn_up_v"], lp["ffn_up_v_slot"], (True, True))
    act = ffn_mid(u_a, u_v, lp["ffn_conv_w"], lp["ffn_conv_b"], tuple(starts))
    return linear(act, lp["ffn_down"], lp["ffn_down_slot"], (True, False))


def _local_loss(diff, fixed):
    p = {**fixed, **diff}
    x, ctx = p["x"][0], p["ctx"][0]
    n_lat, ctx_len = x.shape[0], ctx.shape[0]
    depth = p["ada_b"].shape[0]
    rope = _rope_tables(n_lat)
    rows = jnp.concatenate([ctx, x], axis=0)
    cond = jnp.zeros((16, D_MODEL), F32).at[0].set(jax.nn.silu(p["c"][0])).at[1].set(jax.nn.silu(p["c_ctx"]))
    layer_names = [n for n in WEIGHTS if n != "c_ctx"]
    for i in range(depth):
        last = i == depth - 1
        lp = {n: p[n][i] for n in layer_names}
        lp["ffn_up_v"] = p["ffn_up_v"][i]
        lp.update({n + "_slot": p[n + "_slot"][i] for n in MATMUL_WEIGHTS})
        mod = linear(cond, lp["ada_w"], lp["ada_w_slot"], FROM_XLA) + lp["ada_b"]
        m_lat, m_ctx = jnp.split(mod[0:1], 6, axis=-1), jnp.split(mod[1:2], 6, axis=-1)
        seg = [jnp.concatenate([m_ctx[k], m_lat[k]], axis=0) for k in range(6)]

        h = norm_mod(rows, lp["norm_mix_pre"], seg[1], seg[0], ctx_len)
        y = _token_mixer(h, lp, rope, ctx_len, not last)
        if last:
            rows = rows[ctx_len:]
        ctx_rows = 0 if last else ctx_len
        starts = [0] if last else [0, ctx_len]
        rows = resid_norm(rows, y, lp["norm_mix_post"], seg[2], ctx_rows)
        h = norm_mod(rows, lp["norm_ffn_pre"], seg[4], seg[3], ctx_rows)
        rows = resid_norm(rows, _conv_ffn(h, lp, starts), lp["norm_ffn_post"], seg[5], ctx_rows)
    err = jnp.square(rows - p["loss_target"][0])
    return 0.5 * jnp.sum(jnp.mean(err, axis=-1))


def kernel(x, c, ctx, c_ctx, ada_w, ada_b, norm_mix_pre, norm_mix_post, norm_ffn_pre, norm_ffn_post, w_in, q_norm, k_norm, gla_gate_w, gla_gate_b, gla_out_norm, s5_a_re, s5_a_im, s5_log_dt, s5_b_re, s5_b_im, s5_c_re, s5_c_im, s5_d, s5_glu_w, s5_glu_b, w_br_att, w_br_gla, w_br_s5, w_out, ffn_up, ffn_conv_w, ffn_conv_b, ffn_down, loss_target, m_c_ctx, m_ada_w, m_ada_b, m_norm_mix_pre, m_norm_mix_post, m_norm_ffn_pre, m_norm_ffn_post, m_w_in, m_q_norm, m_k_norm, m_gla_gate_w, m_gla_gate_b, m_gla_out_norm, m_s5_a_re, m_s5_a_im, m_s5_log_dt, m_s5_b_re, m_s5_b_im, m_s5_c_re, m_s5_c_im, m_s5_d, m_s5_glu_w, m_s5_glu_b, m_w_br_att, m_w_br_gla, m_w_br_s5, m_w_out, m_ffn_up, m_ffn_conv_w, m_ffn_conv_b, m_ffn_down, v_c_ctx, v_ada_w, v_ada_b, v_norm_mix_pre, v_norm_mix_post, v_norm_ffn_pre, v_norm_ffn_post, v_w_in, v_q_norm, v_k_norm, v_gla_gate_w, v_gla_gate_b, v_gla_out_norm, v_s5_a_re, v_s5_a_im, v_s5_log_dt, v_s5_b_re, v_s5_b_im, v_s5_c_re, v_s5_c_im, v_s5_d, v_s5_glu_w, v_s5_glu_b, v_w_br_att, v_w_br_gla, v_w_br_s5, v_w_out, v_ffn_up, v_ffn_conv_w, v_ffn_conv_b, v_ffn_down):
    args = (x, c, ctx, c_ctx, ada_w, ada_b, norm_mix_pre, norm_mix_post, norm_ffn_pre, norm_ffn_post, w_in, q_norm, k_norm, gla_gate_w, gla_gate_b, gla_out_norm, s5_a_re, s5_a_im, s5_log_dt, s5_b_re, s5_b_im, s5_c_re, s5_c_im, s5_d, s5_glu_w, s5_glu_b, w_br_att, w_br_gla, w_br_s5, w_out, ffn_up, ffn_conv_w, ffn_conv_b, ffn_down)
    given = dict(zip(FWD_INPUTS, args))
    given["loss_target"] = loss_target
    m_in = dict(zip(WEIGHTS, (m_c_ctx, m_ada_w, m_ada_b, m_norm_mix_pre, m_norm_mix_post, m_norm_ffn_pre, m_norm_ffn_post, m_w_in, m_q_norm, m_k_norm, m_gla_gate_w, m_gla_gate_b, m_gla_out_norm, m_s5_a_re, m_s5_a_im, m_s5_log_dt, m_s5_b_re, m_s5_b_im, m_s5_c_re, m_s5_c_im, m_s5_d, m_s5_glu_w, m_s5_glu_b, m_w_br_att, m_w_br_gla, m_w_br_s5, m_w_out, m_ffn_up, m_ffn_conv_w, m_ffn_conv_b, m_ffn_down)))
    v_in = dict(zip(WEIGHTS, (v_c_ctx, v_ada_w, v_ada_b, v_norm_mix_pre, v_norm_mix_post, v_norm_ffn_pre, v_norm_ffn_post, v_w_in, v_q_norm, v_k_norm, v_gla_gate_w, v_gla_gate_b, v_gla_out_norm, v_s5_a_re, v_s5_a_im, v_s5_log_dt, v_s5_b_re, v_s5_b_im, v_s5_c_re, v_s5_c_im, v_s5_d, v_s5_glu_w, v_s5_glu_b, v_w_br_att, v_w_br_gla, v_w_br_s5, v_w_out, v_ffn_up, v_ffn_conv_w, v_ffn_conv_b, v_ffn_down)))
    depth = ada_b.shape[0]
    big_names, small_names = list(BIG), list(SMALL_SHARDED)
    sharded_names = big_names + small_names

    full = {}
    groups = {}
    for n in big_names:
        groups.setdefault(_round_up(given[n].shape[2], 128), []).append(n)
    per_layer = {n: [] for n in big_names}
    for i in range(depth):
        for width, names in groups.items():
            gathered = all_gather_blocks(_pack_rows([given[n][i] for n in names], width, BF16),
                                         f"gather_weights_{width}")
            for n, blocks in zip(names, _unpack_rows(gathered, [given[n].shape[1:] for n in names])):
                per_layer[n].append(_to_full(blocks, BIG[n]))
    for n in big_names:
        full[n] = jnp.stack(per_layer[n])
    small_shapes = [given[n].shape for n in small_names]
    gathered = all_gather_blocks(_pack([given[n] for n in small_names], F32), "gather_small")
    for n, blocks in zip(small_names, _unpack(gathered, small_shapes)):
        full[n] = _to_full(blocks, SMALL_SHARDED[n] + 1)

    w_in_full = jnp.pad(_w_in_reorder(full["w_in"], True), ((0, 0), (0, 0), (0, D_IN_PAD - D_IN)))

    diff = {"x": x}
    diff.update({n: given[n] for n in REPLICATED})
    diff.update({n: full[n] for n in small_names})
    fixed = {"c": c, "ctx": ctx, "loss_target": loss_target}
    d_ff = full["ffn_up"].shape[2] // 2
    for n in big_names:
        fixed[n] = w_in_full if n == "w_in" else full[n]
    fixed["ffn_up"], fixed["ffn_up_v"] = full["ffn_up"][:, :, :d_ff], full["ffn_up"][:, :, d_ff:]
    for n in MATMUL_WEIGHTS:
        diff[n + "_slot"] = jnp.zeros(fixed[n].shape, F32)
    loss_local, grads = jax.value_and_grad(_local_loss)(diff, fixed)
    loss = lax.psum(loss_local, ("x", "y", "c"))

    g_full = {n: grads[n] for n in small_names}
    for n in big_names:
        g_full[n] = grads[n + "_slot"]
    g_full["ffn_up"] = jnp.concatenate([grads["ffn_up_slot"], grads["ffn_up_v_slot"]], axis=2)
    g_full["w_in"] = _w_in_reorder(g_full["w_in"], False)

    out_g, out_d, out_m, out_v = {}, {}, {}, {}
    layer_out = {n: ([], [], [], []) for n in big_names}
    for i in range(depth):
        for width, names in groups.items():
            send = _pack_rows([_to_blocks(g_full[n][i], BIG[n]) for n in names], width, BF16)
            parts = exchange_blocks(send, f"exchange_grads_{width}")
            results = sum_adamw(parts, _pack_rows([given[n][i] for n in names], width, F32),
                                _pack_rows([m_in[n][i] for n in names], width, F32),
                                _pack_rows([v_in[n][i] for n in names], width, F32), f"adamw_{width}")
            for k, res in enumerate(results):
                for n, arr in zip(names, _unpack_rows(res, [given[n].shape[1:] for n in names])):
                    layer_out[n][k].append(arr)
    for n in big_names:
        out_g[n], out_d[n], out_m[n], out_v[n] = (jnp.stack(parts_k) for parts_k in layer_out[n])
    send = _pack_blocks([_to_blocks(g_full[n], SMALL_SHARDED[n] + 1) for n in small_names], F32)
    parts = exchange_blocks(send, "exchange_small")
    results = sum_adamw(parts, _pack([given[n] for n in small_names], F32), _pack([m_in[n] for n in small_names], F32),
                        _pack([v_in[n] for n in small_names], F32), "adamw_small")
    for store, res in zip((out_g, out_d, out_m, out_v), results):
        for n, arr in zip(small_names, _unpack(res, small_shapes)):
            store[n] = arr

    rep_shapes = [given[n].shape for n in REPLICATED]
    parts = all_gather_blocks(_pack([grads[n] for n in REPLICATED], F32), "gather_rep_grads")
    results = sum_adamw(parts, _pack([given[n] for n in REPLICATED], F32), _pack([m_in[n] for n in REPLICATED], F32),
                        _pack([v_in[n] for n in REPLICATED], F32), "adamw_replicated")
    for store, res in zip((out_g, out_d, out_m, out_v), results):
        for n, arr in zip(REPLICATED, _unpack(res, rep_shapes)):
            store[n] = arr

    return (loss, grads["x"], *[out_g[n] for n in WEIGHTS], *[out_d[n] for n in WEIGHTS],
            *[out_m[n] for n in WEIGHTS], *[out_v[n] for n in WEIGHTS])
```

```python
import functools
import math

import numpy as np
import jax
import jax.numpy as jnp
from jax import lax
from jax.experimental import pallas as pl
from jax.experimental.pallas import tpu as pltpu

F32 = jnp.float32
BF16 = jnp.bfloat16
MESH = pl.DeviceIdType.MESH
N_DEV = 8

D_MODEL = 1024
GRID_W = 64
ATT_HEADS, ATT_KV_HEADS, ATT_HEAD_DIM = 4, 2, 64
ATT_WIDTH, ATT_KV_WIDTH = 256, 128
ROPE_THETA = 10000.0
GLA_HEADS, GLA_DK, GLA_DV = 4, 64, 128
GLA_K_WIDTH, GLA_V_WIDTH = 256, 512
GLA_GATE_RANK, GLA_GATE_NORM, GLA_CHUNK = 16, 16.0, 64
S5_GROUPS, S5_GROUP_CH, S5_WIDTH, S5_STATE = 16, 16, 256, 64
S5_FLAT = S5_GROUPS * S5_STATE
N_BRANCH = 3
EPS = 1e-6
IN_SPLITS = (ATT_WIDTH, ATT_KV_WIDTH, ATT_KV_WIDTH, GLA_K_WIDTH, GLA_K_WIDTH, GLA_V_WIDTH, GLA_V_WIDTH,
             GLA_GATE_RANK, GLA_GATE_RANK, S5_WIDTH, N_BRANCH * D_MODEL)
D_IN = sum(IN_SPLITS)
D_IN_PAD = 5632

ADAM_LR, ADAM_B1, ADAM_B2, ADAM_EPS, ADAM_WD, ADAM_STEP = 0.001, 0.9, 0.999, 1e-08, 0.01, 10

VMEM_LIMIT = 56 * 1024 * 1024
MM_VMEM_BUDGET = 40 * 1024 * 1024
PACK_W = 512
PACK_UNIT = 16 * PACK_W

WEIGHTS = ['c_ctx', 'ada_w', 'ada_b', 'norm_mix_pre', 'norm_mix_post', 'norm_ffn_pre', 'norm_ffn_post', 'w_in',
           'q_norm', 'k_norm', 'gla_gate_w', 'gla_gate_b', 'gla_out_norm', 's5_a_re', 's5_a_im', 's5_log_dt',
           's5_b_re', 's5_b_im', 's5_c_re', 's5_c_im', 's5_d', 's5_glu_w', 's5_glu_b', 'w_br_att', 'w_br_gla',
           'w_br_s5', 'w_out', 'ffn_up', 'ffn_conv_w', 'ffn_conv_b', 'ffn_down']
FWD_INPUTS = ['x', 'c', 'ctx'] + WEIGHTS
BIG = {'ada_w': 1, 'w_in': 1, 'w_br_att': 1, 'w_br_gla': 1, 'w_br_s5': 1, 'w_out': 0, 'ffn_up': 1, 'ffn_down': 0}
SMALL_SHARDED = {'gla_gate_w': 2, 'gla_gate_b': 1, 's5_glu_w': 0, 'ffn_conv_w': 1}
SHARDED = {**BIG, **SMALL_SHARDED}
MATMUL_WEIGHTS = list(BIG) + ['ffn_up_v']
REPLICATED = [n for n in WEIGHTS if n not in SHARDED]


def _pick(n, cands):
    for cand in cands:
        if n % cand == 0:
            return cand
    return n


def _params(sem):
    return pltpu.CompilerParams(dimension_semantics=sem, vmem_limit_bytes=VMEM_LIMIT)


_DIMS = {"nn": ((1,), (0,)), "nt": ((1,), (1,)), "tn": ((0,), (0,))}


def _mm(a, b, mode, name):
    if mode == "tn":
        K, M = a.shape
    else:
        M, K = a.shape
    N = b.shape[0] if mode == "nt" else b.shape[1]
    tm = _pick(M, (1408, 1024, 768, 512, 256, 128))
    tn = _pick(N, (1408, 1024, 512, 256, 128))
    tk_options = [K] + [t for t in (2816, 1408, 1024, 768, 512, 256, 128) if t < K and K % t == 0]
    for tk in tk_options:
        blocks = tm * tk * a.dtype.itemsize + tk * tn * b.dtype.itemsize + tm * tn * 4
        if 2 * blocks <= MM_VMEM_BUDGET:
            break
    nk = K // tk
    dims = (_DIMS[mode], ((), ()))

    def body(a_ref, b_ref, o_ref):
        acc = lax.dot_general(a_ref[...].astype(BF16), b_ref[...].astype(BF16), dims, preferred_element_type=F32)
        if nk == 1:
            o_ref[...] = acc
        else:
            k = pl.program_id(2)

            @pl.when(k == 0)
            def _():
                o_ref[...] = acc

            @pl.when(k > 0)
            def _():
                o_ref[...] += acc

    a_spec = (pl.BlockSpec((tk, tm), lambda i, j, k: (k, i)) if mode == "tn"
              else pl.BlockSpec((tm, tk), lambda i, j, k: (i, k)))
    b_spec = (pl.BlockSpec((tn, tk), lambda i, j, k: (j, k)) if mode == "nt"
              else pl.BlockSpec((tk, tn), lambda i, j, k: (k, j)))
    return pl.pallas_call(
        body, name=name, grid=(M // tm, N // tn, nk),
        in_specs=[a_spec, b_spec], out_specs=pl.BlockSpec((tm, tn), lambda i, j, k: (i, j)),
        out_shape=jax.ShapeDtypeStruct((M, N), F32),
        compiler_params=_params(("parallel", "parallel", "arbitrary")),
    )(a, b)


FROM_XLA = (False, False)


def _rounded(x, from_kernel):
    return x if from_kernel else x.astype(BF16)


@functools.partial(jax.custom_vjp, nondiff_argnums=(2,))
def matmul(a, w, from_kernel=FROM_XLA):
    return _mm(_rounded(a, from_kernel[0]), w.astype(BF16), "nn", "mm_fwd")


def _matmul_fwd(a, w, from_kernel):
    ab, wb = _rounded(a, from_kernel[0]), w.astype(BF16)
    return _mm(ab, wb, "nn", "mm_fwd"), (ab, wb)


def _matmul_bwd(from_kernel, res, dy):
    ab, wb = res
    dyb = _rounded(dy, from_kernel[1])
    return _mm(dyb, wb, "nt", "mm_dx"), _mm(ab, dyb, "tn", "mm_dw")


matmul.defvjp(_matmul_fwd, _matmul_bwd)


@functools.partial(jax.custom_vjp, nondiff_argnums=(3,))
def linear(a, w, w_grad_slot, from_kernel=FROM_XLA):
    del w_grad_slot
    return _mm(_rounded(a, from_kernel[0]), w, "nn", "lin_fwd")


def _linear_fwd(a, w, w_grad_slot, from_kernel):
    del w_grad_slot
    ab = _rounded(a, from_kernel[0])
    return _mm(ab, w, "nn", "lin_fwd"), (ab, w)


def _linear_bwd(from_kernel, res, dy):
    ab, w = res
    dyb = _rounded(dy, from_kernel[1])
    return _mm(dyb, w, "nt", "lin_dx"), jnp.zeros_like(w), _mm(ab, dyb, "tn", "lin_dw")


linear.defvjp(_linear_fwd, _linear_bwd)


def _leading_cols(x, sizes):
    points = np.cumsum((0,) + tuple(sizes))
    return tuple(x[:, int(lo):int(hi)] for lo, hi in zip(points[:-1], points[1:]))


@functools.partial(jax.custom_vjp, nondiff_argnums=(3, 4))
def linear_split(a, w, w_grad_slot, sizes, from_kernel):
    del w_grad_slot
    return _leading_cols(_mm(_rounded(a, from_kernel[0]), w, "nn", "lin_fwd"), sizes)


def _linear_split_fwd(a, w, w_grad_slot, sizes, from_kernel):
    del w_grad_slot
    ab = _rounded(a, from_kernel[0])
    return _leading_cols(_mm(ab, w, "nn", "lin_fwd"), sizes), (ab, w)


def _linear_split_bwd(sizes, from_kernel, res, cts):
    del from_kernel
    ab, w = res
    parts = [ct.astype(BF16) for ct in cts]
    if w.shape[1] > sum(sizes):
        parts.append(jnp.zeros((cts[0].shape[0], w.shape[1] - sum(sizes)), BF16))
    dyb = jnp.concatenate(parts, axis=1)
    return _mm(dyb, w, "nt", "lin_dx"), jnp.zeros_like(w), _mm(ab, dyb, "tn", "lin_dw")


linear_split.defvjp(_linear_split_fwd, _linear_split_bwd)


def _attn_fwd_call(q, k, v):
    H, Tq, d = q.shape
    KV, Tk, _ = k.shape
    G = H // KV
    tq = _pick(Tq, (256, 128, 64))

    def body(q_ref, k_ref, v_ref, o_ref, lse_ref):
        s = lax.dot_general(q_ref[0], k_ref[0], (_DIMS["nt"], ((), ())), preferred_element_type=F32)
        m = jnp.max(s, axis=1, keepdims=True)
        p = jnp.exp(s - m)
        l = jnp.sum(p, axis=1, keepdims=True)
        o_ref[0] = jnp.dot(p.astype(BF16), v_ref[0], preferred_element_type=F32) * (1.0 / l)
        lse_ref[0] = m + jnp.log(l)

    return pl.pallas_call(
        body, name="attn_fwd", grid=(H, Tq // tq),
        in_specs=[pl.BlockSpec((1, tq, d), lambda h, i: (h, i, 0)),
                  pl.BlockSpec((1, Tk, d), lambda h, i: (h // G, 0, 0)),
                  pl.BlockSpec((1, Tk, d), lambda h, i: (h // G, 0, 0))],
        out_specs=[pl.BlockSpec((1, tq, d), lambda h, i: (h, i, 0)),
                   pl.BlockSpec((1, tq, 1), lambda h, i: (h, i, 0))],
        out_shape=[jax.ShapeDtypeStruct((H, Tq, d), F32), jax.ShapeDtypeStruct((H, Tq, 1), F32)],
        compiler_params=_params(("parallel", "parallel")),
    )(q, k, v)


def _attn_bwd_call(q, k, v, o, lse, do, scale):
    H, Tq, d = q.shape
    KV, Tk, _ = k.shape
    G = H // KV
    tq = _pick(Tq, (256, 128, 64))
    ck = _pick(Tk, (1408, 1024, 512, 256, 128, 64))
    nck = Tk // ck

    def body(q_ref, k_ref, v_ref, o_ref, lse_ref, do_ref, dq_ref, dk_ref, dv_ref):
        @pl.when((pl.program_id(1) == 0) & (pl.program_id(2) == 0))
        def _():
            dk_ref[...] = jnp.zeros_like(dk_ref)
            dv_ref[...] = jnp.zeros_like(dv_ref)

        qb = q_ref[0]
        do = do_ref[0]
        dob = do.astype(BF16)
        delta = jnp.sum(do * o_ref[0], axis=1, keepdims=True)
        lse = lse_ref[0]
        dq = jnp.zeros((tq, d), F32)
        for cidx in range(nck):
            rows = slice(cidx * ck, (cidx + 1) * ck)
            ks = k_ref[0, rows, :]
            vs = v_ref[0, rows, :]
            s = lax.dot_general(qb, ks, (_DIMS["nt"], ((), ())), preferred_element_type=F32)
            p = jnp.exp(s - lse)
            dv_ref[0, rows, :] += lax.dot_general(p.astype(BF16), dob, (_DIMS["tn"], ((), ())),
                                                  preferred_element_type=F32)
            dp = lax.dot_general(dob, vs, (_DIMS["nt"], ((), ())), preferred_element_type=F32)
            dsb = (p * (dp - delta)).astype(BF16)
            dq = dq + jnp.dot(dsb, ks, preferred_element_type=F32)
            dk_ref[0, rows, :] += lax.dot_general(dsb, qb, (_DIMS["tn"], ((), ())), preferred_element_type=F32)
        dq_ref[0] = dq * scale

    q_spec = pl.BlockSpec((1, tq, d), lambda kv, g, i: (kv * G + g, i, 0))
    kv_spec = pl.BlockSpec((1, Tk, d), lambda kv, g, i: (kv, 0, 0))
    return pl.pallas_call(
        body, name="attn_bwd", grid=(KV, G, Tq // tq),
        in_specs=[q_spec, kv_spec, kv_spec, q_spec,
                  pl.BlockSpec((1, tq, 1), lambda kv, g, i: (kv * G + g, i, 0)), q_spec],
        out_specs=[q_spec, kv_spec, kv_spec],
        out_shape=[jax.ShapeDtypeStruct((H, Tq, d), F32), jax.ShapeDtypeStruct((KV, Tk, d), F32),
                   jax.ShapeDtypeStruct((KV, Tk, d), F32)],
        compiler_params=_params(("arbitrary", "arbitrary", "arbitrary")),
    )(q, k, v, o, lse, do)


ATT_SCALE = ATT_HEAD_DIM ** -0.5


@jax.custom_vjp
def attention(q, k, v):
    return _attn_fwd_call((q * ATT_SCALE).astype(BF16), k.astype(BF16), v.astype(BF16))[0]


def _attention_fwd(q, k, v):
    qb, kb, vb = (q * ATT_SCALE).astype(BF16), k.astype(BF16), v.astype(BF16)
    o, lse = _attn_fwd_call(qb, kb, vb)
    return o, (qb, kb, vb, o, lse)


def _attention_bwd(res, do):
    qb, kb, vb, o, lse = res
    return _attn_bwd_call(qb, kb, vb, o, lse, do, ATT_SCALE)


attention.defvjp(_attention_fwd, _attention_bwd)


_ORDER_DOWN = {0: False, 1: True, 2: True, 3: False}
_ORDER_ADJOINT = {0: 2, 1: 3}


def _scan_tables(a_re, a_im, down):
    pw_re, pw_im = [a_re], [a_im]
    for _ in range(7):
        pw_re, pw_im = (pw_re + [pw_re[-1] * a_re - pw_im[-1] * a_im],
                        pw_im + [pw_re[-1] * a_im + pw_im[-1] * a_re])
    carry_rows = list(range(7, -1, -1)) if down else list(range(8))
    rows_re = [pw_re[r] for r in carry_rows] + [pw_re[0], pw_re[1], pw_re[3]]
    rows_im = [pw_im[r] for r in carry_rows] + [pw_im[0], pw_im[1], pw_im[3]]
    tab = jnp.concatenate([jnp.stack(rows_re), jnp.stack(rows_im)], axis=1)
    return jnp.concatenate([tab, jnp.zeros((5, 2 * S5_FLAT), F32)], axis=0)


def _scan_call(bu, a_re, a_im, order, ctx_len):
    T, W2 = bu.shape
    P = W2 // 2
    rb = _pick(math.gcd(ctx_len, T - ctx_len), (256, 128, 64, 32, 16, 8))
    nblk, cb = T // rb, ctx_len // rb
    ntile = rb // 8
    down = _ORDER_DOWN[order]
    tab = _scan_tables(a_re, a_im, down)

    def blk(n):
        return _block_in_order(n, nblk, cb, order)

    def body(bu_ref, tab_ref, s_ref, carry_ref):
        @pl.when(pl.program_id(0) == 0)
        def _():
            carry_ref[...] = jnp.zeros_like(carry_ref)

        row = lax.broadcasted_iota(jnp.int32, (8, P), 0)
        cp_re, cp_im = tab_ref[0:8, 0:P], tab_ref[0:8, P:2 * P]
        steps = []
        for j, sh in enumerate((1, 2, 4)):
            keep = (row < 8 - sh) if down else (row >= sh)
            steps.append((8 - sh if down else sh, keep, tab_ref[8 + j:9 + j, 0:P], tab_ref[8 + j:9 + j, P:2 * P]))

        def tile(j, carry):
            c_re, c_im = carry
            t = (ntile - 1 - j) if down else j
            r0 = pl.multiple_of(t * 8, 8)
            x_re = bu_ref[pl.ds(r0, 8), 0:P]
            x_im = bu_ref[pl.ds(r0, 8), P:2 * P]
            for shift, keep, p_re, p_im in steps:
                y_re = jnp.where(keep, pltpu.roll(x_re, shift, 0), 0.0)
                y_im = jnp.where(keep, pltpu.roll(x_im, shift, 0), 0.0)
                x_re, x_im = x_re + p_re * y_re - p_im * y_im, x_im + p_re * y_im + p_im * y_re
            x_re, x_im = x_re + cp_re * c_re - cp_im * c_im, x_im + cp_re * c_im + cp_im * c_re
            s_ref[pl.ds(r0, 8), 0:P] = x_re
            s_ref[pl.ds(r0, 8), P:2 * P] = x_im
            last = 0 if down else 7
            return x_re[last:last + 1, :], x_im[last:last + 1, :]

        c_re, c_im = lax.fori_loop(0, ntile, tile, (carry_ref[0:1, 0:P], carry_ref[0:1, P:2 * P]))
        carry_ref[0:1, 0:P] = c_re
        carry_ref[0:1, P:2 * P] = c_im

    return pl.pallas_call(
        body, name=f"s5_scan_{order}", grid=(nblk,),
        in_specs=[pl.BlockSpec((rb, W2), lambda n: (blk(n), 0)), pl.BlockSpec((16, W2), lambda n: (0, 0))],
        out_specs=pl.BlockSpec((rb, W2), lambda n: (blk(n), 0)),
        out_shape=jax.ShapeDtypeStruct((T, W2), F32),
        scratch_shapes=[pltpu.VMEM((8, W2), F32)],
        compiler_params=_params(("arbitrary",)),
    )(bu, tab)


def _prev_in_order(s, order, ctx_len):
    zero = jnp.zeros_like(s[:1])
    if order == 0:
        return jnp.concatenate([zero, s[:-1]], axis=0)
    return jnp.concatenate([s[1:ctx_len], zero, s[ctx_len + 1:], s[:1]], axis=0)


@functools.partial(jax.custom_vjp, nondiff_argnums=(3, 4))
def s5_scan(bu, a_re, a_im, order, ctx_len):
    return _scan_call(bu, a_re, a_im, order, ctx_len)


def _s5_scan_fwd(bu, a_re, a_im, order, ctx_len):
    s = _scan_call(bu, a_re, a_im, order, ctx_len)
    return s, (s, a_re, a_im)


def _s5_scan_bwd(order, ctx_len, res, ds):
    s, a_re, a_im = res
    lam = _scan_call(ds, a_re, -a_im, _ORDER_ADJOINT[order], ctx_len)
    P = a_re.shape[0]
    sp = _prev_in_order(s, order, ctx_len)
    l_re, l_im, p_re, p_im = lam[:, :P], lam[:, P:], sp[:, :P], sp[:, P:]
    g_re = jnp.sum(l_re * p_re + l_im * p_im, axis=0)
    g_im = jnp.sum(l_im * p_re - l_re * p_im, axis=0)
    return lam, g_re, g_im


s5_scan.defvjp(_s5_scan_fwd, _s5_scan_bwd)


def _dot(a, b, mode, precision=None):
    return lax.dot_general(a, b, (_DIMS[mode], ((), ())), preferred_element_type=F32, precision=precision)


def _dot_with_mask(a, b, mode, mask_first):
    x = b if mask_first else a
    hi = x.astype(BF16)
    lo = (x - hi.astype(F32)).astype(BF16)
    if mask_first:
        mask = a.astype(BF16)
        return _dot(mask, hi, mode) + _dot(mask, lo, mode)
    mask = b.astype(BF16)
    return _dot(hi, mask, mode) + _dot(lo, mask, mode)


def _block_in_order(n, nblk, cblk, order):
    if order == 0:
        return n
    if order == 1:
        return jnp.where(n < cblk, cblk - 1 - n, nblk - 1 - (n - cblk))
    if order == 2:
        return nblk - 1 - n
    return jnp.where(n < nblk - cblk, cblk + n, n - (nblk - cblk))


def _gla_chunk_terms(qn, kn, gn, tri, reverse):
    b = _dot_with_mask(tri, gn, "tn" if reverse else "nn", True)
    edge = 0 if reverse else GLA_CHUNK - 1
    b_end = b[edge:edge + 1, :]
    e_pos, e_neg, e_end = jnp.exp(b), jnp.exp(-b), jnp.exp(b_end - b)
    return b_end, e_pos, e_neg, e_end, qn * e_pos, kn * e_neg, kn * e_end


def _gla_masks():
    L = GLA_CHUNK
    rows, cols = lax.broadcasted_iota(jnp.int32, (L, L), 0), lax.broadcasted_iota(jnp.int32, (L, L), 1)
    return rows >= cols, rows <= cols


def _gla_blocking(n_t, ctx_len):
    n_chunks, ctx_chunks = n_t // GLA_CHUNK, ctx_len // GLA_CHUNK
    per_block = _pick(math.gcd(ctx_chunks, n_chunks - ctx_chunks), (4, 2, 1))
    return n_chunks, per_block, GLA_CHUNK * per_block, n_chunks // per_block, ctx_chunks // per_block


def _gla_fwd_call(q, k, v, g, reverse, ctx_len):
    H, T, dk = q.shape
    dv = v.shape[-1]
    n_chunks, cb, rb, nb, cblk = _gla_blocking(T, ctx_len)
    L = GLA_CHUNK
    order = 1 if reverse else 0

    def body(q_ref, k_ref, v_ref, g_ref, o_ref, sb_ref, s_ref):
        @pl.when(pl.program_id(0) == 0)
        def _():
            s_ref[...] = jnp.zeros_like(s_ref)

        lower, upper = _gla_masks()
        tri = lower.astype(F32)
        seen = upper if reverse else lower
        ones = jnp.ones((L, dv), F32)
        states = [s_ref[h] for h in range(H)]
        for n in (reversed(range(cb)) if reverse else range(cb)):
            rows = slice(n * L, (n + 1) * L)
            for h in range(H):
                qn, kn, vn, gn = q_ref[h, rows, :], k_ref[h, rows, :], v_ref[h, rows, :], g_ref[h, rows, :]
                _, _, _, _, q_in, k_in, k_end = _gla_chunk_terms(qn, kn, gn, tri, reverse)
                att = jnp.where(seen, _dot(q_in.astype(BF16), k_in.astype(BF16), "nt"), 0.0)
                vb = vn.astype(BF16)
                sb_ref[h, n] = states[h]
                o_ref[h, rows, :] = (_dot(att.astype(BF16), vb, "nn")
                                     + _dot(q_in.astype(BF16), states[h].astype(BF16), "nn"))
                decay = jnp.exp(_dot_with_mask(gn, ones, "tn", False))
                states[h] = decay * states[h] + _dot(k_end.astype(BF16), vb, "tn")
        for h in range(H):
            s_ref[h] = states[h]

    def at(i):
        return 0, _block_in_order(i, nb, cblk, order), 0

    row_k, row_v = pl.BlockSpec((H, rb, dk), at), pl.BlockSpec((H, rb, dv), at)
    return pl.pallas_call(
        body, name="gla_fwd", grid=(nb,),
        in_specs=[row_k, row_k, row_v, row_k],
        out_specs=[row_v, pl.BlockSpec((H, cb, dk, dv), lambda i: (*at(i), 0))],
        out_shape=[jax.ShapeDtypeStruct((H, T, dv), F32), jax.ShapeDtypeStruct((H, n_chunks, dk, dv), F32)],
        scratch_shapes=[pltpu.VMEM((H, dk, dv), F32)],
        compiler_params=_params(("arbitrary",)),
    )(q, k, v, g)


def _gla_bwd_call(q, k, v, g, sb, do, reverse, ctx_len):
    H, T, dk = q.shape
    dv = v.shape[-1]
    n_chunks, cb, rb, nb, cblk = _gla_blocking(T, ctx_len)
    L = GLA_CHUNK
    order = 3 if reverse else 2

    def body(q_ref, k_ref, v_ref, g_ref, sb_ref, do_ref, dq_ref, dk_ref, dv_ref, dg_ref, ds_ref):
        @pl.when(pl.program_id(0) == 0)
        def _():
            ds_ref[...] = jnp.zeros_like(ds_ref)

        lower, upper = _gla_masks()
        tri = lower.astype(F32)
        seen = upper if reverse else lower
        ones = jnp.ones((L, dv), F32)
        ones8 = jnp.ones((8, dv), F32)
        d_states = [ds_ref[h] for h in range(H)]
        for n in (range(cb) if reverse else reversed(range(cb))):
            rows = slice(n * L, (n + 1) * L)
            for h in range(H):
                qn, kn, vn, gn = q_ref[h, rows, :], k_ref[h, rows, :], v_ref[h, rows, :], g_ref[h, rows, :]
                state, d_state = sb_ref[h, n], d_states[h]
                b_end, e_pos, e_neg, e_end, q_in, k_in, k_end = _gla_chunk_terms(qn, kn, gn, tri, reverse)
                q_b, k_b, ke_b, vb = q_in.astype(BF16), k_in.astype(BF16), k_end.astype(BF16), vn.astype(BF16)
                dob = do_ref[h, rows, :].astype(BF16)
                dsb = d_state.astype(BF16)
                att = jnp.where(seen, _dot(q_b, k_b, "nt"), 0.0).astype(BF16)
                d_att = jnp.where(seen, _dot(dob, vb, "nt"), 0.0).astype(BF16)
                d_qin = _dot(d_att, k_b, "nn") + _dot(dob, state.astype(BF16), "nt")
                d_kin = _dot(d_att, q_b, "tn")
                d_kend = _dot(vb, dsb, "nt")
                dv_ref[h, rows, :] = _dot(att, dob, "tn") + _dot(ke_b, dsb, "nn")
                through_decay = _dot_with_mask(ones8, state * d_state, "nt", True)[0:1, :]
                d_bend = jnp.sum(d_kend * k_end, axis=0, keepdims=True) + jnp.exp(b_end) * through_decay
                d_b = d_qin * q_in - d_kin * k_in - d_kend * k_end
                dg_ref[h, rows, :] = _dot_with_mask(tri, d_b, "nn" if reverse else "tn", True) + d_bend
                dq_ref[h, rows, :] = d_qin * e_pos
                dk_ref[h, rows, :] = d_kin * e_neg + d_kend * e_end
                decay = jnp.exp(_dot_with_mask(gn, ones, "tn", False))
                d_states[h] = _dot(q_b, dob, "tn") + decay * d_state
        for h in range(H):
            ds_ref[h] = d_states[h]

    def at(i):
        return 0, _block_in_order(i, nb, cblk, order), 0

    row_k, row_v = pl.BlockSpec((H, rb, dk), at), pl.BlockSpec((H, rb, dv), at)
    return pl.pallas_call(
        body, name="gla_bwd", grid=(nb,),
        in_specs=[row_k, row_k, row_v, row_k, pl.BlockSpec((H, cb, dk, dv), lambda i: (*at(i), 0)), row_v],
        out_specs=[row_k, row_k, row_v, row_k],
        out_shape=[jax.ShapeDtypeStruct((H, T, dk), F32), jax.ShapeDtypeStruct((H, T, dk), F32),
                   jax.ShapeDtypeStruct((H, T, dv), F32), jax.ShapeDtypeStruct((H, T, dk), F32)],
        scratch_shapes=[pltpu.VMEM((H, dk, dv), F32)],
        compiler_params=_params(("arbitrary",)),
    )(q, k, v, g, sb, do)


@functools.partial(jax.custom_vjp, nondiff_argnums=(4, 5))
def gla_scan(q, k, v, g, reverse, ctx_len):
    return _gla_fwd_call(q, k, v, g, reverse, ctx_len)[0]


def _gla_scan_fwd(q, k, v, g, reverse, ctx_len):
    o, sb = _gla_fwd_call(q, k, v, g, reverse, ctx_len)
    return o, (q, k, v, g, sb)


def _gla_scan_bwd(reverse, ctx_len, res, do):
    q, k, v, g, sb = res
    return _gla_bwd_call(q, k, v, g, sb, do, reverse, ctx_len)


gla_scan.defvjp(_gla_scan_fwd, _gla_scan_bwd)


def _position():
    return lax.axis_index("x"), lax.axis_index("y"), lax.axis_index("c")


def all_gather_blocks(shard, name):
    R, W = shard.shape

    def body(x_ref, out_ref, send_sems, recv_sems, local_sem):
        x, y, c = _position()
        me, sibling = (x, y, c), (x, y, 1 - c)
        chips = [(1 - x, y), (x, 1 - y), (1 - x, 1 - y)]

        def slot(px, py, pc):
            return out_ref.at[4 * px + 2 * py + pc]

        def copy(k, block, to, src=None):
            return pltpu.make_async_remote_copy(
                src_ref=slot(*block) if src is None else src, dst_ref=slot(*block),
                send_sem=send_sems.at[k], recv_sem=recv_sems.at[k], device_id=to, device_id_type=MESH)

        mine = pltpu.make_async_copy(x_ref, slot(*me), local_sem)
        mine.start()
        first = [copy(0, me, sibling, src=x_ref)]
        first += [copy(1 + j, me, (*chip, c), src=x_ref) for j, chip in enumerate(chips)]
        for cp in first:
            cp.start()
        passed = [copy(4 + j, (*chip, c), sibling) for j, chip in enumerate(chips)]
        for j, chip in enumerate(chips):
            copy(1 + j, (*chip, c), me).wait_recv()
            passed[j].start()
        copy(0, sibling, me).wait_recv()
        for j, chip in enumerate(chips):
            copy(4 + j, (*chip, 1 - c), me).wait_recv()
        for cp in first + passed:
            cp.wait_send()
        mine.wait()

    return pl.pallas_call(
        body, name=name,
        out_shape=jax.ShapeDtypeStruct((N_DEV, R, W), shard.dtype),
        in_specs=[pl.BlockSpec(memory_space=pltpu.HBM)], out_specs=pl.BlockSpec(memory_space=pltpu.HBM),
        scratch_shapes=[pltpu.SemaphoreType.DMA((7,)), pltpu.SemaphoreType.DMA((7,)), pltpu.SemaphoreType.DMA],
    )(shard)


def exchange_blocks(blocks, name):
    _, R, W = blocks.shape
    flips = [(fx, fy, fc) for fx in (0, 1) for fy in (0, 1) for fc in (0, 1)][1:]

    def body(x_ref, out_ref, send_sems, recv_sems, local_sem):
        x, y, c = _position()
        me = 4 * x + 2 * y + c
        mine = pltpu.make_async_copy(x_ref.at[me], out_ref.at[me], local_sem)
        mine.start()
        copies = []
        for k, (fx, fy, fc) in enumerate(flips):
            px, py, pc = x ^ fx, y ^ fy, c ^ fc
            peer = 4 * px + 2 * py + pc
            copies.append((
                pltpu.make_async_remote_copy(src_ref=x_ref.at[peer], dst_ref=out_ref.at[me],
                                             send_sem=send_sems.at[k], recv_sem=recv_sems.at[k],
                                             device_id=(px, py, pc), device_id_type=MESH),
                pltpu.make_async_remote_copy(src_ref=x_ref.at[peer], dst_ref=out_ref.at[peer],
                                             send_sem=send_sems.at[k], recv_sem=recv_sems.at[k],
                                             device_id=(px, py, pc), device_id_type=MESH)))
        for send, _ in copies:
            send.start()
        for _, recv in copies:
            recv.wait_recv()
        for send, _ in copies:
            send.wait_send()
        mine.wait()

    return pl.pallas_call(
        body, name=name,
        out_shape=jax.ShapeDtypeStruct(blocks.shape, blocks.dtype),
        in_specs=[pl.BlockSpec(memory_space=pltpu.HBM)], out_specs=pl.BlockSpec(memory_space=pltpu.HBM),
        scratch_shapes=[pltpu.SemaphoreType.DMA((7,)), pltpu.SemaphoreType.DMA((7,)), pltpu.SemaphoreType.DMA],
    )(blocks)


def sum_adamw(parts, w, m, v, name):
    _, R, W = parts.shape
    tr = _pick(R, (512, 256, 128, 64, 32, 16, 8))

    def body(p_ref, w_ref, m_ref, v_ref, g_out, d_out, m_out, v_out):
        g = p_ref[0].astype(F32)
        for j in range(1, N_DEV):
            g = g + p_ref[j].astype(F32)
        m_new = ADAM_B1 * m_ref[...] + (1.0 - ADAM_B1) * g
        v_new = ADAM_B2 * v_ref[...] + (1.0 - ADAM_B2) * (g * g)
        m_hat = m_new / (1.0 - ADAM_B1 ** ADAM_STEP)
        v_hat = v_new / (1.0 - ADAM_B2 ** ADAM_STEP)
        g_out[...] = g
        d_out[...] = -ADAM_LR * (m_hat / (jnp.sqrt(v_hat) + ADAM_EPS) + ADAM_WD * w_ref[...])
        m_out[...] = m_new
        v_out[...] = v_new

    row = pl.BlockSpec((tr, W), lambda i: (i, 0))
    return pl.pallas_call(
        body, name=name, grid=(R // tr,),
        in_specs=[pl.BlockSpec((N_DEV, tr, W), lambda i: (0, i, 0)), row, row, row],
        out_specs=[row, row, row, row],
        out_shape=[jax.ShapeDtypeStruct((R, W), F32)] * 4,
        compiler_params=_params(("parallel",)),
    )(parts, w, m, v)


def _padded(n):
    return -(-n // PACK_UNIT) * PACK_UNIT


def _pack(arrays, dtype):
    segs = []
    for arr in arrays:
        flat = arr.reshape(-1).astype(dtype)
        segs.append(jnp.pad(flat, (0, _padded(flat.size) - flat.size)))
    return jnp.concatenate(segs).reshape(-1, PACK_W)


def _pack_blocks(arrays, dtype):
    segs = []
    for arr in arrays:
        flat = arr.reshape(N_DEV, -1).astype(dtype)
        segs.append(jnp.pad(flat, ((0, 0), (0, _padded(flat.shape[1]) - flat.shape[1]))))
    return jnp.concatenate(segs, axis=1).reshape(N_DEV, -1, PACK_W)


def _unpack(buf, shapes):
    lead = buf.shape[:-2]
    flat = buf.reshape(*lead, -1)
    out, off = [], 0
    for shape in shapes:
        n = int(np.prod(shape))
        out.append(flat[..., off:off + n].reshape(*lead, *shape))
        off += _padded(n)
    return out


def _round_up(n, unit):
    return -(-n // unit) * unit


def _pack_rows(arrays, width, dtype):
    parts = []
    for arr in arrays:
        r, c = arr.shape[-2:]
        pad = [(0, 0)] * (arr.ndim - 2) + [(0, _round_up(r, 16) - r), (0, width - c)]
        parts.append(jnp.pad(arr.astype(dtype), pad))
    return jnp.concatenate(parts, axis=-2)


def _unpack_rows(buf, shapes):
    out, off = [], 0
    for r, c in shapes:
        out.append(buf[..., off:off + r, :c])
        off += _round_up(r, 16)
    return out


def _to_full(blocks, axis):
    moved = jnp.moveaxis(blocks, 0, axis)
    shape = list(moved.shape)
    shape[axis:axis + 2] = [shape[axis] * shape[axis + 1]]
    return moved.reshape(shape)


def _to_blocks(full, axis):
    shape = list(full.shape)
    shape[axis:axis + 1] = [N_DEV, shape[axis] // N_DEV]
    return jnp.moveaxis(full.reshape(shape), axis, 0)


def rms_norm(x, gain):
    return x * lax.rsqrt(jnp.mean(x * x, axis=-1, keepdims=True) + EPS) * gain


def _rope_tables(n_tokens):
    rows = n_tokens // GRID_W
    row = jnp.repeat(jnp.arange(rows, dtype=F32), GRID_W)
    col = jnp.tile(jnp.arange(GRID_W, dtype=F32), rows)
    n_freq = ATT_HEAD_DIM // 4
    inv_freq = ROPE_THETA ** (-jnp.arange(n_freq, dtype=F32) / n_freq)
    ang = jnp.stack([row[:, None] * inv_freq, col[:, None] * inv_freq], axis=1)
    return jnp.cos(ang), jnp.sin(ang)


def _rope(x, cos, sin):
    n_t, nh, hd = x.shape
    xr = x.reshape(n_t, nh, 2, 2, hd // 4)
    x1, x2 = xr[..., 0, :], xr[..., 1, :]
    cs, sn = cos[:, None], sin[:, None]
    return jnp.stack([x1 * cs - x2 * sn, x2 * cs + x1 * sn], axis=-2).reshape(n_t, nh, hd)


W_IN_ORDER = (0, 1, 2, 3, 4, 5, 6, 9, 10, 7, 8)
W_IN_SIZES = tuple(IN_SPLITS[s] for s in W_IN_ORDER)
W_IN_SEGMENTS = W_IN_SIZES[:8] + (D_MODEL,) * N_BRANCH + W_IN_SIZES[9:]


def _w_in_reorder(w, to_kernel_order):
    if to_kernel_order:
        points = np.cumsum((0,) + IN_SPLITS)
        pieces = [w[..., int(points[s]):int(points[s + 1])] for s in W_IN_ORDER]
    else:
        points = np.cumsum((0,) + W_IN_SIZES)
        where = {s: j for j, s in enumerate(W_IN_ORDER)}
        pieces = [w[..., int(points[where[s]]):int(points[where[s] + 1])] for s in range(len(IN_SPLITS))]
    return jnp.concatenate(pieces, axis=-1)


def _s5_discretize(a_re, a_im, log_dt, b_re, b_im):
    dt = jnp.exp(log_dt)[:, None]
    mag = jnp.exp(a_re * dt)
    ab_re, ab_im = mag * jnp.cos(a_im * dt), mag * jnp.sin(a_im * dt)
    den = a_re * a_re + a_im * a_im
    f_re = ((ab_re - 1.0) * a_re + ab_im * a_im) / den
    f_im = (ab_im * a_re - (ab_re - 1.0) * a_im) / den
    bb_re = f_re[..., None] * b_re - f_im[..., None] * b_im
    bb_im = f_re[..., None] * b_im + f_im[..., None] * b_re
    return ab_re, ab_im, bb_re, bb_im


def _s5_direction(u, lp, d, ctx_len):
    ab_re, ab_im, bb_re, bb_im = _s5_discretize(lp["s5_a_re"][d], lp["s5_a_im"][d], lp["s5_log_dt"][d],
                                                lp["s5_b_re"][d], lp["s5_b_im"][d])
    eye = jnp.eye(S5_GROUPS, dtype=F32)
    b_cat = jnp.concatenate([jnp.einsum("gph,gk->ghkp", bb_re, eye).reshape(S5_WIDTH, S5_FLAT),
                             jnp.einsum("gph,gk->ghkp", bb_im, eye).reshape(S5_WIDTH, S5_FLAT)], axis=1)
    c_cat = jnp.concatenate([jnp.einsum("ghp,gk->gpkh", lp["s5_c_re"][d], eye).reshape(S5_FLAT, S5_WIDTH),
                             -jnp.einsum("ghp,gk->gpkh", lp["s5_c_im"][d], eye).reshape(S5_FLAT, S5_WIDTH)], axis=0)
    bu = matmul(u, b_cat, (False, True))
    s = s5_scan(bu, ab_re.reshape(-1), ab_im.reshape(-1), d, ctx_len)
    return matmul(s, c_cat, (True, False))


def _s5_branch(u, lp, ctx_len):
    y = _s5_direction(u, lp, 0, ctx_len) + _s5_direction(u, lp, 1, ctx_len) + lp["s5_d"] * u
    y = jax.nn.gelu(y)
    return y * jax.nn.sigmoid(matmul(y, lp["s5_glu_w"], FROM_XLA) + lp["s5_glu_b"])


def _heads(a, nh):
    return a.reshape(a.shape[0], nh, a.shape[1] // nh).transpose(1, 0, 2)


def _token_mixer(h, lp, rope, ctx_len, with_ctx_out):
    n_t = h.shape[0]
    aq, ak, av, gq, gk, gv, gr, su, bg_att, bg_gla, bg_s5, glf, glb = linear_split(
        h, lp["w_in"], lp["w_in_slot"], W_IN_SEGMENTS, (True, False))

    aq = rms_norm(aq.reshape(n_t, ATT_HEADS, ATT_HEAD_DIM), lp["q_norm"])
    ak = rms_norm(ak.reshape(n_t, ATT_KV_HEADS, ATT_HEAD_DIM), lp["k_norm"])
    aq = jnp.concatenate([aq[:ctx_len], _rope(aq[ctx_len:], *rope)], axis=0).transpose(1, 0, 2)
    ak = jnp.concatenate([ak[:ctx_len], _rope(ak[ctx_len:], *rope)], axis=0).transpose(1, 0, 2)
    av = av.reshape(n_t, ATT_KV_HEADS, ATT_HEAD_DIM).transpose(1, 0, 2)
    o_att_lat = attention(aq[:, ctx_len:], ak, av)
    if with_ctx_out:
        o_att_ctx = attention(aq[:, :ctx_len], ak[:, :ctx_len], av[:, :ctx_len])
        o_att = jnp.concatenate([o_att_ctx, o_att_lat], axis=1)
    else:
        o_att = o_att_lat
    o_att = o_att.transpose(1, 0, 2).reshape(-1, ATT_WIDTH)

    def log_decay(low, d):
        z = jnp.dot(low, lp["gla_gate_w"][d]) + lp["gla_gate_b"][d]
        return _heads(jax.nn.log_sigmoid(z) / GLA_GATE_NORM, GLA_HEADS)

    q_g, k_g, v_g = _heads(gq, GLA_HEADS) * (GLA_DK ** -0.5), _heads(gk, GLA_HEADS), _heads(gv, GLA_HEADS)
    o_f = gla_scan(q_g, k_g, v_g, log_decay(glf, 0), False, ctx_len)
    o_b = gla_scan(q_g, k_g, v_g, log_decay(glb, 1), True, ctx_len)
    o_gla = rms_norm((o_f + o_b).transpose(1, 0, 2), lp["gla_out_norm"]).reshape(n_t, GLA_V_WIDTH)
    o_gla = o_gla * jax.nn.silu(gr)

    o_s5 = _s5_branch(su, lp, ctx_len)

    if not with_ctx_out:
        o_gla, o_s5 = o_gla[ctx_len:], o_s5[ctx_len:]
        bg_att, bg_gla, bg_s5 = bg_att[ctx_len:], bg_gla[ctx_len:], bg_s5[ctx_len:]
    g_att, g_gla, g_s5 = jax.nn.sigmoid(bg_att), jax.nn.sigmoid(bg_gla), jax.nn.sigmoid(bg_s5)
    merged = (g_att * linear(o_att, lp["w_br_att"], lp["w_br_att_slot"], FROM_XLA)
              + g_gla * linear(o_gla, lp["w_br_gla"], lp["w_br_gla_slot"], FROM_XLA)
              + g_s5 * linear(o_s5, lp["w_br_s5"], lp["w_br_s5_slot"], FROM_XLA))
    return linear(merged, lp["w_out"], lp["w_out_slot"], FROM_XLA)


def _norm_blocking(n_rows, ctx_rows):
    rb = _pick(math.gcd(ctx_rows, n_rows) if ctx_rows else n_rows, (512, 256, 128, 64, 32, 16, 8))
    return rb, n_rows // rb, ctx_rows // rb


def _segment_rows(per_segment):
    return jnp.broadcast_to(per_segment[:, None, :], (2, 8, per_segment.shape[1]))


def _segment_spec(width, cblk):
    return pl.BlockSpec((1, 8, width), lambda i: (jnp.where(i < cblk, 0, 1), 0, 0))


def _segment_sums(parts, row, cblk):
    return jnp.stack([jnp.sum(parts[:cblk, row], axis=0), jnp.sum(parts[cblk:, row], axis=0)])


def _norm_mod_fwd_call(x, gain, scale2, shift2, ctx_rows):
    n_rows, width = x.shape
    rb, nblk, cblk = _norm_blocking(n_rows, ctx_rows)

    def body(x_ref, g_ref, sc_ref, sh_ref, o_ref):
        xv = x_ref[...]
        inv = lax.rsqrt(jnp.mean(xv * xv, axis=1, keepdims=True) + EPS)
        o_ref[...] = xv * inv * (g_ref[0:1, :] * (1.0 + sc_ref[0, 0:1, :])) + sh_ref[0, 0:1, :]

    rows = pl.BlockSpec((rb, width), lambda i: (i, 0))
    return pl.pallas_call(
        body, name="norm_mod_fwd", grid=(nblk,),
        in_specs=[rows, pl.BlockSpec((8, width), lambda i: (0, 0)), _segment_spec(width, cblk), _segment_spec(width, cblk)],
        out_specs=rows, out_shape=jax.ShapeDtypeStruct((n_rows, width), F32),
        compiler_params=_params(("parallel",)),
    )(x, jnp.broadcast_to(gain[None, :], (8, width)), _segment_rows(scale2), _segment_rows(shift2))


def _norm_mod_bwd_call(x, gain, scale2, dh, ctx_rows):
    n_rows, width = x.shape
    rb, nblk, cblk = _norm_blocking(n_rows, ctx_rows)

    def body(x_ref, g_ref, sc_ref, dh_ref, dx_ref, part_ref):
        xv, dh = x_ref[...], dh_ref[...]
        gain_row, one_plus = g_ref[0:1, :], 1.0 + sc_ref[0, 0:1, :]
        inv = lax.rsqrt(jnp.mean(xv * xv, axis=1, keepdims=True) + EPS)
        xhat = xv * inv
        d_xhat = dh * (gain_row * one_plus)
        dx_ref[...] = inv * (d_xhat - xhat * jnp.mean(d_xhat * xhat, axis=1, keepdims=True))
        dh_xhat = dh * xhat
        part_ref[0] = jnp.concatenate([jnp.sum(dh, axis=0, keepdims=True),
                                       jnp.sum(dh_xhat * gain_row, axis=0, keepdims=True),
                                       jnp.sum(dh_xhat * one_plus, axis=0, keepdims=True),
                                       jnp.zeros((5, width), F32)], axis=0)

    rows = pl.BlockSpec((rb, width), lambda i: (i, 0))
    return pl.pallas_call(
        body, name="norm_mod_bwd", grid=(nblk,),
        in_specs=[rows, pl.BlockSpec((8, width), lambda i: (0, 0)), _segment_spec(width, cblk), rows],
        out_specs=[rows, pl.BlockSpec((1, 8, width), lambda i: (i, 0, 0))],
        out_shape=[jax.ShapeDtypeStruct((n_rows, width), F32), jax.ShapeDtypeStruct((nblk, 8, width), F32)],
        compiler_params=_params(("parallel",)),
    )(x, jnp.broadcast_to(gain[None, :], (8, width)), _segment_rows(scale2), dh)


@functools.partial(jax.custom_vjp, nondiff_argnums=(4,))
def norm_mod(x, gain, scale2, shift2, ctx_rows):
    return _norm_mod_fwd_call(x, gain, scale2, shift2, ctx_rows)


def _norm_mod_fwd(x, gain, scale2, shift2, ctx_rows):
    return _norm_mod_fwd_call(x, gain, scale2, shift2, ctx_rows), (x, gain, scale2)


def _norm_mod_bwd(ctx_rows, res, dh):
    x, gain, scale2 = res
    cblk = _norm_blocking(x.shape[0], ctx_rows)[2]
    dx, parts = _norm_mod_bwd_call(x, gain, scale2, dh, ctx_rows)
    return dx, jnp.sum(parts[:, 2], axis=0), _segment_sums(parts, 1, cblk), _segment_sums(parts, 0, cblk)


norm_mod.defvjp(_norm_mod_fwd, _norm_mod_bwd)


def _resid_norm_fwd_call(x, y, gain, gate2, ctx_rows):
    n_rows, width = x.shape
    rb, nblk, cblk = _norm_blocking(n_rows, ctx_rows)

    def body(x_ref, y_ref, g_ref, gate_ref, o_ref):
        yv = y_ref[...]
        inv = lax.rsqrt(jnp.mean(yv * yv, axis=1, keepdims=True) + EPS)
        o_ref[...] = x_ref[...] + gate_ref[0, 0:1, :] * (yv * inv * g_ref[0:1, :])

    rows = pl.BlockSpec((rb, width), lambda i: (i, 0))
    return pl.pallas_call(
        body, name="resid_norm_fwd", grid=(nblk,),
        in_specs=[rows, rows, pl.BlockSpec((8, width), lambda i: (0, 0)), _segment_spec(width, cblk)],
        out_specs=rows, out_shape=jax.ShapeDtypeStruct((n_rows, width), F32),
        compiler_params=_params(("parallel",)),
    )(x, y, jnp.broadcast_to(gain[None, :], (8, width)), _segment_rows(gate2))


def _resid_norm_bwd_call(y, gain, gate2, d_out, ctx_rows):
    n_rows, width = y.shape
    rb, nblk, cblk = _norm_blocking(n_rows, ctx_rows)

    def body(y_ref, g_ref, gate_ref, do_ref, dy_ref, part_ref):
        yv, do = y_ref[...], do_ref[...]
        gain_row, gate_row = g_ref[0:1, :], gate_ref[0, 0:1, :]
        inv = lax.rsqrt(jnp.mean(yv * yv, axis=1, keepdims=True) + EPS)
        yhat = yv * inv
        d_yhat = do * (gate_row * gain_row)
        dy_ref[...] = inv * (d_yhat - yhat * jnp.mean(d_yhat * yhat, axis=1, keepdims=True))
        do_yhat = do * yhat
        part_ref[0] = jnp.concatenate([jnp.sum(do_yhat * gain_row, axis=0, keepdims=True),
                                       jnp.sum(do_yhat * gate_row, axis=0, keepdims=True),
                                       jnp.zeros((6, width), F32)], axis=0)

    rows = pl.BlockSpec((rb, width), lambda i: (i, 0))
    return pl.pallas_call(
        body, name="resid_norm_bwd", grid=(nblk,),
        in_specs=[rows, pl.BlockSpec((8, width), lambda i: (0, 0)), _segment_spec(width, cblk), rows],
        out_specs=[rows, pl.BlockSpec((1, 8, width), lambda i: (i, 0, 0))],
        out_shape=[jax.ShapeDtypeStruct((n_rows, width), F32), jax.ShapeDtypeStruct((nblk, 8, width), F32)],
        compiler_params=_params(("parallel",)),
    )(y, jnp.broadcast_to(gain[None, :], (8, width)), _segment_rows(gate2), d_out)


@functools.partial(jax.custom_vjp, nondiff_argnums=(4,))
def resid_norm(x, y, gain, gate2, ctx_rows):
    return _resid_norm_fwd_call(x, y, gain, gate2, ctx_rows)


def _resid_norm_fwd(x, y, gain, gate2, ctx_rows):
    return _resid_norm_fwd_call(x, y, gain, gate2, ctx_rows), (y, gain, gate2)


def _resid_norm_bwd(ctx_rows, res, d_out):
    y, gain, gate2 = res
    cblk = _norm_blocking(y.shape[0], ctx_rows)[2]
    dy, parts = _resid_norm_bwd_call(y, gain, gate2, d_out, ctx_rows)
    return d_out, dy, jnp.sum(parts[:, 1], axis=0), _segment_sums(parts, 0, cblk)


resid_norm.defvjp(_resid_norm_fwd, _resid_norm_bwd)


HALO = 8


def _ffn_mid_blocking(n_rows, half):
    return _pick(n_rows, (1056, 1024, 256, 128, 64, 32, 16, 8)), _pick(half, (256, 128))


def _ffn_mid_specs(n_rows, rb, tc):
    per = rb // HALO
    return [pl.BlockSpec((rb, tc), lambda j, i: (i, j)),
            pl.BlockSpec((HALO, tc), lambda j, i: (jnp.maximum(i * per - 1, 0), j)),
            pl.BlockSpec((HALO, tc), lambda j, i: (jnp.minimum((i + 1) * per, n_rows // HALO - 1), j))]


def _with_halo(main_ref, prev_ref, next_ref):
    return jnp.concatenate([prev_ref[...], main_ref[...], next_ref[...]], axis=0)


def _row_neighbours(ext, first_row, n_rows, starts):
    n = ext.shape[0]
    row = lax.broadcasted_iota(jnp.int32, ext.shape, 0) + first_row
    first = functools.reduce(jnp.logical_or, [row == s for s in starts])
    last = functools.reduce(jnp.logical_or, [row == e - 1 for e in tuple(starts[1:]) + (n_rows,)])
    return jnp.where(first, 0.0, pltpu.roll(ext, 1, 0)), jnp.where(last, 0.0, pltpu.roll(ext, n - 1, 0))


def _ffn_mid_fwd_call(u_a, u_v, taps, starts):
    n_rows, half = u_a.shape
    rb, tc = _ffn_mid_blocking(n_rows, half)
    specs = _ffn_mid_specs(n_rows, rb, tc)
    nj = half // tc

    def body(am, ap, an, vm, vp, vn, wa, wv, o_ref):
        first_row = pl.program_id(1) * rb - HALO

        def conv(ext, w):
            above, below = _row_neighbours(ext, first_row, n_rows, starts)
            return above * w[0:1, :] + ext * w[1:2, :] + below * w[2:3, :] + w[3:4, :]

        ca = conv(_with_halo(am, ap, an), wa)[HALO:HALO + rb]
        cv = conv(_with_halo(vm, vp, vn), wv)[HALO:HALO + rb]
        o_ref[...] = ca * jax.nn.sigmoid(ca) * cv

    return pl.pallas_call(
        body, name="ffn_mid_fwd", grid=(nj, n_rows // rb),
        in_specs=specs + specs + [pl.BlockSpec((8, tc), lambda j, i: (0, j)),
                                  pl.BlockSpec((8, tc), lambda j, i: (0, nj + j))],
        out_specs=pl.BlockSpec((rb, tc), lambda j, i: (i, j)),
        out_shape=jax.ShapeDtypeStruct((n_rows, half), F32),
        compiler_params=_params(("parallel", "parallel")),
    )(u_a, u_a, u_a, u_v, u_v, u_v, taps, taps)


def _ffn_mid_bwd_call(u_a, u_v, taps, d_act, starts):
    n_rows, half = u_a.shape
    rb, tc = _ffn_mid_blocking(n_rows, half)
    specs = _ffn_mid_specs(n_rows, rb, tc)
    nj = half // tc
    main = slice(HALO, HALO + rb)

    def body(am, ap, an, vm, vp, vn, wa, wv, dm, dp, dn, dua_ref, duv_ref, dwa_ref, dwv_ref):
        i = pl.program_id(1)
        first_row = i * rb - HALO
        neighbours = functools.partial(_row_neighbours, first_row=first_row, n_rows=n_rows, starts=starts)
        ext_a, ext_v, ext_d = _with_halo(am, ap, an), _with_halo(vm, vp, vn), _with_halo(dm, dp, dn)
        above_a, below_a = neighbours(ext_a)
        above_v, below_v = neighbours(ext_v)
        ca = above_a * wa[0:1, :] + ext_a * wa[1:2, :] + below_a * wa[2:3, :] + wa[3:4, :]
        cv = above_v * wv[0:1, :] + ext_v * wv[1:2, :] + below_v * wv[2:3, :] + wv[3:4, :]
        sig = jax.nn.sigmoid(ca)
        d_cv = ext_d * (ca * sig)
        d_ca = ext_d * cv * (sig * (1.0 + ca * (1.0 - sig)))

        def finish(d_c, above, ext, below, w, du_ref, dw_ref):
            d_above, d_below = neighbours(d_c)
            du_ref[...] = (w[1:2, :] * d_c + w[0:1, :] * d_below + w[2:3, :] * d_above)[main]
            d_main = d_c[main]
            sums = jnp.concatenate([jnp.sum(above[main] * d_main, axis=0, keepdims=True),
                                    jnp.sum(ext[main] * d_main, axis=0, keepdims=True),
                                    jnp.sum(below[main] * d_main, axis=0, keepdims=True),
                                    jnp.sum(d_main, axis=0, keepdims=True), jnp.zeros((4, tc), F32)], axis=0)

            @pl.when(i == 0)
            def _():
                dw_ref[...] = sums

            @pl.when(i > 0)
            def _():
                dw_ref[...] += sums

        finish(d_ca, above_a, ext_a, below_a, wa, dua_ref, dwa_ref)
        finish(d_cv, above_v, ext_v, below_v, wv, duv_ref, dwv_ref)

    block = pl.BlockSpec((rb, tc), lambda j, i: (i, j))
    taps_out = pl.BlockSpec((8, tc), lambda j, i: (0, j))
    return pl.pallas_call(
        body, name="ffn_mid_bwd", grid=(nj, n_rows // rb),
        in_specs=specs + specs + [pl.BlockSpec((8, tc), lambda j, i: (0, j)),
                                  pl.BlockSpec((8, tc), lambda j, i: (0, nj + j))] + specs,
        out_specs=[block, block, taps_out, taps_out],
        out_shape=[jax.ShapeDtypeStruct((n_rows, half), F32)] * 2 + [jax.ShapeDtypeStruct((8, half), F32)] * 2,
        compiler_params=_params(("parallel", "arbitrary")),
    )(u_a, u_a, u_a, u_v, u_v, u_v, taps, taps, d_act, d_act, d_act)


def _taps(conv_w, conv_b):
    return jnp.concatenate([conv_w, conv_b[None, :], jnp.zeros((4, conv_w.shape[1]), F32)], axis=0)


@functools.partial(jax.custom_vjp, nondiff_argnums=(4,))
def ffn_mid(u_a, u_v, conv_w, conv_b, starts):
    return _ffn_mid_fwd_call(u_a, u_v, _taps(conv_w, conv_b), starts)


def _ffn_mid_fwd(u_a, u_v, conv_w, conv_b, starts):
    return _ffn_mid_fwd_call(u_a, u_v, _taps(conv_w, conv_b), starts), (u_a, u_v, conv_w, conv_b)


def _ffn_mid_bwd(starts, res, d_act):
    u_a, u_v, conv_w, conv_b = res
    du_a, du_v, dw_a, dw_v = _ffn_mid_bwd_call(u_a, u_v, _taps(conv_w, conv_b), d_act, starts)
    d_taps = jnp.concatenate([dw_a, dw_v], axis=1)
    return du_a, du_v, d_taps[0:3], d_taps[3]


ffn_mid.defvjp(_ffn_mid_fwd, _ffn_mid_bwd)


def _conv_ffn(h, lp, starts):
    u_a = linear(h, lp["ffn_up"], lp["ffn_up_slot"], (True, True))
    u_v = linear(h, lp["ffn_up_v"], lp["ffn_up_v_slot"], (True, True))
    act = ffn_mid(u_a, u_v, lp["ffn_conv_w"], lp["ffn_conv_b"], tuple(starts))
    return linear(act, lp["ffn_down"], lp["ffn_down_slot"], (True, False))


def _local_loss(diff, fixed):
    p = {**fixed, **diff}
    x, ctx = p["x"][0], p["ctx"][0]
    n_lat, ctx_len = x.shape[0], ctx.shape[0]
    depth = p["ada_b"].shape[0]
    rope = _rope_tables(n_lat)
    rows = jnp.concatenate([ctx, x], axis=0)
    cond = jnp.zeros((16, D_MODEL), F32).at[0].set(jax.nn.silu(p["c"][0])).at[1].set(jax.nn.silu(p["c_ctx"]))
    layer_names = [n for n in WEIGHTS if n != "c_ctx"]
    for i in range(depth):
        last = i == depth - 1
        lp = {n: p[n][i] for n in layer_names}
        lp["ffn_up_v"] = p["ffn_up_v"][i]
        lp.update({n + "_slot": p[n + "_slot"][i] for n in MATMUL_WEIGHTS})
        mod = linear(cond, lp["ada_w"], lp["ada_w_slot"], FROM_XLA) + lp["ada_b"]
        m_lat, m_ctx = jnp.split(mod[0:1], 6, axis=-1), jnp.split(mod[1:2], 6, axis=-1)
        seg = [jnp.concatenate([m_ctx[k], m_lat[k]], axis=0) for k in range(6)]

        h = norm_mod(rows, lp["norm_mix_pre"], seg[1], seg[0], ctx_len)
        y = _token_mixer(h, lp, rope, ctx_len, not last)
        if last:
            rows = rows[ctx_len:]
        ctx_rows = 0 if last else ctx_len
        starts = [0] if last else [0, ctx_len]
        rows = resid_norm(rows, y, lp["norm_mix_post"], seg[2], ctx_rows)
        h = norm_mod(rows, lp["norm_ffn_pre"], seg[4], seg[3], ctx_rows)
        rows = resid_norm(rows, _conv_ffn(h, lp, starts), lp["norm_ffn_post"], seg[5], ctx_rows)
    err = jnp.square(rows - p["loss_target"][0])
    return 0.5 * jnp.sum(jnp.mean(err, axis=-1))


def kernel(x, c, ctx, c_ctx, ada_w, ada_b, norm_mix_pre, norm_mix_post, norm_ffn_pre, norm_ffn_post, w_in, q_norm, k_norm, gla_gate_w, gla_gate_b, gla_out_norm, s5_a_re, s5_a_im, s5_log_dt, s5_b_re, s5_b_im, s5_c_re, s5_c_im, s5_d, s5_glu_w, s5_glu_b, w_br_att, w_br_gla, w_br_s5, w_out, ffn_up, ffn_conv_w, ffn_conv_b, ffn_down, loss_target, m_c_ctx, m_ada_w, m_ada_b, m_norm_mix_pre, m_norm_mix_post, m_norm_ffn_pre, m_norm_ffn_post, m_w_in, m_q_norm, m_k_norm, m_gla_gate_w, m_gla_gate_b, m_gla_out_norm, m_s5_a_re, m_s5_a_im, m_s5_log_dt, m_s5_b_re, m_s5_b_im, m_s5_c_re, m_s5_c_im, m_s5_d, m_s5_glu_w, m_s5_glu_b, m_w_br_att, m_w_br_gla, m_w_br_s5, m_w_out, m_ffn_up, m_ffn_conv_w, m_ffn_conv_b, m_ffn_down, v_c_ctx, v_ada_w, v_ada_b, v_norm_mix_pre, v_norm_mix_post, v_norm_ffn_pre, v_norm_ffn_post, v_w_in, v_q_norm, v_k_norm, v_gla_gate_w, v_gla_gate_b, v_gla_out_norm, v_s5_a_re, v_s5_a_im, v_s5_log_dt, v_s5_b_re, v_s5_b_im, v_s5_c_re, v_s5_c_im, v_s5_d, v_s5_glu_w, v_s5_glu_b, v_w_br_att, v_w_br_gla, v_w_br_s5, v_w_out, v_ffn_up, v_ffn_conv_w, v_ffn_conv_b, v_ffn_down):
    args = (x, c, ctx, c_ctx, ada_w, ada_b, norm_mix_pre, norm_mix_post, norm_ffn_pre, norm_ffn_post, w_in, q_norm, k_norm, gla_gate_w, gla_gate_b, gla_out_norm, s5_a_re, s5_a_im, s5_log_dt, s5_b_re, s5_b_im, s5_c_re, s5_c_im, s5_d, s5_glu_w, s5_glu_b, w_br_att, w_br_gla, w_br_s5, w_out, ffn_up, ffn_conv_w, ffn_conv_b, ffn_down)
    given = dict(zip(FWD_INPUTS, args))
    given["loss_target"] = loss_target
    m_in = dict(zip(WEIGHTS, (m_c_ctx, m_ada_w, m_ada_b, m_norm_mix_pre, m_norm_mix_post, m_norm_ffn_pre, m_norm_ffn_post, m_w_in, m_q_norm, m_k_norm, m_gla_gate_w, m_gla_gate_b, m_gla_out_norm, m_s5_a_re, m_s5_a_im, m_s5_log_dt, m_s5_b_re, m_s5_b_im, m_s5_c_re, m_s5_c_im, m_s5_d, m_s5_glu_w, m_s5_glu_b, m_w_br_att, m_w_br_gla, m_w_br_s5, m_w_out, m_ffn_up, m_ffn_conv_w, m_ffn_conv_b, m_ffn_down)))
    v_in = dict(zip(WEIGHTS, (v_c_ctx, v_ada_w, v_ada_b, v_norm_mix_pre, v_norm_mix_post, v_norm_ffn_pre, v_norm_ffn_post, v_w_in, v_q_norm, v_k_norm, v_gla_gate_w, v_gla_gate_b, v_gla_out_norm, v_s5_a_re, v_s5_a_im, v_s5_log_dt, v_s5_b_re, v_s5_b_im, v_s5_c_re, v_s5_c_im, v_s5_d, v_s5_glu_w, v_s5_glu_b, v_w_br_att, v_w_br_gla, v_w_br_s5, v_w_out, v_ffn_up, v_ffn_conv_w, v_ffn_conv_b, v_ffn_down)))
    depth = ada_b.shape[0]
    big_names, small_names = list(BIG), list(SMALL_SHARDED)
    sharded_names = big_names + small_names

    full = {}
    groups = {}
    for n in big_names:
        groups.setdefault(_round_up(given[n].shape[2], 128), []).append(n)
    per_layer = {n: [] for n in big_names}
    for i in range(depth):
        for width, names in groups.items():
            gathered = all_gather_blocks(_pack_rows([given[n][i] for n in names], width, BF16),
                                         f"gather_weights_{width}")
            for n, blocks in zip(names, _unpack_rows(gathered, [given[n].shape[1:] for n in names])):
                per_layer[n].append(_to_full(blocks, BIG[n]))
    for n in big_names:
        full[n] = jnp.stack(per_layer[n])
    small_shapes = [given[n].shape for n in small_names]
    gathered = all_gather_blocks(_pack([given[n] for n in small_names], F32), "gather_small")
    for n, blocks in zip(small_names, _unpack(gathered, small_shapes)):
        full[n] = _to_full(blocks, SMALL_SHARDED[n] + 1)

    w_in_full = jnp.pad(_w_in_reorder(full["w_in"], True), ((0, 0), (0, 0), (0, D_IN_PAD - D_IN)))

    diff = {"x": x}
    diff.update({n: given[n] for n in REPLICATED})
    diff.update({n: full[n] for n in small_names})
    fixed = {"c": c, "ctx": ctx, "loss_target": loss_target}
    d_ff = full["ffn_up"].shape[2] // 2
    for n in big_names:
        fixed[n] = w_in_full if n == "w_in" else full[n]
    fixed["ffn_up"], fixed["ffn_up_v"] = full["ffn_up"][:, :, :d_ff], full["ffn_up"][:, :, d_ff:]
    for n in MATMUL_WEIGHTS:
        diff[n + "_slot"] = jnp.zeros(fixed[n].shape, F32)
    loss_local, grads = jax.value_and_grad(_local_loss)(diff, fixed)
    loss = lax.psum(loss_local, ("x", "y", "c"))

    g_full = {n: grads[n] for n in small_names}
    for n in big_names:
        g_full[n] = grads[n + "_slot"]
    g_full["ffn_up"] = jnp.concatenate([grads["ffn_up_slot"], grads["ffn_up_v_slot"]], axis=2)
    g_full["w_in"] = _w_in_reorder(g_full["w_in"], False)

    out_g, out_d, out_m, out_v = {}, {}, {}, {}
    layer_out = {n: ([], [], [], []) for n in big_names}
    for i in range(depth):
        for width, names in groups.items():
            send = _pack_rows([_to_blocks(g_full[n][i], BIG[n]) for n in names], width, BF16)
            parts = exchange_blocks(send, f"exchange_grads_{width}")
            results = sum_adamw(parts, _pack_rows([given[n][i] for n in names], width, F32),
                                _pack_rows([m_in[n][i] for n in names], width, F32),
                                _pack_rows([v_in[n][i] for n in names], width, F32), f"adamw_{width}")
            for k, res in enumerate(results):
                for n, arr in zip(names, _unpack_rows(res, [given[n].shape[1:] for n in names])):
                    layer_out[n][k].append(arr)
    for n in big_names:
        out_g[n], out_d[n], out_m[n], out_v[n] = (jnp.stack(parts_k) for parts_k in layer_out[n])
    send = _pack_blocks([_to_blocks(g_full[n], SMALL_SHARDED[n] + 1) for n in small_names], F32)
    parts = exchange_blocks(send, "exchange_small")
    results = sum_adamw(parts, _pack([given[n] for n in small_names], F32), _pack([m_in[n] for n in small_names], F32),
                        _pack([v_in[n] for n in small_names], F32), "adamw_small")
    for store, res in zip((out_g, out_d, out_m, out_v), results):
        for n, arr in zip(small_names, _unpack(res, small_shapes)):
            store[n] = arr

    rep_shapes = [given[n].shape for n in REPLICATED]
    parts = all_gather_blocks(_pack([grads[n] for n in REPLICATED], F32), "gather_rep_grads")
    results = sum_adamw(parts, _pack([given[n] for n in REPLICATED], F32), _pack([m_in[n] for n in REPLICATED], F32),
                        _pack([v_in[n] for n in REPLICATED], F32), "adamw_replicated")
    for store, res in zip((out_g, out_d, out_m, out_v), results):
        for n, arr in zip(REPLICATED, _unpack(res, rep_shapes)):
            store[n] = arr

    return (loss, grads["x"], *[out_g[n] for n in WEIGHTS], *[out_d[n] for n in WEIGHTS],
            *[out_m[n] for n in WEIGHTS], *[out_v[n] for n in WEIGHTS])
```

```python
import functools
import math

import numpy as np
import jax
import jax.numpy as jnp
from jax import lax
from jax.experimental import pallas as pl
from jax.experimental.pallas import tpu as pltpu

F32 = jnp.float32
BF16 = jnp.bfloat16
MESH = pl.DeviceIdType.MESH
N_DEV = 8

D_MODEL = 1024
GRID_W = 64
ATT_HEADS, ATT_KV_HEADS, ATT_HEAD_DIM = 4, 2, 64
ATT_WIDTH, ATT_KV_WIDTH = 256, 128
ROPE_THETA = 10000.0
GLA_HEADS, GLA_DK, GLA_DV = 4, 64, 128
GLA_K_WIDTH, GLA_V_WIDTH = 256, 512
GLA_GATE_RANK, GLA_GATE_NORM, GLA_CHUNK = 16, 16.0, 64
S5_GROUPS, S5_GROUP_CH, S5_WIDTH, S5_STATE = 16, 16, 256, 64
S5_FLAT = S5_GROUPS * S5_STATE
N_BRANCH = 3
EPS = 1e-6
IN_SPLITS = (ATT_WIDTH, ATT_KV_WIDTH, ATT_KV_WIDTH, GLA_K_WIDTH, GLA_K_WIDTH, GLA_V_WIDTH, GLA_V_WIDTH,
             GLA_GATE_RANK, GLA_GATE_RANK, S5_WIDTH, N_BRANCH * D_MODEL)
D_IN = sum(IN_SPLITS)
D_IN_PAD = 5632

ADAM_LR, ADAM_B1, ADAM_B2, ADAM_EPS, ADAM_WD, ADAM_STEP = 0.001, 0.9, 0.999, 1e-08, 0.01, 10

VMEM_LIMIT = 56 * 1024 * 1024
MM_VMEM_BUDGET = 40 * 1024 * 1024
PACK_W = 512
PACK_UNIT = 16 * PACK_W

WEIGHTS = ['c_ctx', 'ada_w', 'ada_b', 'norm_mix_pre', 'norm_mix_post', 'norm_ffn_pre', 'norm_ffn_post', 'w_in',
           'q_norm', 'k_norm', 'gla_gate_w', 'gla_gate_b', 'gla_out_norm', 's5_a_re', 's5_a_im', 's5_log_dt',
           's5_b_re', 's5_b_im', 's5_c_re', 's5_c_im', 's5_d', 's5_glu_w', 's5_glu_b', 'w_br_att', 'w_br_gla',
           'w_br_s5', 'w_out', 'ffn_up', 'ffn_conv_w', 'ffn_conv_b', 'ffn_down']
FWD_INPUTS = ['x', 'c', 'ctx'] + WEIGHTS
BIG = {'ada_w': 1, 'w_in': 1, 'w_br_att': 1, 'w_br_gla': 1, 'w_br_s5': 1, 'w_out': 0, 'ffn_up': 1, 'ffn_down': 0}
SMALL_SHARDED = {'gla_gate_w': 2, 'gla_gate_b': 1, 's5_glu_w': 0, 'ffn_conv_w': 1}
SHARDED = {**BIG, **SMALL_SHARDED}
MATMUL_WEIGHTS = list(BIG) + ['ffn_up_v']
REPLICATED = [n for n in WEIGHTS if n not in SHARDED]


def _pick(n, cands):
    for cand in cands:
        if n % cand == 0:
            return cand
    return n


def _params(sem):
    return pltpu.CompilerParams(dimension_semantics=sem, vmem_limit_bytes=VMEM_LIMIT)


_DIMS = {"nn": ((1,), (0,)), "nt": ((1,), (1,)), "tn": ((0,), (0,))}


def _mm(a, b, mode, name):
    if mode == "tn":
        K, M = a.shape
    else:
        M, K = a.shape
    N = b.shape[0] if mode == "nt" else b.shape[1]
    tm = _pick(M, (1408, 1024, 768, 512, 256, 128))
    tn = _pick(N, (1408, 1024, 512, 256, 128))
    tk_options = [K] + [t for t in (2816, 1408, 1024, 768, 512, 256, 128) if t < K and K % t == 0]
    for tk in tk_options:
        blocks = tm * tk * a.dtype.itemsize + tk * tn * b.dtype.itemsize + tm * tn * 4
        if 2 * blocks <= MM_VMEM_BUDGET:
            break
    nk = K // tk
    dims = (_DIMS[mode], ((), ()))

    def body(a_ref, b_ref, o_ref):
        acc = lax.dot_general(a_ref[...].astype(BF16), b_ref[...].astype(BF16), dims, preferred_element_type=F32)
        if nk == 1:
            o_ref[...] = acc
        else:
            k = pl.program_id(2)

            @pl.when(k == 0)
            def _():
                o_ref[...] = acc

            @pl.when(k > 0)
            def _():
                o_ref[...] += acc

    a_spec = (pl.BlockSpec((tk, tm), lambda i, j, k: (k, i)) if mode == "tn"
              else pl.BlockSpec((tm, tk), lambda i, j, k: (i, k)))
    b_spec = (pl.BlockSpec((tn, tk), lambda i, j, k: (j, k)) if mode == "nt"
              else pl.BlockSpec((tk, tn), lambda i, j, k: (k, j)))
    return pl.pallas_call(
        body, name=name, grid=(M // tm, N // tn, nk),
        in_specs=[a_spec, b_spec], out_specs=pl.BlockSpec((tm, tn), lambda i, j, k: (i, j)),
        out_shape=jax.ShapeDtypeStruct((M, N), F32),
        compiler_params=_params(("parallel", "parallel", "arbitrary")),
    )(a, b)


FROM_XLA = (False, False)


def _rounded(x, from_kernel):
    return x if from_kernel else x.astype(BF16)


@functools.partial(jax.custom_vjp, nondiff_argnums=(2,))
def matmul(a, w, from_kernel=FROM_XLA):
    return _mm(_rounded(a, from_kernel[0]), w.astype(BF16), "nn", "mm_fwd")


def _matmul_fwd(a, w, from_kernel):
    ab, wb = _rounded(a, from_kernel[0]), w.astype(BF16)
    return _mm(ab, wb, "nn", "mm_fwd"), (ab, wb)


def _matmul_bwd(from_kernel, res, dy):
    ab, wb = res
    dyb = _rounded(dy, from_kernel[1])
    return _mm(dyb, wb, "nt", "mm_dx"), _mm(ab, dyb, "tn", "mm_dw")


matmul.defvjp(_matmul_fwd, _matmul_bwd)


@functools.partial(jax.custom_vjp, nondiff_argnums=(3,))
def linear(a, w, w_grad_slot, from_kernel=FROM_XLA):
    del w_grad_slot
    return _mm(_rounded(a, from_kernel[0]), w, "nn", "lin_fwd")


def _linear_fwd(a, w, w_grad_slot, from_kernel):
    del w_grad_slot
    ab = _rounded(a, from_kernel[0])
    return _mm(ab, w, "nn", "lin_fwd"), (ab, w)


def _linear_bwd(from_kernel, res, dy):
    ab, w = res
    dyb = _rounded(dy, from_kernel[1])
    return _mm(dyb, w, "nt", "lin_dx"), jnp.zeros_like(w), _mm(ab, dyb, "tn", "lin_dw")


linear.defvjp(_linear_fwd, _linear_bwd)


def _leading_cols(x, sizes):
    points = np.cumsum((0,) + tuple(sizes))
    return tuple(x[:, int(lo):int(hi)] for lo, hi in zip(points[:-1], points[1:]))


@functools.partial(jax.custom_vjp, nondiff_argnums=(3, 4))
def linear_split(a, w, w_grad_slot, sizes, from_kernel):
    del w_grad_slot
    return _leading_cols(_mm(_rounded(a, from_kernel[0]), w, "nn", "lin_fwd"), sizes)


def _linear_split_fwd(a, w, w_grad_slot, sizes, from_kernel):
    del w_grad_slot
    ab = _rounded(a, from_kernel[0])
    return _leading_cols(_mm(ab, w, "nn", "lin_fwd"), sizes), (ab, w)


def _linear_split_bwd(sizes, from_kernel, res, cts):
    del from_kernel
    ab, w = res
    parts = [ct.astype(BF16) for ct in cts]
    if w.shape[1] > sum(sizes):
        parts.append(jnp.zeros((cts[0].shape[0], w.shape[1] - sum(sizes)), BF16))
    dyb = jnp.concatenate(parts, axis=1)
    return _mm(dyb, w, "nt", "lin_dx"), jnp.zeros_like(w), _mm(ab, dyb, "tn", "lin_dw")


linear_split.defvjp(_linear_split_fwd, _linear_split_bwd)


def _attn_fwd_call(q, k, v):
    H, Tq, d = q.shape
    KV, Tk, _ = k.shape
    G = H // KV
    tq = _pick(Tq, (256, 128, 64))

    def body(q_ref, k_ref, v_ref, o_ref, lse_ref):
        s = lax.dot_general(q_ref[0], k_ref[0], (_DIMS["nt"], ((), ())), preferred_element_type=F32)
        m = jnp.max(s, axis=1, keepdims=True)
        p = jnp.exp(s - m)
        l = jnp.sum(p, axis=1, keepdims=True)
        o_ref[0] = jnp.dot(p.astype(BF16), v_ref[0], preferred_element_type=F32) * (1.0 / l)
        lse_ref[0] = m + jnp.log(l)

    return pl.pallas_call(
        body, name="attn_fwd", grid=(H, Tq // tq),
        in_specs=[pl.BlockSpec((1, tq, d), lambda h, i: (h, i, 0)),
                  pl.BlockSpec((1, Tk, d), lambda h, i: (h // G, 0, 0)),
                  pl.BlockSpec((1, Tk, d), lambda h, i: (h // G, 0, 0))],
        out_specs=[pl.BlockSpec((1, tq, d), lambda h, i: (h, i, 0)),
                   pl.BlockSpec((1, tq, 1), lambda h, i: (h, i, 0))],
        out_shape=[jax.ShapeDtypeStruct((H, Tq, d), F32), jax.ShapeDtypeStruct((H, Tq, 1), F32)],
        compiler_params=_params(("parallel", "parallel")),
    )(q, k, v)


def _attn_bwd_call(q, k, v, o, lse, do, scale):
    H, Tq, d = q.shape
    KV, Tk, _ = k.shape
    G = H // KV
    tq = _pick(Tq, (256, 128, 64))
    ck = _pick(Tk, (1408, 1024, 512, 256, 128, 64))
    nck = Tk // ck

    def body(q_ref, k_ref, v_ref, o_ref, lse_ref, do_ref, dq_ref, dk_ref, dv_ref):
        @pl.when((pl.program_id(1) == 0) & (pl.program_id(2) == 0))
        def _():
            dk_ref[...] = jnp.zeros_like(dk_ref)
            dv_ref[...] = jnp.zeros_like(dv_ref)

        qb = q_ref[0]
        do = do_ref[0]
        dob = do.astype(BF16)
        delta = jnp.sum(do * o_ref[0], axis=1, keepdims=True)
        lse = lse_ref[0]
        dq = jnp.zeros((tq, d), F32)
        for cidx in range(nck):
            rows = slice(cidx * ck, (cidx + 1) * ck)
            ks = k_ref[0, rows, :]
            vs = v_ref[0, rows, :]
            s = lax.dot_general(qb, ks, (_DIMS["nt"], ((), ())), preferred_element_type=F32)
            p = jnp.exp(s - lse)
            dv_ref[0, rows, :] += lax.dot_general(p.astype(BF16), dob, (_DIMS["tn"], ((), ())),
                                                  preferred_element_type=F32)
            dp = lax.dot_general(dob, vs, (_DIMS["nt"], ((), ())), preferred_element_type=F32)
            dsb = (p * (dp - delta)).astype(BF16)
            dq = dq + jnp.dot(dsb, ks, preferred_element_type=F32)
            dk_ref[0, rows, :] += lax.dot_general(dsb, qb, (_DIMS["tn"], ((), ())), preferred_element_type=F32)
        dq_ref[0] = dq * scale

    q_spec = pl.BlockSpec((1, tq, d), lambda kv, g, i: (kv * G + g, i, 0))
    kv_spec = pl.BlockSpec((1, Tk, d), lambda kv, g, i: (kv, 0, 0))
    return pl.pallas_call(
        body, name="attn_bwd", grid=(KV, G, Tq // tq),
        in_specs=[q_spec, kv_spec, kv_spec, q_spec,
                  pl.BlockSpec((1, tq, 1), lambda kv, g, i: (kv * G + g, i, 0)), q_spec],
        out_specs=[q_spec, kv_spec, kv_spec],
        out_shape=[jax.ShapeDtypeStruct((H, Tq, d), F32), jax.ShapeDtypeStruct((KV, Tk, d), F32),
                   jax.ShapeDtypeStruct((KV, Tk, d), F32)],
        compiler_params=_params(("arbitrary", "arbitrary", "arbitrary")),
    )(q, k, v, o, lse, do)


ATT_SCALE = ATT_HEAD_DIM ** -0.5


@jax.custom_vjp
def attention(q, k, v):
    return _attn_fwd_call((q * ATT_SCALE).astype(BF16), k.astype(BF16), v.astype(BF16))[0]


def _attention_fwd(q, k, v):
    qb, kb, vb = (q * ATT_SCALE).astype(BF16), k.astype(BF16), v.astype(BF16)
    o, lse = _attn_fwd_call(qb, kb, vb)
    return o, (qb, kb, vb, o, lse)


def _attention_bwd(res, do):
    qb, kb, vb, o, lse = res
    return _attn_bwd_call(qb, kb, vb, o, lse, do, ATT_SCALE)


attention.defvjp(_attention_fwd, _attention_bwd)


_ORDER_DOWN = {0: False, 1: True, 2: True, 3: False}
_ORDER_ADJOINT = {0: 2, 1: 3}


def _scan_tables(a_re, a_im, down):
    pw_re, pw_im = [a_re], [a_im]
    for _ in range(7):
        pw_re, pw_im = (pw_re + [pw_re[-1] * a_re - pw_im[-1] * a_im],
                        pw_im + [pw_re[-1] * a_im + pw_im[-1] * a_re])
    carry_rows = list(range(7, -1, -1)) if down else list(range(8))
    rows_re = [pw_re[r] for r in carry_rows] + [pw_re[0], pw_re[1], pw_re[3]]
    rows_im = [pw_im[r] for r in carry_rows] + [pw_im[0], pw_im[1], pw_im[3]]
    tab = jnp.concatenate([jnp.stack(rows_re), jnp.stack(rows_im)], axis=1)
    return jnp.concatenate([tab, jnp.zeros((5, 2 * S5_FLAT), F32)], axis=0)


def _scan_call(bu, a_re, a_im, order, ctx_len):
    T, W2 = bu.shape
    P = W2 // 2
    rb = _pick(math.gcd(ctx_len, T - ctx_len), (256, 128, 64, 32, 16, 8))
    nblk, cb = T // rb, ctx_len // rb
    ntile = rb // 8
    down = _ORDER_DOWN[order]
    tab = _scan_tables(a_re, a_im, down)

    def blk(n):
        return _block_in_order(n, nblk, cb, order)

    def body(bu_ref, tab_ref, s_ref, carry_ref):
        @pl.when(pl.program_id(0) == 0)
        def _():
            carry_ref[...] = jnp.zeros_like(carry_ref)

        row = lax.broadcasted_iota(jnp.int32, (8, P), 0)
        cp_re, cp_im = tab_ref[0:8, 0:P], tab_ref[0:8, P:2 * P]
        steps = []
        for j, sh in enumerate((1, 2, 4)):
            keep = (row < 8 - sh) if down else (row >= sh)
            steps.append((8 - sh if down else sh, keep, tab_ref[8 + j:9 + j, 0:P], tab_ref[8 + j:9 + j, P:2 * P]))

        def tile(j, carry):
            c_re, c_im = carry
            t = (ntile - 1 - j) if down else j
            r0 = pl.multiple_of(t * 8, 8)
            x_re = bu_ref[pl.ds(r0, 8), 0:P]
            x_im = bu_ref[pl.ds(r0, 8), P:2 * P]
            for shift, keep, p_re, p_im in steps:
                y_re = jnp.where(keep, pltpu.roll(x_re, shift, 0), 0.0)
                y_im = jnp.where(keep, pltpu.roll(x_im, shift, 0), 0.0)
                x_re, x_im = x_re + p_re * y_re - p_im * y_im, x_im + p_re * y_im + p_im * y_re
            x_re, x_im = x_re + cp_re * c_re - cp_im * c_im, x_im + cp_re * c_im + cp_im * c_re
            s_ref[pl.ds(r0, 8), 0:P] = x_re
            s_ref[pl.ds(r0, 8), P:2 * P] = x_im
            last = 0 if down else 7
            return x_re[last:last + 1, :], x_im[last:last + 1, :]

        c_re, c_im = lax.fori_loop(0, ntile, tile, (carry_ref[0:1, 0:P], carry_ref[0:1, P:2 * P]))
        carry_ref[0:1, 0:P] = c_re
        carry_ref[0:1, P:2 * P] = c_im

    return pl.pallas_call(
        body, name=f"s5_scan_{order}", grid=(nblk,),
        in_specs=[pl.BlockSpec((rb, W2), lambda n: (blk(n), 0)), pl.BlockSpec((16, W2), lambda n: (0, 0))],
        out_specs=pl.BlockSpec((rb, W2), lambda n: (blk(n), 0)),
        out_shape=jax.ShapeDtypeStruct((T, W2), F32),
        scratch_shapes=[pltpu.VMEM((8, W2), F32)],
        compiler_params=_params(("arbitrary",)),
    )(bu, tab)


def _prev_in_order(s, order, ctx_len):
    zero = jnp.zeros_like(s[:1])
    if order == 0:
        return jnp.concatenate([zero, s[:-1]], axis=0)
    return jnp.concatenate([s[1:ctx_len], zero, s[ctx_len + 1:], s[:1]], axis=0)


@functools.partial(jax.custom_vjp, nondiff_argnums=(3, 4))
def s5_scan(bu, a_re, a_im, order, ctx_len):
    return _scan_call(bu, a_re, a_im, order, ctx_len)


def _s5_scan_fwd(bu, a_re, a_im, order, ctx_len):
    s = _scan_call(bu, a_re, a_im, order, ctx_len)
    return s, (s, a_re, a_im)


def _s5_scan_bwd(order, ctx_len, res, ds):
    s, a_re, a_im = res
    lam = _scan_call(ds, a_re, -a_im, _ORDER_ADJOINT[order], ctx_len)
    P = a_re.shape[0]
    sp = _prev_in_order(s, order, ctx_len)
    l_re, l_im, p_re, p_im = lam[:, :P], lam[:, P:], sp[:, :P], sp[:, P:]
    g_re = jnp.sum(l_re * p_re + l_im * p_im, axis=0)
    g_im = jnp.sum(l_im * p_re - l_re * p_im, axis=0)
    return lam, g_re, g_im


s5_scan.defvjp(_s5_scan_fwd, _s5_scan_bwd)


def _dot(a, b, mode, precision=None):
    return lax.dot_general(a, b, (_DIMS[mode], ((), ())), preferred_element_type=F32, precision=precision)


def _dot_with_mask(a, b, mode, mask_first):
    x = b if mask_first else a
    hi = x.astype(BF16)
    lo = (x - hi.astype(F32)).astype(BF16)
    if mask_first:
        mask = a.astype(BF16)
        return _dot(mask, hi, mode) + _dot(mask, lo, mode)
    mask = b.astype(BF16)
    return _dot(hi, mask, mode) + _dot(lo, mask, mode)


def _block_in_order(n, nblk, cblk, order):
    if order == 0:
        return n
    if order == 1:
        return jnp.where(n < cblk, cblk - 1 - n, nblk - 1 - (n - cblk))
    if order == 2:
        return nblk - 1 - n
    return jnp.where(n < nblk - cblk, cblk + n, n - (nblk - cblk))


def _gla_chunk_terms(qn, kn, gn, tri, reverse):
    b = _dot_with_mask(tri, gn, "tn" if reverse else "nn", True)
    edge = 0 if reverse else GLA_CHUNK - 1
    b_end = b[edge:edge + 1, :]
    e_pos, e_neg, e_end = jnp.exp(b), jnp.exp(-b), jnp.exp(b_end - b)
    return b_end, e_pos, e_neg, e_end, qn * e_pos, kn * e_neg, kn * e_end


def _gla_masks():
    L = GLA_CHUNK
    rows, cols = lax.broadcasted_iota(jnp.int32, (L, L), 0), lax.broadcasted_iota(jnp.int32, (L, L), 1)
    return rows >= cols, rows <= cols


def _gla_blocking(n_t, ctx_len):
    n_chunks, ctx_chunks = n_t // GLA_CHUNK, ctx_len // GLA_CHUNK
    per_block = _pick(math.gcd(ctx_chunks, n_chunks - ctx_chunks), (4, 2, 1))
    return n_chunks, per_block, GLA_CHUNK * per_block, n_chunks // per_block, ctx_chunks // per_block


def _gla_fwd_call(q, k, v, g, reverse, ctx_len):
    H, T, dk = q.shape
    dv = v.shape[-1]
    n_chunks, cb, rb, nb, cblk = _gla_blocking(T, ctx_len)
    L = GLA_CHUNK
    order = 1 if reverse else 0

    def body(q_ref, k_ref, v_ref, g_ref, o_ref, sb_ref, s_ref):
        @pl.when(pl.program_id(0) == 0)
        def _():
            s_ref[...] = jnp.zeros_like(s_ref)

        lower, upper = _gla_masks()
        tri = lower.astype(F32)
        seen = upper if reverse else lower
        ones = jnp.ones((L, dv), F32)
        states = [s_ref[h] for h in range(H)]
        for n in (reversed(range(cb)) if reverse else range(cb)):
            rows = slice(n * L, (n + 1) * L)
            for h in range(H):
                qn, kn, vn, gn = q_ref[h, rows, :], k_ref[h, rows, :], v_ref[h, rows, :], g_ref[h, rows, :]
                _, _, _, _, q_in, k_in, k_end = _gla_chunk_terms(qn, kn, gn, tri, reverse)
                att = jnp.where(seen, _dot(q_in.astype(BF16), k_in.astype(BF16), "nt"), 0.0)
                vb = vn.astype(BF16)
                sb_ref[h, n] = states[h]
                o_ref[h, rows, :] = (_dot(att.astype(BF16), vb, "nn")
                                     + _dot(q_in.astype(BF16), states[h].astype(BF16), "nn"))
                decay = jnp.exp(_dot_with_mask(gn, ones, "tn", False))
                states[h] = decay * states[h] + _dot(k_end.astype(BF16), vb, "tn")
        for h in range(H):
            s_ref[h] = states[h]

    def at(i):
        return 0, _block_in_order(i, nb, cblk, order), 0

    row_k, row_v = pl.BlockSpec((H, rb, dk), at), pl.BlockSpec((H, rb, dv), at)
    return pl.pallas_call(
        body, name="gla_fwd", grid=(nb,),
        in_specs=[row_k, row_k, row_v, row_k],
        out_specs=[row_v, pl.BlockSpec((H, cb, dk, dv), lambda i: (*at(i), 0))],
        out_shape=[jax.ShapeDtypeStruct((H, T, dv), F32), jax.ShapeDtypeStruct((H, n_chunks, dk, dv), F32)],
        scratch_shapes=[pltpu.VMEM((H, dk, dv), F32)],
        compiler_params=_params(("arbitrary",)),
    )(q, k, v, g)


def _gla_bwd_call(q, k, v, g, sb, do, reverse, ctx_len):
    H, T, dk = q.shape
    dv = v.shape[-1]
    n_chunks, cb, rb, nb, cblk = _gla_blocking(T, ctx_len)
    L = GLA_CHUNK
    order = 3 if reverse else 2

    def body(q_ref, k_ref, v_ref, g_ref, sb_ref, do_ref, dq_ref, dk_ref, dv_ref, dg_ref, ds_ref):
        @pl.when(pl.program_id(0) == 0)
        def _():
            ds_ref[...] = jnp.zeros_like(ds_ref)

        lower, upper = _gla_masks()
        tri = lower.astype(F32)
        seen = upper if reverse else lower
        ones = jnp.ones((L, dv), F32)
        ones8 = jnp.ones((8, dv), F32)
        d_states = [ds_ref[h] for h in range(H)]
        for n in (range(cb) if reverse else reversed(range(cb))):
            rows = slice(n * L, (n + 1) * L)
            for h in range(H):
                qn, kn, vn, gn = q_ref[h, rows, :], k_ref[h, rows, :], v_ref[h, rows, :], g_ref[h, rows, :]
                state, d_state = sb_ref[h, n], d_states[h]
                b_end, e_pos, e_neg, e_end, q_in, k_in, k_end = _gla_chunk_terms(qn, kn, gn, tri, reverse)
                q_b, k_b, ke_b, vb = q_in.astype(BF16), k_in.astype(BF16), k_end.astype(BF16), vn.astype(BF16)
                dob = do_ref[h, rows, :].astype(BF16)
                dsb = d_state.astype(BF16)
                att = jnp.where(seen, _dot(q_b, k_b, "nt"), 0.0).astype(BF16)
                d_att = jnp.where(seen, _dot(dob, vb, "nt"), 0.0).astype(BF16)
                d_qin = _dot(d_att, k_b, "nn") + _dot(dob, state.astype(BF16), "nt")
                d_kin = _dot(d_att, q_b, "tn")
                d_kend = _dot(vb, dsb, "nt")
                dv_ref[h, rows, :] = _dot(att, dob, "tn") + _dot(ke_b, dsb, "nn")
                through_decay = _dot_with_mask(ones8, state * d_state, "nt", True)[0:1, :]
                d_bend = jnp.sum(d_kend * k_end, axis=0, keepdims=True) + jnp.exp(b_end) * through_decay
                d_b = d_qin * q_in - d_kin * k_in - d_kend * k_end
                dg_ref[h, rows, :] = _dot_with_mask(tri, d_b, "nn" if reverse else "tn", True) + d_bend
                dq_ref[h, rows, :] = d_qin * e_pos
                dk_ref[h, rows, :] = d_kin * e_neg + d_kend * e_end
                decay = jnp.exp(_dot_with_mask(gn, ones, "tn", False))
                d_states[h] = _dot(q_b, dob, "tn") + decay * d_state
        for h in range(H):
            ds_ref[h] = d_states[h]

    def at(i):
        return 0, _block_in_order(i, nb, cblk, order), 0

    row_k, row_v = pl.BlockSpec((H, rb, dk), at), pl.BlockSpec((H, rb, dv), at)
    return pl.pallas_call(
        body, name="gla_bwd", grid=(nb,),
        in_specs=[row_k, row_k, row_v, row_k, pl.BlockSpec((H, cb, dk, dv), lambda i: (*at(i), 0)), row_v],
        out_specs=[row_k, row_k, row_v, row_k],
        out_shape=[jax.ShapeDtypeStruct((H, T, dk), F32), jax.ShapeDtypeStruct((H, T, dk), F32),
                   jax.ShapeDtypeStruct((H, T, dv), F32), jax.ShapeDtypeStruct((H, T, dk), F32)],
        scratch_shapes=[pltpu.VMEM((H, dk, dv), F32)],
        compiler_params=_params(("arbitrary",)),
    )(q, k, v, g, sb, do)


@functools.partial(jax.custom_vjp, nondiff_argnums=(4, 5))
def gla_scan(q, k, v, g, reverse, ctx_len):
    return _gla_fwd_call(q, k, v, g, reverse, ctx_len)[0]


def _gla_scan_fwd(q, k, v, g, reverse, ctx_len):
    o, sb = _gla_fwd_call(q, k, v, g, reverse, ctx_len)
    return o, (q, k, v, g, sb)


def _gla_scan_bwd(reverse, ctx_len, res, do):
    q, k, v, g, sb = res
    return _gla_bwd_call(q, k, v, g, sb, do, reverse, ctx_len)


gla_scan.defvjp(_gla_scan_fwd, _gla_scan_bwd)


def _position():
    return lax.axis_index("x"), lax.axis_index("y"), lax.axis_index("c")


def all_gather_blocks(shard, name):
    R, W = shard.shape

    def body(x_ref, out_ref, send_sems, recv_sems, local_sem):
        x, y, c = _position()
        me, sibling = (x, y, c), (x, y, 1 - c)
        chips = [(1 - x, y), (x, 1 - y), (1 - x, 1 - y)]

        def slot(px, py, pc):
            return out_ref.at[4 * px + 2 * py + pc]

        def copy(k, block, to, src=None):
            return pltpu.make_async_remote_copy(
                src_ref=slot(*block) if src is None else src, dst_ref=slot(*block),
                send_sem=send_sems.at[k], recv_sem=recv_sems.at[k], device_id=to, device_id_type=MESH)

        mine = pltpu.make_async_copy(x_ref, slot(*me), local_sem)
        mine.start()
        first = [copy(0, me, sibling, src=x_ref)]
        first += [copy(1 + j, me, (*chip, c), src=x_ref) for j, chip in enumerate(chips)]
        for cp in first:
            cp.start()
        passed = [copy(4 + j, (*chip, c), sibling) for j, chip in enumerate(chips)]
        for j, chip in enumerate(chips):
            copy(1 + j, (*chip, c), me).wait_recv()
            passed[j].start()
        copy(0, sibling, me).wait_recv()
        for j, chip in enumerate(chips):
            copy(4 + j, (*chip, 1 - c), me).wait_recv()
        for cp in first + passed:
            cp.wait_send()
        mine.wait()

    return pl.pallas_call(
        body, name=name,
        out_shape=jax.ShapeDtypeStruct((N_DEV, R, W), shard.dtype),
        in_specs=[pl.BlockSpec(memory_space=pltpu.HBM)], out_specs=pl.BlockSpec(memory_space=pltpu.HBM),
        scratch_shapes=[pltpu.SemaphoreType.DMA((7,)), pltpu.SemaphoreType.DMA((7,)), pltpu.SemaphoreType.DMA],
    )(shard)


def exchange_blocks(blocks, name):
    _, R, W = blocks.shape
    flips = [(fx, fy, fc) for fx in (0, 1) for fy in (0, 1) for fc in (0, 1)][1:]

    def body(x_ref, out_ref, send_sems, recv_sems, local_sem):
        x, y, c = _position()
        me = 4 * x + 2 * y + c
        mine = pltpu.make_async_copy(x_ref.at[me], out_ref.at[me], local_sem)
        mine.start()
        copies = []
        for k, (fx, fy, fc) in enumerate(flips):
            px, py, pc = x ^ fx, y ^ fy, c ^ fc
            peer = 4 * px + 2 * py + pc
            copies.append((
                pltpu.make_async_remote_copy(src_ref=x_ref.at[peer], dst_ref=out_ref.at[me],
                                             send_sem=send_sems.at[k], recv_sem=recv_sems.at[k],
                                             device_id=(px, py, pc), device_id_type=MESH),
                pltpu.make_async_remote_copy(src_ref=x_ref.at[peer], dst_ref=out_ref.at[peer],
                                             send_sem=send_sems.at[k], recv_sem=recv_sems.at[k],
                                             device_id=(px, py, pc), device_id_type=MESH)))
        for send, _ in copies:
            send.start()
        for _, recv in copies:
            recv.wait_recv()
        for send, _ in copies:
            send.wait_send()
        mine.wait()

    return pl.pallas_call(
        body, name=name,
        out_shape=jax.ShapeDtypeStruct(blocks.shape, blocks.dtype),
        in_specs=[pl.BlockSpec(memory_space=pltpu.HBM)], out_specs=pl.BlockSpec(memory_space=pltpu.HBM),
        scratch_shapes=[pltpu.SemaphoreType.DMA((7,)), pltpu.SemaphoreType.DMA((7,)), pltpu.SemaphoreType.DMA],
    )(blocks)


def sum_adamw(parts, w, m, v, name):
    _, R, W = parts.shape
    tr = _pick(R, (512, 256, 128, 64, 32, 16, 8))

    def body(p_ref, w_ref, m_ref, v_ref, g_out, d_out, m_out, v_out):
        g = p_ref[0].astype(F32)
        for j in range(1, N_DEV):
            g = g + p_ref[j].astype(F32)
        m_new = ADAM_B1 * m_ref[...] + (1.0 - ADAM_B1) * g
        v_new = ADAM_B2 * v_ref[...] + (1.0 - ADAM_B2) * (g * g)
        m_hat = m_new / (1.0 - ADAM_B1 ** ADAM_STEP)
        v_hat = v_new / (1.0 - ADAM_B2 ** ADAM_STEP)
        g_out[...] = g
        d_out[...] = -ADAM_LR * (m_hat / (jnp.sqrt(v_hat) + ADAM_EPS) + ADAM_WD * w_ref[...])
        m_out[...] = m_new
        v_out[...] = v_new

    row = pl.BlockSpec((tr, W), lambda i: (i, 0))
    return pl.pallas_call(
        body, name=name, grid=(R // tr,),
        in_specs=[pl.BlockSpec((N_DEV, tr, W), lambda i: (0, i, 0)), row, row, row],
        out_specs=[row, row, row, row],
        out_shape=[jax.ShapeDtypeStruct((R, W), F32)] * 4,
        compiler_params=_params(("parallel",)),
    )(parts, w, m, v)


def _padded(n):
    return -(-n // PACK_UNIT) * PACK_UNIT


def _pack(arrays, dtype):
    segs = []
    for arr in arrays:
        flat = arr.reshape(-1).astype(dtype)
        segs.append(jnp.pad(flat, (0, _padded(flat.size) - flat.size)))
    return jnp.concatenate(segs).reshape(-1, PACK_W)


def _pack_blocks(arrays, dtype):
    segs = []
    for arr in arrays:
        flat = arr.reshape(N_DEV, -1).astype(dtype)
        segs.append(jnp.pad(flat, ((0, 0), (0, _padded(flat.shape[1]) - flat.shape[1]))))
    return jnp.concatenate(segs, axis=1).reshape(N_DEV, -1, PACK_W)


def _unpack(buf, shapes):
    lead = buf.shape[:-2]
    flat = buf.reshape(*lead, -1)
    out, off = [], 0
    for shape in shapes:
        n = int(np.prod(shape))
        out.append(flat[..., off:off + n].reshape(*lead, *shape))
        off += _padded(n)
    return out


def _round_up(n, unit):
    return -(-n // unit) * unit


def _pack_rows(arrays, width, dtype):
    parts = []
    for arr in arrays:
        r, c = arr.shape[-2:]
        pad = [(0, 0)] * (arr.ndim - 2) + [(0, _round_up(r, 16) - r), (0, width - c)]
        parts.append(jnp.pad(arr.astype(dtype), pad))
    return jnp.concatenate(parts, axis=-2)


def _unpack_rows(buf, shapes):
    out, off = [], 0
    for r, c in shapes:
        out.append(buf[..., off:off + r, :c])
        off += _round_up(r, 16)
    return out


def _to_full(blocks, axis):
    moved = jnp.moveaxis(blocks, 0, axis)
    shape = list(moved.shape)
    shape[axis:axis + 2] = [shape[axis] * shape[axis + 1]]
    return moved.reshape(shape)


def _to_blocks(full, axis):
    shape = list(full.shape)
    shape[axis:axis + 1] = [N_DEV, shape[axis] // N_DEV]
    return jnp.moveaxis(full.reshape(shape), axis, 0)


def rms_norm(x, gain):
    return x * lax.rsqrt(jnp.mean(x * x, axis=-1, keepdims=True) + EPS) * gain


def _rope_tables(n_tokens):
    rows = n_tokens // GRID_W
    row = jnp.repeat(jnp.arange(rows, dtype=F32), GRID_W)
    col = jnp.tile(jnp.arange(GRID_W, dtype=F32), rows)
    n_freq = ATT_HEAD_DIM // 4
    inv_freq = ROPE_THETA ** (-jnp.arange(n_freq, dtype=F32) / n_freq)
    ang = jnp.stack([row[:, None] * inv_freq, col[:, None] * inv_freq], axis=1)
    return jnp.cos(ang), jnp.sin(ang)


def _rope(x, cos, sin):
    n_t, nh, hd = x.shape
    xr = x.reshape(n_t, nh, 2, 2, hd // 4)
    x1, x2 = xr[..., 0, :], xr[..., 1, :]
    cs, sn = cos[:, None], sin[:, None]
    return jnp.stack([x1 * cs - x2 * sn, x2 * cs + x1 * sn], axis=-2).reshape(n_t, nh, hd)


W_IN_ORDER = (0, 1, 2, 3, 4, 5, 6, 9, 10, 7, 8)
W_IN_SIZES = tuple(IN_SPLITS[s] for s in W_IN_ORDER)
W_IN_SEGMENTS = W_IN_SIZES[:8] + (D_MODEL,) * N_BRANCH + W_IN_SIZES[9:]


def _w_in_reorder(w, to_kernel_order):
    if to_kernel_order:
        points = np.cumsum((0,) + IN_SPLITS)
        pieces = [w[..., int(points[s]):int(points[s + 1])] for s in W_IN_ORDER]
    else:
        points = np.cumsum((0,) + W_IN_SIZES)
        where = {s: j for j, s in enumerate(W_IN_ORDER)}
        pieces = [w[..., int(points[where[s]]):int(points[where[s] + 1])] for s in range(len(IN_SPLITS))]
    return jnp.concatenate(pieces, axis=-1)


def _s5_discretize(a_re, a_im, log_dt, b_re, b_im):
    dt = jnp.exp(log_dt)[:, None]
    mag = jnp.exp(a_re * dt)
    ab_re, ab_im = mag * jnp.cos(a_im * dt), mag * jnp.sin(a_im * dt)
    den = a_re * a_re + a_im * a_im
    f_re = ((ab_re - 1.0) * a_re + ab_im * a_im) / den
    f_im = (ab_im * a_re - (ab_re - 1.0) * a_im) / den
    bb_re = f_re[..., None] * b_re - f_im[..., None] * b_im
    bb_im = f_re[..., None] * b_im + f_im[..., None] * b_re
    return ab_re, ab_im, bb_re, bb_im


def _s5_direction(u, lp, d, ctx_len):
    ab_re, ab_im, bb_re, bb_im = _s5_discretize(lp["s5_a_re"][d], lp["s5_a_im"][d], lp["s5_log_dt"][d],
                                                lp["s5_b_re"][d], lp["s5_b_im"][d])
    eye = jnp.eye(S5_GROUPS, dtype=F32)
    b_cat = jnp.concatenate([jnp.einsum("gph,gk->ghkp", bb_re, eye).reshape(S5_WIDTH, S5_FLAT),
                             jnp.einsum("gph,gk->ghkp", bb_im, eye).reshape(S5_WIDTH, S5_FLAT)], axis=1)
    c_cat = jnp.concatenate([jnp.einsum("ghp,gk->gpkh", lp["s5_c_re"][d], eye).reshape(S5_FLAT, S5_WIDTH),
                             -jnp.einsum("ghp,gk->gpkh", lp["s5_c_im"][d], eye).reshape(S5_FLAT, S5_WIDTH)], axis=0)
    bu = matmul(u, b_cat, (False, True))
    s = s5_scan(bu, ab_re.reshape(-1), ab_im.reshape(-1), d, ctx_len)
    return matmul(s, c_cat, (True, False))


def _s5_branch(u, lp, ctx_len):
    y = _s5_direction(u, lp, 0, ctx_len) + _s5_direction(u, lp, 1, ctx_len) + lp["s5_d"] * u
    y = jax.nn.gelu(y)
    return y * jax.nn.sigmoid(matmul(y, lp["s5_glu_w"], FROM_XLA) + lp["s5_glu_b"])


def _heads(a, nh):
    return a.reshape(a.shape[0], nh, a.shape[1] // nh).transpose(1, 0, 2)


def _token_mixer(h, lp, rope, ctx_len, with_ctx_out):
    n_t = h.shape[0]
    aq, ak, av, gq, gk, gv, gr, su, bg_att, bg_gla, bg_s5, glf, glb = linear_split(
        h, lp["w_in"], lp["w_in_slot"], W_IN_SEGMENTS, (True, False))

    aq = rms_norm(aq.reshape(n_t, ATT_HEADS, ATT_HEAD_DIM), lp["q_norm"])
    ak = rms_norm(ak.reshape(n_t, ATT_KV_HEADS, ATT_HEAD_DIM), lp["k_norm"])
    aq = jnp.concatenate([aq[:ctx_len], _rope(aq[ctx_len:], *rope)], axis=0).transpose(1, 0, 2)
    ak = jnp.concatenate([ak[:ctx_len], _rope(ak[ctx_len:], *rope)], axis=0).transpose(1, 0, 2)
    av = av.reshape(n_t, ATT_KV_HEADS, ATT_HEAD_DIM).transpose(1, 0, 2)
    o_att_lat = attention(aq[:, ctx_len:], ak, av)
    if with_ctx_out:
        o_att_ctx = attention(aq[:, :ctx_len], ak[:, :ctx_len], av[:, :ctx_len])
        o_att = jnp.concatenate([o_att_ctx, o_att_lat], axis=1)
    else:
        o_att = o_att_lat
    o_att = o_att.transpose(1, 0, 2).reshape(-1, ATT_WIDTH)

    def log_decay(low, d):
        z = jnp.dot(low, lp["gla_gate_w"][d]) + lp["gla_gate_b"][d]
        return _heads(jax.nn.log_sigmoid(z) / GLA_GATE_NORM, GLA_HEADS)

    q_g, k_g, v_g = _heads(gq, GLA_HEADS) * (GLA_DK ** -0.5), _heads(gk, GLA_HEADS), _heads(gv, GLA_HEADS)
    o_f = gla_scan(q_g, k_g, v_g, log_decay(glf, 0), False, ctx_len)
    o_b = gla_scan(q_g, k_g, v_g, log_decay(glb, 1), True, ctx_len)
    o_gla = rms_norm((o_f + o_b).transpose(1, 0, 2), lp["gla_out_norm"]).reshape(n_t, GLA_V_WIDTH)
    o_gla = o_gla * jax.nn.silu(gr)

    o_s5 = _s5_branch(su, lp, ctx_len)

    if not with_ctx_out:
        o_gla, o_s5 = o_gla[ctx_len:], o_s5[ctx_len:]
        bg_att, bg_gla, bg_s5 = bg_att[ctx_len:], bg_gla[ctx_len:], bg_s5[ctx_len:]
    g_att, g_gla, g_s5 = jax.nn.sigmoid(bg_att), jax.nn.sigmoid(bg_gla), jax.nn.sigmoid(bg_s5)
    merged = (g_att * linear(o_att, lp["w_br_att"], lp["w_br_att_slot"], FROM_XLA)
              + g_gla * linear(o_gla, lp["w_br_gla"], lp["w_br_gla_slot"], FROM_XLA)
              + g_s5 * linear(o_s5, lp["w_br_s5"], lp["w_br_s5_slot"], FROM_XLA))
    return linear(merged, lp["w_out"], lp["w_out_slot"], FROM_XLA)


def _norm_blocking(n_rows, ctx_rows):
    rb = _pick(math.gcd(ctx_rows, n_rows) if ctx_rows else n_rows, (512, 256, 128, 64, 32, 16, 8))
    return rb, n_rows // rb, ctx_rows // rb


def _segment_rows(per_segment):
    return jnp.broadcast_to(per_segment[:, None, :], (2, 8, per_segment.shape[1]))


def _segment_spec(width, cblk):
    return pl.BlockSpec((1, 8, width), lambda i: (jnp.where(i < cblk, 0, 1), 0, 0))


def _segment_sums(parts, row, cblk):
    return jnp.stack([jnp.sum(parts[:cblk, row], axis=0), jnp.sum(parts[cblk:, row], axis=0)])


def _norm_mod_fwd_call(x, gain, scale2, shift2, ctx_rows):
    n_rows, width = x.shape
    rb, nblk, cblk = _norm_blocking(n_rows, ctx_rows)

    def body(x_ref, g_ref, sc_ref, sh_ref, o_ref):
        xv = x_ref[...]
        inv = lax.rsqrt(jnp.mean(xv * xv, axis=1, keepdims=True) + EPS)
        o_ref[...] = xv * inv * (g_ref[0:1, :] * (1.0 + sc_ref[0, 0:1, :])) + sh_ref[0, 0:1, :]

    rows = pl.BlockSpec((rb, width), lambda i: (i, 0))
    return pl.pallas_call(
        body, name="norm_mod_fwd", grid=(nblk,),
        in_specs=[rows, pl.BlockSpec((8, width), lambda i: (0, 0)), _segment_spec(width, cblk), _segment_spec(width, cblk)],
        out_specs=rows, out_shape=jax.ShapeDtypeStruct((n_rows, width), F32),
        compiler_params=_params(("parallel",)),
    )(x, jnp.broadcast_to(gain[None, :], (8, width)), _segment_rows(scale2), _segment_rows(shift2))


def _norm_mod_bwd_call(x, gain, scale2, dh, ctx_rows):
    n_rows, width = x.shape
    rb, nblk, cblk = _norm_blocking(n_rows, ctx_rows)

    def body(x_ref, g_ref, sc_ref, dh_ref, dx_ref, part_ref):
        xv, dh = x_ref[...], dh_ref[...]
        gain_row, one_plus = g_ref[0:1, :], 1.0 + sc_ref[0, 0:1, :]
        inv = lax.rsqrt(jnp.mean(xv * xv, axis=1, keepdims=True) + EPS)
        xhat = xv * inv
        d_xhat = dh * (gain_row * one_plus)
        dx_ref[...] = inv * (d_xhat - xhat * jnp.mean(d_xhat * xhat, axis=1, keepdims=True))
        dh_xhat = dh * xhat
        part_ref[0] = jnp.concatenate([jnp.sum(dh, axis=0, keepdims=True),
                                       jnp.sum(dh_xhat * gain_row, axis=0, keepdims=True),
                                       jnp.sum(dh_xhat * one_plus, axis=0, keepdims=True),
                                       jnp.zeros((5, width), F32)], axis=0)

    rows = pl.BlockSpec((rb, width), lambda i: (i, 0))
    return pl.pallas_call(
        body, name="norm_mod_bwd", grid=(nblk,),
        in_specs=[rows, pl.BlockSpec((8, width), lambda i: (0, 0)), _segment_spec(width, cblk), rows],
        out_specs=[rows, pl.BlockSpec((1, 8, width), lambda i: (i, 0, 0))],
        out_shape=[jax.ShapeDtypeStruct((n_rows, width), F32), jax.ShapeDtypeStruct((nblk, 8, width), F32)],
        compiler_params=_params(("parallel",)),
    )(x, jnp.broadcast_to(gain[None, :], (8, width)), _segment_rows(scale2), dh)


@functools.partial(jax.custom_vjp, nondiff_argnums=(4,))
def norm_mod(x, gain, scale2, shift2, ctx_rows):
    return _norm_mod_fwd_call(x, gain, scale2, shift2, ctx_rows)


def _norm_mod_fwd(x, gain, scale2, shift2, ctx_rows):
    return _norm_mod_fwd_call(x, gain, scale2, shift2, ctx_rows), (x, gain, scale2)


def _norm_mod_bwd(ctx_rows, res, dh):
    x, gain, scale2 = res
    cblk = _norm_blocking(x.shape[0], ctx_rows)[2]
    dx, parts = _norm_mod_bwd_call(x, gain, scale2, dh, ctx_rows)
    return dx, jnp.sum(parts[:, 2], axis=0), _segment_sums(parts, 1, cblk), _segment_sums(parts, 0, cblk)


norm_mod.defvjp(_norm_mod_fwd, _norm_mod_bwd)


def _resid_norm_fwd_call(x, y, gain, gate2, ctx_rows):
    n_rows, width = x.shape
    rb, nblk, cblk = _norm_blocking(n_rows, ctx_rows)

    def body(x_ref, y_ref, g_ref, gate_ref, o_ref):
        yv = y_ref[...]
        inv = lax.rsqrt(jnp.mean(yv * yv, axis=1, keepdims=True) + EPS)
        o_ref[...] = x_ref[...] + gate_ref[0, 0:1, :] * (yv * inv * g_ref[0:1, :])

    rows = pl.BlockSpec((rb, width), lambda i: (i, 0))
    return pl.pallas_call(
        body, name="resid_norm_fwd", grid=(nblk,),
        in_specs=[rows, rows, pl.BlockSpec((8, width), lambda i: (0, 0)), _segment_spec(width, cblk)],
        out_specs=rows, out_shape=jax.ShapeDtypeStruct((n_rows, width), F32),
        compiler_params=_params(("parallel",)),
    )(x, y, jnp.broadcast_to(gain[None, :], (8, width)), _segment_rows(gate2))


def _resid_norm_bwd_call(y, gain, gate2, d_out, ctx_rows):
    n_rows, width = y.shape
    rb, nblk, cblk = _norm_blocking(n_rows, ctx_rows)

    def body(y_ref, g_ref, gate_ref, do_ref, dy_ref, part_ref):
        yv, do = y_ref[...], do_ref[...]
        gain_row, gate_row = g_ref[0:1, :], gate_ref[0, 0:1, :]
        inv = lax.rsqrt(jnp.mean(yv * yv, axis=1, keepdims=True) + EPS)
        yhat = yv * inv
        d_yhat = do * (gate_row * gain_row)
        dy_ref[...] = inv * (d_yhat - yhat * jnp.mean(d_yhat * yhat, axis=1, keepdims=True))
        do_yhat = do * yhat
        part_ref[0] = jnp.concatenate([jnp.sum(do_yhat * gain_row, axis=0, keepdims=True),
                                       jnp.sum(do_yhat * gate_row, axis=0, keepdims=True),
                                       jnp.zeros((6, width), F32)], axis=0)

    rows = pl.BlockSpec((rb, width), lambda i: (i, 0))
    return pl.pallas_call(
        body, name="resid_norm_bwd", grid=(nblk,),
        in_specs=[rows, pl.BlockSpec((8, width), lambda i: (0, 0)), _segment_spec(width, cblk), rows],
        out_specs=[rows, pl.BlockSpec((1, 8, width), lambda i: (i, 0, 0))],
        out_shape=[jax.ShapeDtypeStruct((n_rows, width), F32), jax.ShapeDtypeStruct((nblk, 8, width), F32)],
        compiler_params=_params(("parallel",)),
    )(y, jnp.broadcast_to(gain[None, :], (8, width)), _segment_rows(gate2), d_out)


@functools.partial(jax.custom_vjp, nondiff_argnums=(4,))
def resid_norm(x, y, gain, gate2, ctx_rows):
    return _resid_norm_fwd_call(x, y, gain, gate2, ctx_rows)


def _resid_norm_fwd(x, y, gain, gate2, ctx_rows):
    return _resid_norm_fwd_call(x, y, gain, gate2, ctx_rows), (y, gain, gate2)


def _resid_norm_bwd(ctx_rows, res, d_out):
    y, gain, gate2 = res
    cblk = _norm_blocking(y.shape[0], ctx_rows)[2]
    dy, parts = _resid_norm_bwd_call(y, gain, gate2, d_out, ctx_rows)
    return d_out, dy, jnp.sum(parts[:, 1], axis=0), _segment_sums(parts, 0, cblk)


resid_norm.defvjp(_resid_norm_fwd, _resid_norm_bwd)


HALO = 8


def _ffn_mid_blocking(n_rows, half):
    return _pick(n_rows, (1056, 1024, 256, 128, 64, 32, 16, 8)), _pick(half, (256, 128))


def _ffn_mid_specs(n_rows, rb, tc):
    per = rb // HALO
    return [pl.BlockSpec((rb, tc), lambda j, i: (i, j)),
            pl.BlockSpec((HALO, tc), lambda j, i: (jnp.maximum(i * per - 1, 0), j)),
            pl.BlockSpec((HALO, tc), lambda j, i: (jnp.minimum((i + 1) * per, n_rows // HALO - 1), j))]


def _with_halo(main_ref, prev_ref, next_ref):
    return jnp.concatenate([prev_ref[...], main_ref[...], next_ref[...]], axis=0)


def _segment_edges(shape, first_row, n_rows, starts):
    row = lax.broadcasted_iota(jnp.int32, shape, 0) + first_row
    first = functools.reduce(jnp.logical_or, [row == s for s in starts])
    last = functools.reduce(jnp.logical_or, [row == e - 1 for e in tuple(starts[1:]) + (n_rows,)])
    return first, last


def _row_neighbours(ext, edges):
    first, last = edges
    n = ext.shape[0]
    return jnp.where(first, 0.0, pltpu.roll(ext, 1, 0)), jnp.where(last, 0.0, pltpu.roll(ext, n - 1, 0))


def _ffn_mid_fwd_call(u_a, u_v, taps, starts):
    n_rows, half = u_a.shape
    rb, tc = _ffn_mid_blocking(n_rows, half)
    specs = _ffn_mid_specs(n_rows, rb, tc)
    nj = half // tc

    def body(am, ap, an, vm, vp, vn, wa, wv, o_ref):
        edges = _segment_edges((rb + 2 * HALO, tc), pl.program_id(1) * rb - HALO, n_rows, starts)

        def conv(ext, w):
            above, below = _row_neighbours(ext, edges)
            return above * w[0:1, :] + ext * w[1:2, :] + below * w[2:3, :] + w[3:4, :]

        ca = conv(_with_halo(am, ap, an), wa)[HALO:HALO + rb]
        cv = conv(_with_halo(vm, vp, vn), wv)[HALO:HALO + rb]
        o_ref[...] = ca * jax.nn.sigmoid(ca) * cv

    return pl.pallas_call(
        body, name="ffn_mid_fwd", grid=(nj, n_rows // rb),
        in_specs=specs + specs + [pl.BlockSpec((8, tc), lambda j, i: (0, j)),
                                  pl.BlockSpec((8, tc), lambda j, i: (0, nj + j))],
        out_specs=pl.BlockSpec((rb, tc), lambda j, i: (i, j)),
        out_shape=jax.ShapeDtypeStruct((n_rows, half), F32),
        compiler_params=_params(("parallel", "parallel")),
    )(u_a, u_a, u_a, u_v, u_v, u_v, taps, taps)


def _ffn_mid_bwd_call(u_a, u_v, taps, d_act, starts):
    n_rows, half = u_a.shape
    rb, tc = _ffn_mid_blocking(n_rows, half)
    specs = _ffn_mid_specs(n_rows, rb, tc)
    nj = half // tc
    main = slice(HALO, HALO + rb)

    def body(am, ap, an, vm, vp, vn, wa, wv, dm, dp, dn, dua_ref, duv_ref, dwa_ref, dwv_ref):
        i = pl.program_id(1)
        edges = _segment_edges((rb + 2 * HALO, tc), i * rb - HALO, n_rows, starts)
        neighbours = functools.partial(_row_neighbours, edges=edges)
        ext_a, ext_v, ext_d = _with_halo(am, ap, an), _with_halo(vm, vp, vn), _with_halo(dm, dp, dn)
        above_a, below_a = neighbours(ext_a)
        above_v, below_v = neighbours(ext_v)
        ca = above_a * wa[0:1, :] + ext_a * wa[1:2, :] + below_a * wa[2:3, :] + wa[3:4, :]
        cv = above_v * wv[0:1, :] + ext_v * wv[1:2, :] + below_v * wv[2:3, :] + wv[3:4, :]
        sig = jax.nn.sigmoid(ca)
        d_cv = ext_d * (ca * sig)
        d_ca = ext_d * cv * (sig * (1.0 + ca * (1.0 - sig)))

        def finish(d_c, above, ext, below, w, du_ref, dw_ref):
            d_above, d_below = neighbours(d_c)
            du_ref[...] = (w[1:2, :] * d_c + w[0:1, :] * d_below + w[2:3, :] * d_above)[main]
            d_main = d_c[main]
            sums = jnp.concatenate([jnp.sum(above[main] * d_main, axis=0, keepdims=True),
                                    jnp.sum(ext[main] * d_main, axis=0, keepdims=True),
                                    jnp.sum(below[main] * d_main, axis=0, keepdims=True),
                                    jnp.sum(d_main, axis=0, keepdims=True), jnp.zeros((4, tc), F32)], axis=0)

            @pl.when(i == 0)
            def _():
                dw_ref[...] = sums

            @pl.when(i > 0)
            def _():
                dw_ref[...] += sums

        finish(d_ca, above_a, ext_a, below_a, wa, dua_ref, dwa_ref)
        finish(d_cv, above_v, ext_v, below_v, wv, duv_ref, dwv_ref)

    block = pl.BlockSpec((rb, tc), lambda j, i: (i, j))
    taps_out = pl.BlockSpec((8, tc), lambda j, i: (0, j))
    return pl.pallas_call(
        body, name="ffn_mid_bwd", grid=(nj, n_rows // rb),
        in_specs=specs + specs + [pl.BlockSpec((8, tc), lambda j, i: (0, j)),
                                  pl.BlockSpec((8, tc), lambda j, i: (0, nj + j))] + specs,
        out_specs=[block, block, taps_out, taps_out],
        out_shape=[jax.ShapeDtypeStruct((n_rows, half), F32)] * 2 + [jax.ShapeDtypeStruct((8, half), F32)] * 2,
        compiler_params=_params(("parallel", "arbitrary")),
    )(u_a, u_a, u_a, u_v, u_v, u_v, taps, taps, d_act, d_act, d_act)


def _taps(conv_w, conv_b):
    return jnp.concatenate([conv_w, conv_b[None, :], jnp.zeros((4, conv_w.shape[1]), F32)], axis=0)


@functools.partial(jax.custom_vjp, nondiff_argnums=(4,))
def ffn_mid(u_a, u_v, conv_w, conv_b, starts):
    return _ffn_mid_fwd_call(u_a, u_v, _taps(conv_w, conv_b), starts)


def _ffn_mid_fwd(u_a, u_v, conv_w, conv_b, starts):
    return _ffn_mid_fwd_call(u_a, u_v, _taps(conv_w, conv_b), starts), (u_a, u_v, conv_w, conv_b)


def _ffn_mid_bwd(starts, res, d_act):
    u_a, u_v, conv_w, conv_b = res
    du_a, du_v, dw_a, dw_v = _ffn_mid_bwd_call(u_a, u_v, _taps(conv_w, conv_b), d_act, starts)
    d_taps = jnp.concatenate([dw_a, dw_v], axis=1)
    return du_a, du_v, d_taps[0:3], d_taps[3]


ffn_mid.defvjp(_ffn_mid_fwd, _ffn_mid_bwd)


def _conv_ffn(h, lp, starts):
    u_a = linear(h, lp["ffn_up"], lp["ffn_up_slot"], (True, True))
    u_v = linear(h, lp["ffn_up_v"], lp["ffn_up_v_slot"], (True, True))
    act = ffn_mid(u_a, u_v, lp["ffn_conv_w"], lp["ffn_conv_b"], tuple(starts))
    return linear(act, lp["ffn_down"], lp["ffn_down_slot"], (True, False))


def _local_loss(diff, fixed):
    p = {**fixed, **diff}
    x, ctx = p["x"][0], p["ctx"][0]
    n_lat, ctx_len = x.shape[0], ctx.shape[0]
    depth = p["ada_b"].shape[0]
    rope = _rope_tables(n_lat)
    rows = jnp.concatenate([ctx, x], axis=0)
    cond = jnp.zeros((16, D_MODEL), F32).at[0].set(jax.nn.silu(p["c"][0])).at[1].set(jax.nn.silu(p["c_ctx"]))
    layer_names = [n for n in WEIGHTS if n != "c_ctx"]
    for i in range(depth):
        last = i == depth - 1
        lp = {n: p[n][i] for n in layer_names}
        lp["ffn_up_v"] = p["ffn_up_v"][i]
        lp.update({n + "_slot": p[n + "_slot"][i] for n in MATMUL_WEIGHTS})
        mod = linear(cond, lp["ada_w"], lp["ada_w_slot"], FROM_XLA) + lp["ada_b"]
        m_lat, m_ctx = jnp.split(mod[0:1], 6, axis=-1), jnp.split(mod[1:2], 6, axis=-1)
        seg = [jnp.concatenate([m_ctx[k], m_lat[k]], axis=0) for k in range(6)]

        h = norm_mod(rows, lp["norm_mix_pre"], seg[1], seg[0], ctx_len)
        y = _token_mixer(h, lp, rope, ctx_len, not last)
        if last:
            rows = rows[ctx_len:]
        ctx_rows = 0 if last else ctx_len
        starts = [0] if last else [0, ctx_len]
        rows = resid_norm(rows, y, lp["norm_mix_post"], seg[2], ctx_rows)
        h = norm_mod(rows, lp["norm_ffn_pre"], seg[4], seg[3], ctx_rows)
        rows = resid_norm(rows, _conv_ffn(h, lp, starts), lp["norm_ffn_post"], seg[5], ctx_rows)
    err = jnp.square(rows - p["loss_target"][0])
    return 0.5 * jnp.sum(jnp.mean(err, axis=-1))


def kernel(x, c, ctx, c_ctx, ada_w, ada_b, norm_mix_pre, norm_mix_post, norm_ffn_pre, norm_ffn_post, w_in, q_norm, k_norm, gla_gate_w, gla_gate_b, gla_out_norm, s5_a_re, s5_a_im, s5_log_dt, s5_b_re, s5_b_im, s5_c_re, s5_c_im, s5_d, s5_glu_w, s5_glu_b, w_br_att, w_br_gla, w_br_s5, w_out, ffn_up, ffn_conv_w, ffn_conv_b, ffn_down, loss_target, m_c_ctx, m_ada_w, m_ada_b, m_norm_mix_pre, m_norm_mix_post, m_norm_ffn_pre, m_norm_ffn_post, m_w_in, m_q_norm, m_k_norm, m_gla_gate_w, m_gla_gate_b, m_gla_out_norm, m_s5_a_re, m_s5_a_im, m_s5_log_dt, m_s5_b_re, m_s5_b_im, m_s5_c_re, m_s5_c_im, m_s5_d, m_s5_glu_w, m_s5_glu_b, m_w_br_att, m_w_br_gla, m_w_br_s5, m_w_out, m_ffn_up, m_ffn_conv_w, m_ffn_conv_b, m_ffn_down, v_c_ctx, v_ada_w, v_ada_b, v_norm_mix_pre, v_norm_mix_post, v_norm_ffn_pre, v_norm_ffn_post, v_w_in, v_q_norm, v_k_norm, v_gla_gate_w, v_gla_gate_b, v_gla_out_norm, v_s5_a_re, v_s5_a_im, v_s5_log_dt, v_s5_b_re, v_s5_b_im, v_s5_c_re, v_s5_c_im, v_s5_d, v_s5_glu_w, v_s5_glu_b, v_w_br_att, v_w_br_gla, v_w_br_s5, v_w_out, v_ffn_up, v_ffn_conv_w, v_ffn_conv_b, v_ffn_down):
    args = (x, c, ctx, c_ctx, ada_w, ada_b, norm_mix_pre, norm_mix_post, norm_ffn_pre, norm_ffn_post, w_in, q_norm, k_norm, gla_gate_w, gla_gate_b, gla_out_norm, s5_a_re, s5_a_im, s5_log_dt, s5_b_re, s5_b_im, s5_c_re, s5_c_im, s5_d, s5_glu_w, s5_glu_b, w_br_att, w_br_gla, w_br_s5, w_out, ffn_up, ffn_conv_w, ffn_conv_b, ffn_down)
    given = dict(zip(FWD_INPUTS, args))
    given["loss_target"] = loss_target
    m_in = dict(zip(WEIGHTS, (m_c_ctx, m_ada_w, m_ada_b, m_norm_mix_pre, m_norm_mix_post, m_norm_ffn_pre, m_norm_ffn_post, m_w_in, m_q_norm, m_k_norm, m_gla_gate_w, m_gla_gate_b, m_gla_out_norm, m_s5_a_re, m_s5_a_im, m_s5_log_dt, m_s5_b_re, m_s5_b_im, m_s5_c_re, m_s5_c_im, m_s5_d, m_s5_glu_w, m_s5_glu_b, m_w_br_att, m_w_br_gla, m_w_br_s5, m_w_out, m_ffn_up, m_ffn_conv_w, m_ffn_conv_b, m_ffn_down)))
    v_in = dict(zip(WEIGHTS, (v_c_ctx, v_ada_w, v_ada_b, v_norm_mix_pre, v_norm_mix_post, v_norm_ffn_pre, v_norm_ffn_post, v_w_in, v_q_norm, v_k_norm, v_gla_gate_w, v_gla_gate_b, v_gla_out_norm, v_s5_a_re, v_s5_a_im, v_s5_log_dt, v_s5_b_re, v_s5_b_im, v_s5_c_re, v_s5_c_im, v_s5_d, v_s5_glu_w, v_s5_glu_b, v_w_br_att, v_w_br_gla, v_w_br_s5, v_w_out, v_ffn_up, v_ffn_conv_w, v_ffn_conv_b, v_ffn_down)))
    depth = ada_b.shape[0]
    big_names, small_names = list(BIG), list(SMALL_SHARDED)
    sharded_names = big_names + small_names

    full = {}
    groups = {}
    for n in big_names:
        groups.setdefault(_round_up(given[n].shape[2], 128), []).append(n)
    per_layer = {n: [] for n in big_names}
    for i in range(depth):
        for width, names in groups.items():
            gathered = all_gather_blocks(_pack_rows([given[n][i] for n in names], width, BF16),
                                         f"gather_weights_{width}")
            for n, blocks in zip(names, _unpack_rows(gathered, [given[n].shape[1:] for n in names])):
                per_layer[n].append(_to_full(blocks, BIG[n]))
    for n in big_names:
        full[n] = jnp.stack(per_layer[n])
    small_shapes = [given[n].shape for n in small_names]
    gathered = all_gather_blocks(_pack([given[n] for n in small_names], F32), "gather_small")
    for n, blocks in zip(small_names, _unpack(gathered, small_shapes)):
        full[n] = _to_full(blocks, SMALL_SHARDED[n] + 1)

    w_in_full = jnp.pad(_w_in_reorder(full["w_in"], True), ((0, 0), (0, 0), (0, D_IN_PAD - D_IN)))

    diff = {"x": x}
    diff.update({n: given[n] for n in REPLICATED})
    diff.update({n: full[n] for n in small_names})
    fixed = {"c": c, "ctx": ctx, "loss_target": loss_target}
    d_ff = full["ffn_up"].shape[2] // 2
    for n in big_names:
        fixed[n] = w_in_full if n == "w_in" else full[n]
    fixed["ffn_up"], fixed["ffn_up_v"] = full["ffn_up"][:, :, :d_ff], full["ffn_up"][:, :, d_ff:]
    for n in MATMUL_WEIGHTS:
        diff[n + "_slot"] = jnp.zeros(fixed[n].shape, F32)
    loss_local, grads = jax.value_and_grad(_local_loss)(diff, fixed)
    loss = lax.psum(loss_local, ("x", "y", "c"))

    g_full = {n: grads[n] for n in small_names}
    for n in big_names:
        g_full[n] = grads[n + "_slot"]
    g_full["ffn_up"] = jnp.concatenate([grads["ffn_up_slot"], grads["ffn_up_v_slot"]], axis=2)
    g_full["w_in"] = _w_in_reorder(g_full["w_in"], False)

    out_g, out_d, out_m, out_v = {}, {}, {}, {}
    layer_out = {n: ([], [], [], []) for n in big_names}
    for i in range(depth):
        for width, names in groups.items():
            send = _pack_rows([_to_blocks(g_full[n][i], BIG[n]) for n in names], width, BF16)
            parts = exchange_blocks(send, f"exchange_grads_{width}")
            results = sum_adamw(parts, _pack_rows([given[n][i] for n in names], width, F32),
                                _pack_rows([m_in[n][i] for n in names], width, F32),
                                _pack_rows([v_in[n][i] for n in names], width, F32), f"adamw_{width}")
            for k, res in enumerate(results):
                for n, arr in zip(names, _unpack_rows(res, [given[n].shape[1:] for n in names])):
                    layer_out[n][k].append(arr)
    for n in big_names:
        out_g[n], out_d[n], out_m[n], out_v[n] = (jnp.stack(parts_k) for parts_k in layer_out[n])
    send = _pack_blocks([_to_blocks(g_full[n], SMALL_SHARDED[n] + 1) for n in small_names], F32)
    parts = exchange_blocks(send, "exchange_small")
    results = sum_adamw(parts, _pack([given[n] for n in small_names], F32), _pack([m_in[n] for n in small_names], F32),
                        _pack([v_in[n] for n in small_names], F32), "adamw_small")
    for store, res in zip((out_g, out_d, out_m, out_v), results):
        for n, arr in zip(small_names, _unpack(res, small_shapes)):
            store[n] = arr

    rep_shapes = [given[n].shape for n in REPLICATED]
    parts = all_gather_blocks(_pack([grads[n] for n in REPLICATED], F32), "gather_rep_grads")
    results = sum_adamw(parts, _pack([given[n] for n in REPLICATED], F32), _pack([m_in[n] for n in REPLICATED], F32),
                        _pack([v_in[n] for n in REPLICATED], F32), "adamw_replicated")
    for store, res in zip((out_g, out_d, out_m, out_v), results):
        for n, arr in zip(REPLICATED, _unpack(res, rep_shapes)):
            store[n] = arr

    return (loss, grads["x"], *[out_g[n] for n in WEIGHTS], *[out_d[n] for n in WEIGHTS],
            *[out_m[n] for n in WEIGHTS], *[out_v[n] for n in WEIGHTS])
```
